```python
import math
import jax, jax.numpy as jnp
from jax import lax
import numpy as np

D_MODEL = 1024
BATCH = 4
SEQ = 8192
DEPTH = 2

CTX_LEN = 256
GRID_W = 64
Q_BLOCK = 128
ROPE_THETA = 10000.0
NORM_EPS = 1e-6

N_BRANCH = 4
BRANCH_W = 512

S5_WIDTH = 512
S5_GROUP = 16
S5_GROUPS = S5_WIDTH // S5_GROUP
S5_STATE = 64

DA_HEADS = 4
DA_DK = 64
DA_DV = 2 * DA_DK

GQ_HEADS = 4
GQ_KV = 2
GQ_DH = 128

ML_HEADS = 4
ML_QRANK = 256
ML_KVRANK = 128
ML_NOPE = 128
ML_ROPE = 64
ML_DV = 128

D_FF = ((8 * D_MODEL + 3 * 256 - 1) // (3 * 256)) * 256

COLS_S5 = S5_WIDTH
COLS_DA = 3 * DA_HEADS * DA_DV
COLS_GQ = (GQ_HEADS + 2 * GQ_KV) * GQ_DH
COLS_ML = ML_QRANK + ML_KVRANK + ML_ROPE
OFF_DA = COLS_S5
OFF_GQ = OFF_DA + COLS_DA
OFF_ML = OFF_GQ + COLS_GQ
MIX_COLS = OFF_ML + COLS_ML
IN_COLS = MIX_COLS + N_BRANCH * D_MODEL

kernel_name = "hybrid_gated_s5_diffattn_gqa_mla_dit_block"


def rmsnorm(x, gain):
    xf = x.astype(jnp.float32)
    y = xf * lax.rsqrt(jnp.mean(xf * xf, axis=-1, keepdims=True) + NORM_EPS)
    return (y * gain.astype(jnp.float32)).astype(x.dtype)


def modulate(h, shift, scale):
    return h * (1.0 + scale) + shift


def axial_rope(rows, rot_dim):
    t = jnp.arange(rows * GRID_W)
    r = (t // GRID_W).astype(jnp.float32)
    col = (t % GRID_W).astype(jnp.float32)
    half = rot_dim // 2
    inv = ROPE_THETA ** (-jnp.arange(0, half, 2, dtype=jnp.float32) / half)
    ang = jnp.concatenate([r[:, None] * inv, col[:, None] * inv], axis=-1)
    return jnp.cos(ang), jnp.sin(ang)


def apply_rope(x, cos, sin):
    xf = x.astype(jnp.float32).reshape(x.shape[:-1] + (-1, 2))
    x0, x1 = xf[..., 0], xf[..., 1]
    out = jnp.stack([x0 * cos - x1 * sin, x0 * sin + x1 * cos], axis=-1)
    return out.reshape(x.shape).astype(x.dtype)


def _attend_block(q, k, v, scale):
    s = jnp.einsum('bhgqd,bhkd->bhgqk', q.astype(jnp.float32), k.astype(jnp.float32)) * scale
    p = jax.nn.softmax(s, axis=-1)
    return jnp.einsum('bhgqk,bhkd->bhgqd', p, v.astype(jnp.float32))


def attention(q, k, v, scale):
    b, hk, g, lq, dk = q.shape
    nb = lq // Q_BLOCK
    qb = jnp.moveaxis(q.reshape(b, hk, g, nb, Q_BLOCK, dk), 3, 0)
    out = lax.map(lambda qq: _attend_block(qq, k, v, scale), qb)
    return jnp.moveaxis(out, 0, 3).reshape(b, hk, g, lq, v.shape[-1])


def s5_discretise(lam_re, lam_im, log_step, b_re, b_im):
    lam_re = lam_re.astype(jnp.float32)
    lam_im = lam_im.astype(jnp.float32)
    dt = jnp.exp(log_step.astype(jnp.float32))[:, None]
    mag = jnp.exp(lam_re * dt)
    ab_re, ab_im = mag * jnp.cos(lam_im * dt), mag * jnp.sin(lam_im * dt)
    den = lam_re * lam_re + lam_im * lam_im
    nr, ni = ab_re - 1.0, ab_im
    coef_re = (nr * lam_re + ni * lam_im) / den
    coef_im = (ni * lam_re - nr * lam_im) / den
    b_re = b_re.astype(jnp.float32)
    b_im = b_im.astype(jnp.float32)
    bb_re = coef_re[..., None] * b_re - coef_im[..., None] * b_im
    bb_im = coef_re[..., None] * b_im + coef_im[..., None] * b_re
    return ab_re, ab_im, bb_re, bb_im


def _complex_affine_combine(e1, e2):
    a1r, a1i, b1r, b1i = e1
    a2r, a2i, b2r, b2i = e2
    return (a2r * a1r - a2i * a1i,
            a2r * a1i + a2i * a1r,
            a2r * b1r - a2i * b1i + b2r,
            a2r * b1i + a2i * b1r + b2i)


def s5_scan(u, ab_re, ab_im, bb_re, bb_im, h0_re, h0_im, reverse):
    bu_re = jnp.einsum('blgc,gpc->blgp', u, bb_re)
    bu_im = jnp.einsum('blgc,gpc->blgp', u, bb_im)
    if reverse:
        bu_re, bu_im = jnp.flip(bu_re, axis=1), jnp.flip(bu_im, axis=1)
    if h0_re is not None:
        bu_re = bu_re.at[:, 0].add(ab_re * h0_re - ab_im * h0_im)
        bu_im = bu_im.at[:, 0].add(ab_re * h0_im + ab_im * h0_re)
    n = u.shape[1]
    a_re = jnp.broadcast_to(ab_re[None, None], (1, n) + ab_re.shape)
    a_im = jnp.broadcast_to(ab_im[None, None], (1, n) + ab_im.shape)
    _, _, h_re, h_im = lax.associative_scan(_complex_affine_combine, (a_re, a_im, bu_re, bu_im), axis=1)
    f_re, f_im = h_re[:, -1], h_im[:, -1]
    if reverse:
        h_re, h_im = jnp.flip(h_re, axis=1), jnp.flip(h_im, axis=1)
    return h_re, h_im, f_re, f_im


def s5_readout(h_re, h_im, c_re, c_im):
    return (jnp.einsum('blgp,gcp->blgc', h_re, c_re.astype(jnp.float32))
            - jnp.einsum('blgp,gcp->blgc', h_im, c_im.astype(jnp.float32)))


def s5_branch(ul, uc, lam_re, lam_im, log_step, b_re, b_im, c_re, c_im, d, w_glu, b_glu, ctx_out):
    def groups(u):
        return u.astype(jnp.float32).reshape(u.shape[0], u.shape[1], S5_GROUPS, S5_GROUP)
    gl, gc = groups(ul), groups(uc)
    d_g = d.astype(jnp.float32).reshape(S5_GROUPS, S5_GROUP)
    y_l = d_g * gl
    y_c = d_g * gc if ctx_out else None
    for direction, reverse in ((0, False), (1, True)):
        disc = s5_discretise(lam_re[direction], lam_im[direction], log_step[direction],
                             b_re[direction], b_im[direction])
        hc_re, hc_im, fc_re, fc_im = s5_scan(gc, *disc, None, None, reverse)
        hl_re, hl_im, _, _ = s5_scan(gl, *disc, fc_re, fc_im, reverse)
        y_l = y_l + s5_readout(hl_re, hl_im, c_re[direction], c_im[direction])
        if ctx_out:
            y_c = y_c + s5_readout(hc_re, hc_im, c_re[direction], c_im[direction])

    def glu(y):
        g = jax.nn.gelu(y.reshape(y.shape[0], y.shape[1], S5_WIDTH))
        gate = jax.nn.sigmoid(g @ w_glu.astype(jnp.float32) + b_glu.astype(jnp.float32))
        return (g * gate).astype(ul.dtype)

    return glu(y_l), (glu(y_c) if ctx_out else None)


def diff_attn_branch(zl, zc, q_norm, k_norm, lam, subln, lam_init, cos, sin, ctx_out):
    def project(z, rope):
        b, l, _ = z.shape
        q, k, v = jnp.split(z, 3, axis=-1)
        q = rmsnorm(q.reshape(b, l, DA_HEADS, 2, DA_DK).transpose(0, 2, 3, 1, 4), q_norm)
        k = rmsnorm(k.reshape(b, l, DA_HEADS, 2, DA_DK).transpose(0, 2, 3, 1, 4), k_norm)
        v = v.reshape(b, l, DA_HEADS, DA_DV).transpose(0, 2, 1, 3)
        if rope:
            q, k = apply_rope(q, cos, sin), apply_rope(k, cos, sin)
        return q, k, v

    lam_f = lam.astype(jnp.float32)
    lam_full = jnp.exp(jnp.sum(lam_f[0] * lam_f[1])) - jnp.exp(jnp.sum(lam_f[2] * lam_f[3])) + lam_init
    scale = DA_DK ** -0.5

    def diff(q, k, v):
        a1 = attention(q[:, :, 0:1], k[:, :, 0], v, scale)[:, :, 0]
        a2 = attention(q[:, :, 1:2], k[:, :, 1], v, scale)[:, :, 0]
        o = rmsnorm(a1 - lam_full * a2, subln) * (1.0 - lam_init)
        b, h, l, dv = o.shape
        return o.transpose(0, 2, 1, 3).reshape(b, l, h * dv).astype(zl.dtype)

    qc, kc, vc = project(zc, False)
    ql, kl, vl = project(zl, True)
    out_l = diff(ql, jnp.concatenate([kc, kl], axis=3), jnp.concatenate([vc, vl], axis=2))
    out_c = diff(qc, kc, vc) if ctx_out else None
    return out_l, out_c


def gqa_branch(zl, zc, q_norm, k_norm, cos, sin, ctx_out):
    group = GQ_HEADS // GQ_KV

    def project(z, rope):
        b, l, _ = z.shape
        q, k, v = jnp.split(z, [GQ_HEADS * GQ_DH, (GQ_HEADS + GQ_KV) * GQ_DH], axis=-1)
        q = rmsnorm(q.reshape(b, l, GQ_KV, group, GQ_DH).transpose(0, 2, 3, 1, 4), q_norm)
        k = rmsnorm(k.reshape(b, l, GQ_KV, GQ_DH).transpose(0, 2, 1, 3), k_norm)
        v = v.reshape(b, l, GQ_KV, GQ_DH).transpose(0, 2, 1, 3)
        if rope:
            q, k = apply_rope(q, cos, sin), apply_rope(k, cos, sin)
        return q, k, v

    def attend(q, k, v):
        o = attention(q, k, v, GQ_DH ** -0.5)
        b, kv, g, l, dh = o.shape
        return o.transpose(0, 3, 1, 2, 4).reshape(b, l, kv * g * dh).astype(zl.dtype)

    qc, kc, vc = project(zc, False)
    ql, kl, vl = project(zl, True)
    out_l = attend(ql, jnp.concatenate([kc, kl], axis=2), jnp.concatenate([vc, vl], axis=2))
    out_c = attend(qc, kc, vc) if ctx_out else None
    return out_l, out_c


def mla_branch(zl, zc, cq_norm, ckv_norm, w_uq, w_ukv, q_norm, k_norm, cos, sin, ctx_out):
    dqk = ML_NOPE + ML_ROPE

    def project(z, rope):
        b, l, _ = z.shape
        cq, ckv, kr = jnp.split(z, [ML_QRANK, ML_QRANK + ML_KVRANK], axis=-1)
        q = (rmsnorm(cq, cq_norm) @ w_uq).reshape(b, l, ML_HEADS, dqk).transpose(0, 2, 1, 3)
        kv = (rmsnorm(ckv, ckv_norm) @ w_ukv).reshape(b, l, ML_HEADS, ML_NOPE + ML_DV).transpose(0, 2, 1, 3)
        q_nope = rmsnorm(q[..., :ML_NOPE], q_norm[:ML_NOPE])
        q_rope = rmsnorm(q[..., ML_NOPE:], q_norm[ML_NOPE:])
        k_nope = rmsnorm(kv[..., :ML_NOPE], k_norm[:ML_NOPE])
        v = kv[..., ML_NOPE:]
        k_rope = rmsnorm(kr, k_norm[ML_NOPE:])[:, None]
        if rope:
            q_rope, k_rope = apply_rope(q_rope, cos, sin), apply_rope(k_rope, cos, sin)
        q = jnp.concatenate([q_nope, q_rope], axis=-1)
        k = jnp.concatenate([k_nope, jnp.broadcast_to(k_rope, k_nope.shape[:-1] + (ML_ROPE,))], axis=-1)
        return q, k, v

    def attend(q, k, v):
        o = attention(q[:, :, None], k, v, dqk ** -0.5)[:, :, 0]
        b, h, l, dv = o.shape
        return o.transpose(0, 2, 1, 3).reshape(b, l, h * dv).astype(zl.dtype)

    qc, kc, vc = project(zc, False)
    ql, kl, vl = project(zl, True)
    out_l = attend(ql, jnp.concatenate([kc, kl], axis=2), jnp.concatenate([vc, vl], axis=2))
    out_c = attend(qc, kc, vc) if ctx_out else None
    return out_l, out_c


def merge_branches(branches, gate_logits, w_branch, w_out):
    b, l, _ = gate_logits.shape
    gates = jax.nn.sigmoid(gate_logits.reshape(b, l, N_BRANCH, D_MODEL))
    acc = gates[:, :, 0] * (branches[0] @ w_branch[0])
    for n in range(1, N_BRANCH):
        acc = acc + gates[:, :, n] * (branches[n] @ w_branch[n])
    return acc @ w_out


def swiglu(h, w_in, w_out):
    gate, up = jnp.split(h @ w_in, 2, axis=-1)
    return (jax.nn.silu(gate) * up) @ w_out


def setup_inputs(seed: int = 0) -> dict:
    key = jax.random.key(seed)
    keys = jax.random.split(key, 40)
    counter = iter(range(40))

    def nrm(shape, scale):
        return jax.random.normal(keys[next(counter)], shape, jnp.float32) * scale

    def gain(shape):
        return 1.0 + 0.02 * jax.random.normal(keys[next(counter)], shape, jnp.float32)

    L, G, P, C = DEPTH, S5_GROUPS, S5_STATE, S5_GROUP
    n_idx = jnp.arange(P, dtype=jnp.float32)
    return {
        "x": nrm((BATCH, SEQ, D_MODEL), 1.0),
        "c": nrm((BATCH, D_MODEL), 1.0),
        "ctx": nrm((BATCH, CTX_LEN, D_MODEL), 1.0),
        "c_ctx": nrm((D_MODEL,), 1.0),
        "w_mod": nrm((L, D_MODEL, 6 * D_MODEL), 0.3 * D_MODEL ** -0.5),
        "b_mod": nrm((L, 6 * D_MODEL), 0.01),
        "norm_mix": gain((L, D_MODEL)),
        "w_in": nrm((L, D_MODEL, IN_COLS), D_MODEL ** -0.5),
        "s5_lam_re": -0.5 + nrm((L, 2, G, P), 0.01),
        "s5_lam_im": math.pi * n_idx + nrm((L, 2, G, P), 0.01),
        "s5_log_step": jax.random.uniform(keys[next(counter)], (L, 2, G), jnp.float32,
                                          math.log(1e-3), math.log(1e-1)),
        "s5_b_re": nrm((L, 2, G, P, C), (2 * C) ** -0.5),
        "s5_b_im": nrm((L, 2, G, P, C), (2 * C) ** -0.5),
        "s5_c_re": nrm((L, 2, G, C, P), (2 * P) ** -0.5),
        "s5_c_im": nrm((L, 2, G, C, P), (2 * P) ** -0.5),
        "s5_d": nrm((L, S5_WIDTH), 1.0),
        "s5_w_glu": nrm((L, S5_WIDTH, S5_WIDTH), S5_WIDTH ** -0.5),
        "s5_b_glu": nrm((L, S5_WIDTH), 0.01),
        "da_q_norm": gain((L, DA_DK)),
        "da_k_norm": gain((L, DA_DK)),
        "da_lam": nrm((L, 4, DA_DK), 0.1),
        "da_subln": gain((L, DA_DV)),
        "gq_q_norm": gain((L, GQ_DH)),
        "gq_k_norm": gain((L, GQ_DH)),
        "ml_cq_norm": gain((L, ML_QRANK)),
        "ml_ckv_norm": gain((L, ML_KVRANK)),
        "ml_w_uq": nrm((L, ML_QRANK, ML_HEADS * (ML_NOPE + ML_ROPE)), ML_QRANK ** -0.5),
        "ml_w_ukv": nrm((L, ML_KVRANK, ML_HEADS * (ML_NOPE + ML_DV)), ML_KVRANK ** -0.5),
        "ml_q_norm": gain((L, ML_NOPE + ML_ROPE)),
        "ml_k_norm": gain((L, ML_NOPE + ML_ROPE)),
        "w_branch": nrm((L, N_BRANCH, BRANCH_W, D_MODEL), BRANCH_W ** -0.5),
        "w_out": nrm((L, D_MODEL, D_MODEL), D_MODEL ** -0.5),
        "norm_ffn": gain((L, D_MODEL)),
        "w_ffn_in": nrm((L, D_MODEL, 2 * D_FF), D_MODEL ** -0.5),
        "w_ffn_out": nrm((L, D_FF, D_MODEL), D_FF ** -0.5),
    }


def reference(x, c, ctx, c_ctx, w_mod, b_mod, norm_mix, w_in, s5_lam_re, s5_lam_im, s5_log_step,
              s5_b_re, s5_b_im, s5_c_re, s5_c_im, s5_d, s5_w_glu, s5_b_glu, da_q_norm, da_k_norm,
              da_lam, da_subln, gq_q_norm, gq_k_norm, ml_cq_norm, ml_ckv_norm, ml_w_uq, ml_w_ukv,
              ml_q_norm, ml_k_norm, w_branch, w_out, norm_ffn, w_ffn_in, w_ffn_out):
    rows = x.shape[1] // GRID_W
    cos64, sin64 = axial_rope(rows, DA_DK)
    cos128, sin128 = axial_rope(rows, GQ_DH)
    splits = [OFF_DA, OFF_GQ, OFF_ML, MIX_COLS]
    for i in range(DEPTH):
        ctx_out = i < DEPTH - 1
        mod_l = (jax.nn.silu(c) @ w_mod[i] + b_mod[i])[:, None, :]
        mod_c = jax.nn.silu(c_ctx) @ w_mod[i] + b_mod[i]
        sh1l, sc1l, g1l, sh2l, sc2l, g2l = jnp.split(mod_l, 6, axis=-1)
        sh1c, sc1c, g1c, sh2c, sc2c, g2c = jnp.split(mod_c, 6, axis=-1)

        hl = modulate(rmsnorm(x, norm_mix[i]), sh1l, sc1l)
        hc = modulate(rmsnorm(ctx, norm_mix[i]), sh1c, sc1c)
        zl = hl @ w_in[i]
        zc = hc @ (w_in[i] if ctx_out else w_in[i][:, :MIX_COLS])
        s5l, dal, gql, mll, gate_l = jnp.split(zl, splits, axis=-1)
        s5c, dac, gqc, mlc, gate_c = jnp.split(zc, splits, axis=-1)

        a_l, a_c = s5_branch(s5l, s5c, s5_lam_re[i], s5_lam_im[i], s5_log_step[i], s5_b_re[i], s5_b_im[i],
                             s5_c_re[i], s5_c_im[i], s5_d[i], s5_w_glu[i], s5_b_glu[i], ctx_out)
        lam_init = 0.8 - 0.6 * math.exp(-0.3 * i)
        b_l, b_c = diff_attn_branch(dal, dac, da_q_norm[i], da_k_norm[i], da_lam[i], da_subln[i],
                                    lam_init, cos64, sin64, ctx_out)
        c_l, c_c = gqa_branch(gql, gqc, gq_q_norm[i], gq_k_norm[i], cos128, sin128, ctx_out)
        d_l, d_c = mla_branch(mll, mlc, ml_cq_norm[i], ml_ckv_norm[i], ml_w_uq[i], ml_w_ukv[i],
                              ml_q_norm[i], ml_k_norm[i], cos64, sin64, ctx_out)

        x = x + g1l * merge_branches((a_l, b_l, c_l, d_l), gate_l, w_branch[i], w_out[i])
        x = x + g2l * swiglu(modulate(rmsnorm(x, norm_ffn[i]), sh2l, sc2l), w_ffn_in[i], w_ffn_out[i])
        if ctx_out:
            ctx = ctx + g1c * merge_branches((a_c, b_c, c_c, d_c), gate_c, w_branch[i], w_out[i])
            ctx = ctx + g2c * swiglu(modulate(rmsnorm(ctx, norm_ffn[i]), sh2c, sc2c), w_ffn_in[i], w_ffn_out[i])
    return x
```

```python
import functools
import math

import numpy as np
import jax
import jax.numpy as jnp
from jax import lax
from jax.experimental import pallas as pl
from jax.experimental.pallas import tpu as pltpu

GRID_W = 64
ROPE_THETA = 10000.0
NORM_EPS = 1e-6

S5_GROUP = 16
S5_STATE = 64
S5_CHUNK = 16
S5_ROWS = 8

DA_HEADS = 4
DA_DK = 64
DA_DV = 128
GQ_HEADS = 4
GQ_KV = 2
GQ_DH = 128
ML_HEADS = 4
ML_QRANK = 256
ML_KVRANK = 128
ML_NOPE = 128
ML_ROPE = 64
ML_DV = 128
N_BRANCH = 4
BRANCH_W = 512

LANES = 128
ROW_TILE = 256
VMEM_LIMIT = 56 * 1024 * 1024

F32 = jnp.float32
BF16 = jnp.bfloat16


def _deinterleave(n):
    return np.concatenate([np.arange(0, n, 2), np.arange(1, n, 2)])


def _cparams(sem):
    return pltpu.CompilerParams(dimension_semantics=sem, vmem_limit_bytes=VMEM_LIMIT)


def _const_spec(shape):
    nd = len(shape)
    return pl.BlockSpec(shape, lambda *_: (0,) * nd)


def _mod_kernel(cc_ref, w_ref, b_ref, o_ref):
    a = cc_ref[...]
    a = a * jax.nn.sigmoid(a)
    o_ref[0] = jnp.dot(a, w_ref[0], preferred_element_type=F32,
                       precision=lax.Precision.HIGHEST) + b_ref[0]


def _modulation(cc, w_mod, b_mod):
    depth, d, n = w_mod.shape
    tn = n // 4
    return pl.pallas_call(
        _mod_kernel,
        grid=(depth, n // tn),
        in_specs=[pl.BlockSpec((8, d), lambda i, j: (0, 0)),
                  pl.BlockSpec((1, d, tn), lambda i, j: (i, 0, j)),
                  pl.BlockSpec((1, 1, tn), lambda i, j: (i, 0, j))],
        out_specs=pl.BlockSpec((1, 8, tn), lambda i, j: (i, 0, j)),
        out_shape=jax.ShapeDtypeStruct((depth, 8, n), F32),
        compiler_params=_cparams(("arbitrary", "arbitrary")),
        name="modulation",
    )(cc, w_mod, b_mod.reshape(depth, 1, n))


def _seg_rms(z, seg):
    width = z.shape[-1]
    if seg == 2 * LANES:
        outs = []
        for g in range(width // seg):
            zg = z[:, g * seg:(g + 1) * seg]
            ms = jnp.sum(zg * zg, axis=-1, keepdims=True) * (1.0 / seg)
            outs.append(zg * lax.rsqrt(ms + NORM_EPS))
        return outs[0] if len(outs) == 1 else jnp.concatenate(outs, axis=-1)
    outs = []
    for g in range(width // LANES):
        zg = z[:, g * LANES:(g + 1) * LANES]
        zz = zg * zg
        if seg == LANES:
            ms = jnp.sum(zz, axis=-1, keepdims=True) * (1.0 / seg)
        else:
            lo = lax.broadcasted_iota(jnp.int32, zz.shape, 1) < seg
            s_lo = jnp.sum(jnp.where(lo, zz, 0.0), axis=-1, keepdims=True)
            s_hi = jnp.sum(jnp.where(lo, 0.0, zz), axis=-1, keepdims=True)
            ms = jnp.where(lo, s_lo, s_hi) * (1.0 / seg)
        outs.append(zg * lax.rsqrt(ms + NORM_EPS))
    return outs[0] if len(outs) == 1 else jnp.concatenate(outs, axis=-1)


def _rope(x, cos, sin, unit):
    outs = []
    for g in range(x.shape[-1] // LANES):
        xg = x[:, g * LANES:(g + 1) * LANES]
        if unit == LANES:
            rot = pltpu.roll(xg, LANES // 2, 1)
        else:
            lane = lax.broadcasted_iota(jnp.int32, xg.shape, 1)
            rot = jnp.where((lane & (unit // 2)) == 0,
                            pltpu.roll(xg, LANES - unit // 2, 1), pltpu.roll(xg, unit // 2, 1))
        outs.append(xg * cos + rot * sin)
    return outs[0] if len(outs) == 1 else jnp.concatenate(outs, axis=-1)


_C_S5 = 0
_C_DAQ = 512
_C_DAK = 1024
_C_DAV = 1536
_C_GQQ = 2048
_C_GQK = 2560
_C_GQV = 2816
_C_ML = 3072
_C_GATE = 3584
_C_END = 3584 + 4096

_G_DAQ, _G_DAK, _G_GQQ, _G_GQK = 0, 512, 1024, 1536
_G_CQ, _G_CKV, _G_QN, _G_QR, _G_KN, _G_KR, _G_END = 1792, 2048, 2176, 2688, 2944, 3456, 3584


def _inproj_kernel(x_ref, mod_ref, nrm_ref, gains_ref, rope_ref, w_ref, wuq_ref, wukv_ref,
                   u_ref, daq_ref, dak_ref, dav_ref, gqq_ref, gqk_ref, gqv_ref,
                   mlq_ref, mlk_ref, mlv_ref, gate_ref):
    d = x_ref.shape[-1]
    x = x_ref[0]
    ms = jnp.mean(x * x, axis=-1, keepdims=True)
    mod = mod_ref[0]
    shift, scale = mod[:, 0:d], mod[:, d:2 * d]
    h = (x * lax.rsqrt(ms + NORM_EPS) * nrm_ref[...]) * (1.0 + scale) + shift
    hb = h.astype(BF16)

    def proj(c0, c1):
        return jnp.dot(hb, w_ref[:, c0:c1], preferred_element_type=F32)

    def gain(g0, g1):
        return gains_ref[:, g0:g1]

    cos64, sin64 = rope_ref[:, 0:128], rope_ref[:, 128:256]
    cos128, sin128 = rope_ref[:, 256:384], rope_ref[:, 384:512]

    u_ref[0] = proj(_C_S5, _C_DAQ).astype(BF16)

    q = _seg_rms(proj(_C_DAQ, _C_DAK), DA_DK) * gain(_G_DAQ, _G_DAK)
    daq_ref[0] = _rope(q, cos64, sin64, DA_DK).astype(BF16)
    k = _seg_rms(proj(_C_DAK, _C_DAV), DA_DK) * gain(_G_DAK, _G_GQQ)
    dak_ref[0] = _rope(k, cos64, sin64, DA_DK).astype(BF16)
    dav_ref[0] = proj(_C_DAV, _C_GQQ).astype(BF16)

    q = _seg_rms(proj(_C_GQQ, _C_GQK), GQ_DH) * gain(_G_GQQ, _G_GQK)
    gqq_ref[0] = _rope(q, cos128, sin128, GQ_DH).astype(BF16)
    k = _seg_rms(proj(_C_GQK, _C_GQV), GQ_DH) * gain(_G_GQK, _G_CQ)
    gqk_ref[0] = _rope(k, cos128, sin128, GQ_DH).astype(BF16)
    gqv_ref[0] = proj(_C_GQV, _C_ML).astype(BF16)

    zc = proj(_C_ML, _C_GATE)
    cq = (_seg_rms(zc[:, 0:256], 256) * gain(_G_CQ, _G_CKV)).astype(BF16)
    ckv = (_seg_rms(zc[:, 256:384], 128) * gain(_G_CKV, _G_QN)).astype(BF16)
    qq = jnp.dot(cq, wuq_ref[...], preferred_element_type=F32)
    kv = jnp.dot(ckv, wukv_ref[...], preferred_element_type=F32)
    qn = (_seg_rms(qq[:, 0:512], ML_NOPE) * gain(_G_QN, _G_QR)).astype(BF16)
    qr = _seg_rms(qq[:, 512:768], ML_ROPE) * gain(_G_QR, _G_KN)
    qr = _rope(qr, cos64, sin64, ML_ROPE)
    kn = (_seg_rms(kv[:, 0:512], ML_NOPE) * gain(_G_KN, _G_KR)).astype(BF16)
    kr = _seg_rms(zc[:, 384:512], ML_ROPE) * gain(_G_KR, _G_END)
    kr = _rope(kr, cos64, sin64, ML_ROPE).astype(BF16)
    lo = lax.broadcasted_iota(jnp.int32, (x.shape[0], LANES), 1) < ML_ROPE
    for hd in range(ML_HEADS):
        pair = qr[:, (hd // 2) * LANES:(hd // 2 + 1) * LANES]
        keep = lo if hd % 2 == 0 else jnp.logical_not(lo)
        mlq_ref[0, :, hd * 256:hd * 256 + 128] = qn[:, hd * 128:(hd + 1) * 128]
        mlq_ref[0, :, hd * 256 + 128:(hd + 1) * 256] = jnp.where(keep, pair, 0.0).astype(BF16)
        mlk_ref[0, :, hd * 256:hd * 256 + 128] = kn[:, hd * 128:(hd + 1) * 128]
        mlk_ref[0, :, hd * 256 + 128:(hd + 1) * 256] = kr
    mlv_ref[0] = kv[:, 512:1024].astype(BF16)

    for j in range(N_BRANCH):
        c0 = _C_GATE + j * d
        gate_ref[0, :, j * d:(j + 1) * d] = jax.nn.sigmoid(proj(c0, c0 + d)).astype(BF16)


def _inproj(xc, mod_rows, layer, n_batch, n_lat, nrm, gains, rope, w, wuq, wukv):
    b, lt, d = xc.shape
    tm = ROW_TILE
    nt = lt // tm
    lat_tiles = n_lat // tm

    def mod_idx(i, t):
        return (layer * 8 + jnp.where(t >= lat_tiles, n_batch, i), 0, 0)

    widths = [512, 512, 512, 512, 512, 256, 256, 1024, 1024, 512, N_BRANCH * d]
    return pl.pallas_call(
        _inproj_kernel,
        grid=(b, nt),
        in_specs=[pl.BlockSpec((1, tm, d), lambda i, t: (i, t, 0)),
                  pl.BlockSpec((1, 1, mod_rows.shape[-1]), mod_idx),
                  _const_spec(nrm.shape), _const_spec(gains.shape),
                  pl.BlockSpec((tm, 512), lambda i, t: (t, 0)),
                  _const_spec(w.shape), _const_spec(wuq.shape), _const_spec(wukv.shape)],
        out_specs=[pl.BlockSpec((1, tm, wd), lambda i, t: (i, t, 0)) for wd in widths],
        out_shape=[jax.ShapeDtypeStruct((b, lt, wd), BF16) for wd in widths],
        compiler_params=_cparams(("parallel", "parallel")),
        name="inproj",
    )(xc, mod_rows, nrm, gains, rope, w, wuq, wukv)


def _s5_kernel(u_ref, m_ref, ws_ref, wo_ref, are_ref, aim_ref, y_ref,
               s_sc, hfre_sc, hfim_sc, hbre_sc, hbim_sc, *, n_chunks, n_lat_chunks):
    rows = S5_ROWS
    u = u_ref[0]
    s_sc[...] = jnp.dot(u, ws_ref[0], preferred_element_type=F32)
    a_re = jnp.broadcast_to(are_ref[0], (rows, LANES))
    a_im = jnp.broadcast_to(aim_ref[0], (rows, LANES))
    is_fwd = lax.broadcasted_iota(jnp.int32, (rows, LANES), 1) < S5_STATE
    n_ctx_chunks = n_chunks - n_lat_chunks

    def step(i, carry):
        h_re, h_im = carry
        cf = jnp.where(i < n_ctx_chunks, n_lat_chunks + i, i - n_ctx_chunks)
        cb = n_chunks - 1 - i
        rf = pl.ds(pl.multiple_of(cf * rows, rows), rows)
        rb = pl.ds(pl.multiple_of(cb * rows, rows), rows)
        hfre_sc[rf, :] = h_re
        hfim_sc[rf, :] = h_im
        hbre_sc[rb, :] = h_re
        hbim_sc[rb, :] = h_im
        s_re = jnp.where(is_fwd, s_sc[rf, 0:LANES], s_sc[rb, 0:LANES])
        s_im = jnp.where(is_fwd, s_sc[rf, LANES:2 * LANES], s_sc[rb, LANES:2 * LANES])
        return (a_re * h_re - a_im * h_im + s_re, a_re * h_im + a_im * h_re + s_im)

    zero = jnp.zeros((rows, LANES), F32)
    lax.fori_loop(0, n_chunks, step, (zero, zero))

    fwd_lane = lax.broadcasted_iota(jnp.int32, hfre_sc.shape, 1) < S5_STATE
    h_in = jnp.concatenate([jnp.where(fwd_lane, hfre_sc[...], hbre_sc[...]),
                            jnp.where(fwd_lane, hfim_sc[...], hbim_sc[...])], axis=-1).astype(BF16)
    y_ref[0] = (jnp.dot(u, m_ref[0], preferred_element_type=F32)
                + jnp.dot(h_in, wo_ref[0], preferred_element_type=F32))


def _s5_prep(lam_re, lam_im, log_step, b_re, b_im, c_re, c_im, d_skip):
    hp = lax.Precision.HIGHEST
    t_len, n_g, n_p, n_c = S5_CHUNK, lam_re.shape[1], S5_STATE, S5_GROUP
    lam_re, lam_im = lam_re.astype(F32), lam_im.astype(F32)
    dt = jnp.exp(log_step.astype(F32))[..., None]
    mag = jnp.exp(lam_re * dt)
    ab_re, ab_im = mag * jnp.cos(lam_im * dt), mag * jnp.sin(lam_im * dt)
    den = lam_re * lam_re + lam_im * lam_im
    nr, ni = ab_re - 1.0, ab_im
    coef_re = (nr * lam_re + ni * lam_im) / den
    coef_im = (ni * lam_re - nr * lam_im) / den
    b_re, b_im = b_re.astype(F32), b_im.astype(F32)
    bb_re = coef_re[..., None] * b_re - coef_im[..., None] * b_im
    bb_im = coef_re[..., None] * b_im + coef_im[..., None] * b_re
    kk = jnp.arange(t_len + 1, dtype=F32)[:, None, None, None]
    pmag = jnp.exp(lam_re * dt * kk)
    pw_re, pw_im = pmag * jnp.cos(lam_im * dt * kk), pmag * jnp.sin(lam_im * dt * kk)
    c_re, c_im = c_re.astype(F32), c_im.astype(F32)
    cp_re = c_re[None] * pw_re[:, :, :, None, :] - c_im[None] * pw_im[:, :, :, None, :]
    cp_im = c_re[None] * pw_im[:, :, :, None, :] + c_im[None] * pw_re[:, :, :, None, :]
    kern = (jnp.einsum('tdgop,dgpc->tdgoc', cp_re, bb_re, precision=hp)
            - jnp.einsum('tdgop,dgpc->tdgoc', cp_im, bb_im, precision=hp))
    s_idx = np.arange(t_len)[:, None]
    t_idx = np.arange(t_len)[None, :]
    lag_f = np.clip(t_idx - s_idx, 0, t_len)
    lag_b = np.clip(s_idx - t_idx, 0, t_len)
    mf = kern[lag_f, 0] * jnp.asarray(t_idx >= s_idx, F32)[:, :, None, None, None]
    mb = kern[lag_b, 1] * jnp.asarray(s_idx >= t_idx, F32)[:, :, None, None, None]
    m_full = jnp.transpose(mf + mb, (2, 0, 4, 1, 3))
    eye_t = jnp.eye(t_len, dtype=F32)[None, :, None, :, None]
    eye_c = jnp.eye(n_c, dtype=F32)[None, None, :, None, :]
    m_full = m_full + eye_t * eye_c * d_skip.astype(F32).reshape(n_g, 1, n_c, 1, 1)
    m_full = m_full.reshape(n_g, t_len * n_c, t_len * n_c)

    def bpow(pw_r, pw_i, direction):
        re = pw_r[..., None] * bb_re[direction][None] - pw_i[..., None] * bb_im[direction][None]
        im = pw_r[..., None] * bb_im[direction][None] + pw_i[..., None] * bb_re[direction][None]
        return jnp.transpose(re, (1, 0, 3, 2)), jnp.transpose(im, (1, 0, 3, 2))

    rev = np.arange(t_len - 1, -1, -1)
    fwd = np.arange(t_len)
    f_re, f_im = bpow(pw_re[rev, 0], pw_im[rev, 0], 0)
    g_re, g_im = bpow(pw_re[fwd, 1], pw_im[fwd, 1], 1)
    ws = jnp.concatenate([f_re, g_re, f_im, g_im], axis=-1).reshape(n_g, t_len * n_c, 4 * n_p)

    def cpow(idx, direction):
        return (jnp.transpose(cp_re[idx, direction], (1, 3, 0, 2)),
                jnp.transpose(cp_im[idx, direction], (1, 3, 0, 2)))

    of_re, of_im = cpow(np.arange(1, t_len + 1), 0)
    ob_re, ob_im = cpow(np.arange(t_len, 0, -1), 1)
    wo = jnp.concatenate([of_re, ob_re, -of_im, -ob_im], axis=1).reshape(n_g, 4 * n_p, t_len * n_c)
    a_re = jnp.concatenate([pw_re[t_len, 0], pw_re[t_len, 1]], axis=-1)[:, None, :]
    a_im = jnp.concatenate([pw_im[t_len, 0], pw_im[t_len, 1]], axis=-1)[:, None, :]
    return m_full.astype(BF16), ws.astype(BF16), wo.astype(BF16), a_re, a_im


def _s5(u, n_lat, ops):
    m_full, ws, wo, a_re, a_im = ops
    b, lt, width = u.shape
    n_g = width // S5_GROUP
    n_chunks = lt // S5_CHUNK
    cw = S5_CHUNK * S5_GROUP
    ug = u.reshape(b, n_chunks, S5_CHUNK, n_g, S5_GROUP).transpose(3, 1, 0, 2, 4)
    ug = jnp.pad(ug.reshape(n_g, n_chunks, b, cw), ((0, 0), (0, 0), (0, S5_ROWS - b), (0, 0)))
    rows = n_chunks * S5_ROWS
    ug = ug.reshape(n_g, rows, cw)
    grp = lambda g: (g, 0, 0)
    y = pl.pallas_call(
        functools.partial(_s5_kernel, n_chunks=n_chunks, n_lat_chunks=n_lat // S5_CHUNK),
        grid=(n_g,),
        in_specs=[pl.BlockSpec((1, rows, cw), grp),
                  pl.BlockSpec((1, cw, cw), grp), pl.BlockSpec((1, cw, 4 * S5_STATE), grp),
                  pl.BlockSpec((1, 4 * S5_STATE, cw), grp),
                  pl.BlockSpec((1, 1, LANES), grp), pl.BlockSpec((1, 1, LANES), grp)],
        out_specs=pl.BlockSpec((1, rows, cw), grp),
        out_shape=jax.ShapeDtypeStruct((n_g, rows, cw), F32),
        scratch_shapes=[pltpu.VMEM((rows, 4 * S5_STATE), F32)] + [pltpu.VMEM((rows, LANES), F32)] * 4,
        compiler_params=_cparams(("parallel",)),
        name="s5_scan",
    )(ug, m_full, ws, wo, a_re, a_im)
    y = y.reshape(n_g, n_chunks, S5_ROWS, S5_CHUNK, S5_GROUP)[:, :, :b]
    return y.transpose(2, 1, 3, 0, 4).reshape(b, lt, width)


def _attn_kernel(*refs, mode, tq, tk, n_lat, lam_init):
    if mode == "da":
        q_ref, k_ref, v_ref, lam_ref, sub_ref, o_ref, acc_sc = refs
    else:
        q_ref, k_ref, v_ref, o_ref, acc_sc = refs
    lt = k_ref.shape[1]
    qt = pl.program_id(2)
    is_ctx = qt * tq >= n_lat

    q = q_ref[0]
    if mode == "da":
        lo = lax.broadcasted_iota(jnp.int32, q.shape, 1) < DA_DK
        zero = jnp.zeros_like(q)
        qs = jnp.concatenate([jnp.where(lo, q, zero), jnp.where(lo, zero, q)], axis=0)
    elif mode == "gq":
        qs = jnp.concatenate([q[:, 0:GQ_DH], q[:, GQ_DH:2 * GQ_DH]], axis=0)
    else:
        qs = q
    n_rows = qs.shape[0]

    def chunk(start, size, m, l):
        kc = k_ref[0, pl.ds(start, size), :]
        vc = v_ref[0, pl.ds(start, size), :]
        s = lax.dot_general(qs, kc, (((1,), (1,)), ((), ())), preferred_element_type=F32)
        m_new = jnp.maximum(m, jnp.max(s, axis=-1, keepdims=True))
        alpha = jnp.exp(m - m_new)
        p = jnp.exp(s - m_new)
        l_new = alpha * l + jnp.sum(p, axis=-1, keepdims=True)
        acc_sc[...] = alpha * acc_sc[...] + jnp.dot(p.astype(BF16), vc, preferred_element_type=F32)
        return m_new, l_new

    acc_sc[...] = jnp.zeros_like(acc_sc)
    m0 = jnp.full((n_rows, 1), -1e30, F32)
    l0 = jnp.zeros((n_rows, 1), F32)
    n_main = jnp.where(is_ctx, 0, n_lat // tk)

    def body(j, carry):
        return chunk(pl.multiple_of(j * tk, tk), tk, *carry)

    m, l = lax.fori_loop(0, n_main, body, (m0, l0))
    m, l = chunk(n_lat, lt - n_lat, m, l)
    o = acc_sc[...] / l

    if mode == "da":
        lam = lam_ref[...]
        lam_full = (jnp.exp(jnp.sum(lam[0:1] * lam[1:2], axis=-1, keepdims=True))
                    - jnp.exp(jnp.sum(lam[2:3] * lam[3:4], axis=-1, keepdims=True)) + lam_init)
        dlt = o[0:tq] - lam_full * o[tq:2 * tq]
        ms = jnp.mean(dlt * dlt, axis=-1, keepdims=True)
        o_ref[0] = (dlt * lax.rsqrt(ms + NORM_EPS) * sub_ref[...] * (1.0 - lam_init)).astype(o_ref.dtype)
    elif mode == "gq":
        o_ref[0] = jnp.concatenate([o[0:tq], o[tq:2 * tq]], axis=-1).astype(o_ref.dtype)
    else:
        o_ref[0] = o.astype(o_ref.dtype)


def _pick_tk(n_lat):
    for tk in (1024, 512, 256):
        if n_lat % tk == 0:
            return tk
    raise ValueError("latent length must be a multiple of 256")


def _attention(mode, q, k, v, n_lat, n_q_rows, extra=(), lam_init=0.0):
    b, lt, _ = q.shape
    tq = ROW_TILE
    if mode == "da":
        heads, qw, kw, ow, g = DA_HEADS, 128, 128, 128, 2
    elif mode == "gq":
        heads, qw, kw, ow, g = GQ_KV, 256, 128, 256, 2
    else:
        heads, qw, kw, ow, g = ML_HEADS, 256, 256, 128, 1
    in_specs = [pl.BlockSpec((1, tq, qw), lambda i, h, t: (i, t, h)),
                pl.BlockSpec((1, lt, kw), lambda i, h, t: (i, 0, h)),
                pl.BlockSpec((1, lt, 128), lambda i, h, t: (i, 0, h))]
    in_specs += [_const_spec(e.shape) for e in extra]
    return pl.pallas_call(
        functools.partial(_attn_kernel, mode=mode, tq=tq, tk=_pick_tk(n_lat), n_lat=n_lat,
                          lam_init=lam_init),
        grid=(b, heads, n_q_rows // tq),
        in_specs=in_specs,
        out_specs=pl.BlockSpec((1, tq, ow), lambda i, h, t: (i, t, h)),
        out_shape=jax.ShapeDtypeStruct((b, n_q_rows, BRANCH_W), BF16),
        scratch_shapes=[pltpu.VMEM((g * tq, 128), F32)],
        compiler_params=_cparams(("parallel", "parallel", "arbitrary")),
        name="attn_" + mode,
    )(q, k, v, *extra)


def _gelu_tanh(x):
    return 0.5 * x * (1.0 + jnp.tanh(math.sqrt(2.0 / math.pi) * (x + 0.044715 * (x * x * x))))


def _merge_kernel(x_ref, mod_ref, y_ref, da_ref, gq_ref, ml_ref, gate_ref,
                  wglu_ref, bglu_ref, wbr_ref, wout_ref, o_ref):
    d = x_ref.shape[-1]
    g = _gelu_tanh(y_ref[0])
    glu = jax.nn.sigmoid(jnp.dot(g.astype(BF16), wglu_ref[...], preferred_element_type=F32)
                         + bglu_ref[...])
    branches = [(g * glu).astype(BF16), da_ref[0], gq_ref[0], ml_ref[0]]
    acc = None
    for n, br in enumerate(branches):
        term = gate_ref[0, :, n * d:(n + 1) * d].astype(F32) * jnp.dot(
            br, wbr_ref[n], preferred_element_type=F32)
        acc = term if acc is None else acc + term
    out = jnp.dot(acc.astype(BF16), wout_ref[...], preferred_element_type=F32)
    g1 = mod_ref[0][:, 2 * d:3 * d]
    o_ref[0] = x_ref[0] + g1 * out


def _ffn_kernel(x_ref, mod_ref, nrm_ref, win_ref, wout_ref, o_ref, *, ff_chunk):
    d = x_ref.shape[-1]
    d_ff = wout_ref.shape[0]
    x = x_ref[0]
    mod = mod_ref[0]
    shift, scale, g2 = mod[:, 3 * d:4 * d], mod[:, 4 * d:5 * d], mod[:, 5 * d:6 * d]
    ms = jnp.mean(x * x, axis=-1, keepdims=True)
    hb = ((x * lax.rsqrt(ms + NORM_EPS) * nrm_ref[...]) * (1.0 + scale) + shift).astype(BF16)
    out = None
    for c0 in range(0, d_ff, ff_chunk):
        gate = jnp.dot(hb, win_ref[:, c0:c0 + ff_chunk], preferred_element_type=F32)
        up = jnp.dot(hb, win_ref[:, d_ff + c0:d_ff + c0 + ff_chunk], preferred_element_type=F32)
        act = (gate * jax.nn.sigmoid(gate) * up).astype(BF16)
        part = jnp.dot(act, wout_ref[c0:c0 + ff_chunk, :], preferred_element_type=F32)
        out = part if out is None else out + part
    o_ref[0] = x + g2 * out


def _mod_index(layer, n_batch, lat_tiles):
    def idx(i, t):
        return (layer * 8 + jnp.where(t >= lat_tiles, n_batch, i), 0, 0)
    return idx


def _merge(xc, mod_rows, layer, n_lat, n_rows, y, da, gq, ml, gates, wglu, bglu, wbr, wout):
    b, _, d = xc.shape
    tm = ROW_TILE
    row = lambda i, t: (i, t, 0)
    return pl.pallas_call(
        _merge_kernel,
        grid=(b, n_rows // tm),
        in_specs=[pl.BlockSpec((1, tm, d), row),
                  pl.BlockSpec((1, 1, mod_rows.shape[-1]), _mod_index(layer, b, n_lat // tm)),
                  pl.BlockSpec((1, tm, BRANCH_W), row), pl.BlockSpec((1, tm, BRANCH_W), row),
                  pl.BlockSpec((1, tm, BRANCH_W), row), pl.BlockSpec((1, tm, BRANCH_W), row),
                  pl.BlockSpec((1, tm, N_BRANCH * d), row),
                  _const_spec(wglu.shape), _const_spec(bglu.shape),
                  _const_spec(wbr.shape), _const_spec(wout.shape)],
        out_specs=pl.BlockSpec((1, tm, d), row),
        out_shape=jax.ShapeDtypeStruct((b, n_rows, d), F32),
        compiler_params=_cparams(("parallel", "parallel")),
        name="merge",
    )(xc, mod_rows, y, da, gq, ml, gates, wglu, bglu, wbr, wout)


def _ffn(x1, mod_rows, layer, n_lat, nrm, win, wout):
    b, n_rows, d = x1.shape
    tm = ROW_TILE
    d_ff = wout.shape[0]
    ff_chunk = d_ff // 2 if (d_ff // 2) % LANES == 0 else d_ff
    row = lambda i, t: (i, t, 0)
    return pl.pallas_call(
        functools.partial(_ffn_kernel, ff_chunk=ff_chunk),
        grid=(b, n_rows // tm),
        in_specs=[pl.BlockSpec((1, tm, d), row),
                  pl.BlockSpec((1, 1, mod_rows.shape[-1]), _mod_index(layer, b, n_lat // tm)),
                  _const_spec(nrm.shape), _const_spec(win.shape), _const_spec(wout.shape)],
        out_specs=pl.BlockSpec((1, tm, d), row),
        out_shape=jax.ShapeDtypeStruct((b, n_rows, d), F32),
        compiler_params=_cparams(("parallel", "parallel")),
        name="ffn",
    )(x1, mod_rows, nrm, win, wout)


def _inproj_columns(d):
    de64, de128 = _deinterleave(64), _deinterleave(128)
    off_da = 512
    off_gq = off_da + 3 * DA_HEADS * DA_DV
    off_ml = off_gq + (GQ_HEADS + 2 * GQ_KV) * GQ_DH
    off_gate = off_ml + ML_QRANK + ML_KVRANK + ML_ROPE
    cols = [np.arange(512)]
    for part in range(2):
        for hd in range(DA_HEADS):
            for comp in range(2):
                cols.append(off_da + part * 512 + hd * 128 + comp * 64 + de64)
    cols.append(off_da + 1024 + np.arange(512))
    for hd in range(GQ_HEADS):
        cols.append(off_gq + hd * 128 + de128)
    for hd in range(GQ_KV):
        cols.append(off_gq + 512 + hd * 128 + de128)
    cols.append(off_gq + 768 + np.arange(256))
    cols.append(off_ml + np.arange(ML_QRANK + ML_KVRANK))
    cols.append(off_ml + ML_QRANK + ML_KVRANK + de64)
    cols.append(off_ml + ML_QRANK + ML_KVRANK + de64)
    cols.append(off_gate + np.arange(N_BRANCH * d))
    return np.concatenate(cols)


def _rope_table(n_lat, n_ctx):
    t = np.arange(n_lat)
    r = (t // GRID_W).astype(np.float32)
    col = (t % GRID_W).astype(np.float32)

    def angles(rot_dim):
        half = rot_dim // 2
        inv = jnp.asarray(ROPE_THETA, F32) ** (-jnp.arange(0, half, 2, dtype=F32) / half)
        return jnp.concatenate([jnp.asarray(r)[:, None] * inv, jnp.asarray(col)[:, None] * inv], axis=-1)

    a64, a128 = angles(64), angles(128)
    c64, s64 = jnp.cos(a64), jnp.sin(a64)
    c128, s128 = jnp.cos(a128), jnp.sin(a128)
    lat = jnp.concatenate([c64, c64, c64, c64, -s64, s64, -s64, s64, c128, c128, -s128, s128], axis=-1)
    ctx = jnp.concatenate([jnp.ones((n_ctx, 128), F32), jnp.zeros((n_ctx, 128), F32),
                           jnp.ones((n_ctx, 128), F32), jnp.zeros((n_ctx, 128), F32)], axis=-1)
    return jnp.concatenate([lat, ctx], axis=0)


def kernel(x, c, ctx, c_ctx, w_mod, b_mod, norm_mix, w_in, s5_lam_re, s5_lam_im, s5_log_step, s5_b_re, s5_b_im, s5_c_re, s5_c_im, s5_d, s5_w_glu, s5_b_glu, da_q_norm, da_k_norm, da_lam, da_subln, gq_q_norm, gq_k_norm, ml_cq_norm, ml_ckv_norm, ml_w_uq, ml_w_ukv, ml_q_norm, ml_k_norm, w_branch, w_out, norm_ffn, w_ffn_in, w_ffn_out):
    b, n_lat, d = x.shape
    n_ctx = ctx.shape[1]
    depth = w_mod.shape[0]
    assert n_ctx == ROW_TILE and n_lat % ROW_TILE == 0 and b < 8 and d == 1024

    cc = jnp.zeros((8, d), F32).at[:b].set(c).at[b].set(c_ctx)
    mod_rows = _modulation(cc, w_mod, b_mod).reshape(depth * 8, 1, 6 * d)
    rope = _rope_table(n_lat, n_ctx)
    xc = jnp.concatenate([x, ctx], axis=1)

    de64, de128 = _deinterleave(64), _deinterleave(128)
    cols = _inproj_columns(d)
    uq_cols = np.concatenate([hd * 192 + np.arange(128) for hd in range(ML_HEADS)]
                             + [hd * 192 + 128 + de64 for hd in range(ML_HEADS)])
    ukv_cols = np.concatenate([hd * 256 + np.arange(128) for hd in range(ML_HEADS)]
                              + [hd * 256 + 128 + np.arange(128) for hd in range(ML_HEADS)])
    ml_scale = (ML_NOPE + ML_ROPE) ** -0.5

    for i in range(depth):
        last = i == depth - 1
        lam_init = 0.8 - 0.6 * math.exp(-0.3 * i)
        w = w_in[i][:, cols].astype(BF16)
        wuq = ml_w_uq[i][:, uq_cols].astype(BF16)
        wukv = ml_w_ukv[i][:, ukv_cols].astype(BF16)
        gains = jnp.concatenate([
            jnp.tile(da_q_norm[i][de64], 8) * DA_DK ** -0.5, jnp.tile(da_k_norm[i][de64], 8),
            jnp.tile(gq_q_norm[i][de128], 4) * GQ_DH ** -0.5, jnp.tile(gq_k_norm[i][de128], 2),
            ml_cq_norm[i], ml_ckv_norm[i],
            jnp.tile(ml_q_norm[i][:ML_NOPE], 4) * ml_scale,
            jnp.tile(ml_q_norm[i][ML_NOPE:][de64], 4) * ml_scale,
            jnp.tile(ml_k_norm[i][:ML_NOPE], 4), jnp.tile(ml_k_norm[i][ML_NOPE:][de64], 2),
        ]).astype(F32)[None, :]

        (u, daq, dak, dav, gqq, gqk, gqv, mlq, mlk, mlv, gates) = _inproj(
            xc, mod_rows, i, b, n_lat, norm_mix[i][None, :], gains, rope, w, wuq, wukv)

        s5_ops = _s5_prep(s5_lam_re[i], s5_lam_im[i], s5_log_step[i], s5_b_re[i], s5_b_im[i],
                          s5_c_re[i], s5_c_im[i], s5_d[i])
        y = _s5(u, n_lat, s5_ops)

        n_rows = n_lat if last else n_lat + n_ctx
        da = _attention("da", daq, dak, dav, n_lat, n_rows,
                        extra=(da_lam[i].astype(F32), da_subln[i].astype(F32)[None, :]),
                        lam_init=lam_init)
        gq = _attention("gq", gqq, gqk, gqv, n_lat, n_rows)
        ml = _attention("ml", mlq, mlk, mlv, n_lat, n_rows)

        x1 = _merge(xc, mod_rows, i, n_lat, n_rows, y, da, gq, ml, gates,
                    s5_w_glu[i].astype(BF16), s5_b_glu[i].astype(F32)[None, :],
                    w_branch[i].astype(BF16), w_out[i].astype(BF16))
        xc = _ffn(x1, mod_rows, i, n_lat, norm_ffn[i][None, :],
                  w_ffn_in[i].astype(BF16), w_ffn_out[i].astype(BF16))
    return xc
```

```python
import functools
import math

import numpy as np
import jax
import jax.numpy as jnp
from jax import lax
from jax.experimental import pallas as pl
from jax.experimental.pallas import tpu as pltpu

GRID_W = 64
ROPE_THETA = 10000.0
NORM_EPS = 1e-6

S5_GROUP = 16
S5_STATE = 64
S5_CHUNK = 16
S5_ROWS = 8

DA_HEADS = 4
DA_DK = 64
DA_DV = 128
GQ_HEADS = 4
GQ_KV = 2
GQ_DH = 128
ML_HEADS = 4
ML_QRANK = 256
ML_KVRANK = 128
ML_NOPE = 128
ML_ROPE = 64
ML_DV = 128
N_BRANCH = 4
BRANCH_W = 512

LANES = 128
ROW_TILE = 256
VMEM_LIMIT = 56 * 1024 * 1024

F32 = jnp.float32
BF16 = jnp.bfloat16


def _deinterleave(n):
    return np.concatenate([np.arange(0, n, 2), np.arange(1, n, 2)])


def _cparams(sem):
    return pltpu.CompilerParams(dimension_semantics=sem, vmem_limit_bytes=VMEM_LIMIT)


def _const_spec(shape):
    nd = len(shape)
    return pl.BlockSpec(shape, lambda *_: (0,) * nd)


def _mod_kernel(cc_ref, w_ref, b_ref, o_ref):
    a = cc_ref[...]
    a = a * jax.nn.sigmoid(a)
    o_ref[0] = jnp.dot(a, w_ref[0], preferred_element_type=F32,
                       precision=lax.Precision.HIGHEST) + b_ref[0]


def _modulation(cc, w_mod, b_mod):
    depth, d, n = w_mod.shape
    tn = n // 4
    return pl.pallas_call(
        _mod_kernel,
        grid=(depth, n // tn),
        in_specs=[pl.BlockSpec((8, d), lambda i, j: (0, 0)),
                  pl.BlockSpec((1, d, tn), lambda i, j: (i, 0, j)),
                  pl.BlockSpec((1, 1, tn), lambda i, j: (i, 0, j))],
        out_specs=pl.BlockSpec((1, 8, tn), lambda i, j: (i, 0, j)),
        out_shape=jax.ShapeDtypeStruct((depth, 8, n), F32),
        compiler_params=_cparams(("arbitrary", "arbitrary")),
        name="modulation",
    )(cc, w_mod, b_mod.reshape(depth, 1, n))


def _seg_rms(z, seg):
    width = z.shape[-1]
    if seg == 2 * LANES:
        outs = []
        for g in range(width // seg):
            zg = z[:, g * seg:(g + 1) * seg]
            ms = jnp.sum(zg * zg, axis=-1, keepdims=True) * (1.0 / seg)
            outs.append(zg * lax.rsqrt(ms + NORM_EPS))
        return outs[0] if len(outs) == 1 else jnp.concatenate(outs, axis=-1)
    outs = []
    for g in range(width // LANES):
        zg = z[:, g * LANES:(g + 1) * LANES]
        zz = zg * zg
        if seg == LANES:
            ms = jnp.sum(zz, axis=-1, keepdims=True) * (1.0 / seg)
        else:
            lo = lax.broadcasted_iota(jnp.int32, zz.shape, 1) < seg
            s_lo = jnp.sum(jnp.where(lo, zz, 0.0), axis=-1, keepdims=True)
            s_hi = jnp.sum(jnp.where(lo, 0.0, zz), axis=-1, keepdims=True)
            ms = jnp.where(lo, s_lo, s_hi) * (1.0 / seg)
        outs.append(zg * lax.rsqrt(ms + NORM_EPS))
    return outs[0] if len(outs) == 1 else jnp.concatenate(outs, axis=-1)


def _rope(x, cos, sin, unit):
    outs = []
    for g in range(x.shape[-1] // LANES):
        xg = x[:, g * LANES:(g + 1) * LANES]
        if unit == LANES:
            rot = pltpu.roll(xg, LANES // 2, 1)
        else:
            lane = lax.broadcasted_iota(jnp.int32, xg.shape, 1)
            rot = jnp.where((lane & (unit // 2)) == 0,
                            pltpu.roll(xg, LANES - unit // 2, 1), pltpu.roll(xg, unit // 2, 1))
        outs.append(xg * cos + rot * sin)
    return outs[0] if len(outs) == 1 else jnp.concatenate(outs, axis=-1)


_C_S5 = 0
_C_DAQ = 512
_C_DAK = 1024
_C_DAV = 1536
_C_GQQ = 2048
_C_GQK = 2560
_C_GQV = 2816
_C_ML = 3072
_C_GATE = 3584
_C_END = 3584 + 4096

_G_DAQ, _G_DAK, _G_GQQ, _G_GQK = 0, 512, 1024, 1536
_G_CQ, _G_CKV, _G_QN, _G_QR, _G_KN, _G_KR, _G_END = 1792, 2048, 2176, 2688, 2944, 3456, 3584


def _inproj_kernel(x_ref, mod_ref, nrm_ref, gains_ref, rope_ref, w_ref, wuq_ref, wukv_ref,
                   u_ref, daq_ref, dak_ref, dav_ref, gqq_ref, gqk_ref, gqv_ref,
                   mlq_ref, mlk_ref, mlv_ref, gate_ref):
    d = x_ref.shape[-1]
    x = x_ref[0]
    ms = jnp.mean(x * x, axis=-1, keepdims=True)
    mod = mod_ref[0]
    shift, scale = mod[:, 0:d], mod[:, d:2 * d]
    h = (x * lax.rsqrt(ms + NORM_EPS) * nrm_ref[...]) * (1.0 + scale) + shift
    hb = h.astype(BF16)

    def proj(c0, c1):
        return jnp.dot(hb, w_ref[:, c0:c1], preferred_element_type=F32)

    def gain(g0, g1):
        return gains_ref[:, g0:g1]

    cos64, sin64 = rope_ref[:, 0:128], rope_ref[:, 128:256]
    cos128, sin128 = rope_ref[:, 256:384], rope_ref[:, 384:512]

    u_ref[0] = proj(_C_S5, _C_DAQ).astype(BF16)

    q = _seg_rms(proj(_C_DAQ, _C_DAK), DA_DK) * gain(_G_DAQ, _G_DAK)
    daq_ref[0] = _rope(q, cos64, sin64, DA_DK).astype(BF16)
    k = _seg_rms(proj(_C_DAK, _C_DAV), DA_DK) * gain(_G_DAK, _G_GQQ)
    dak_ref[0] = _rope(k, cos64, sin64, DA_DK).astype(BF16)
    dav_ref[0] = proj(_C_DAV, _C_GQQ).astype(BF16)

    q = _seg_rms(proj(_C_GQQ, _C_GQK), GQ_DH) * gain(_G_GQQ, _G_GQK)
    gqq_ref[0] = _rope(q, cos128, sin128, GQ_DH).astype(BF16)
    k = _seg_rms(proj(_C_GQK, _C_GQV), GQ_DH) * gain(_G_GQK, _G_CQ)
    gqk_ref[0] = _rope(k, cos128, sin128, GQ_DH).astype(BF16)
    gqv_ref[0] = proj(_C_GQV, _C_ML).astype(BF16)

    zc = proj(_C_ML, _C_GATE)
    cq = (_seg_rms(zc[:, 0:256], 256) * gain(_G_CQ, _G_CKV)).astype(BF16)
    ckv = (_seg_rms(zc[:, 256:384], 128) * gain(_G_CKV, _G_QN)).astype(BF16)
    qq = jnp.dot(cq, wuq_ref[...], preferred_element_type=F32)
    kv = jnp.dot(ckv, wukv_ref[...], preferred_element_type=F32)
    qn = (_seg_rms(qq[:, 0:512], ML_NOPE) * gain(_G_QN, _G_QR)).astype(BF16)
    qr = _seg_rms(qq[:, 512:768], ML_ROPE) * gain(_G_QR, _G_KN)
    qr = _rope(qr, cos64, sin64, ML_ROPE)
    kn = (_seg_rms(kv[:, 0:512], ML_NOPE) * gain(_G_KN, _G_KR)).astype(BF16)
    kr = _seg_rms(zc[:, 384:512], ML_ROPE) * gain(_G_KR, _G_END)
    kr = _rope(kr, cos64, sin64, ML_ROPE).astype(BF16)
    lo = lax.broadcasted_iota(jnp.int32, (x.shape[0], LANES), 1) < ML_ROPE
    for hd in range(ML_HEADS):
        pair = qr[:, (hd // 2) * LANES:(hd // 2 + 1) * LANES]
        keep = lo if hd % 2 == 0 else jnp.logical_not(lo)
        mlq_ref[0, :, hd * 256:hd * 256 + 128] = qn[:, hd * 128:(hd + 1) * 128]
        mlq_ref[0, :, hd * 256 + 128:(hd + 1) * 256] = jnp.where(keep, pair, 0.0).astype(BF16)
        mlk_ref[0, :, hd * 256:hd * 256 + 128] = kn[:, hd * 128:(hd + 1) * 128]
        mlk_ref[0, :, hd * 256 + 128:(hd + 1) * 256] = kr
    mlv_ref[0] = kv[:, 512:1024].astype(BF16)

    for j in range(N_BRANCH):
        c0 = _C_GATE + j * d
        gate_ref[0, :, j * d:(j + 1) * d] = jax.nn.sigmoid(proj(c0, c0 + d)).astype(BF16)


def _inproj(xc, mod_rows, layer, n_batch, n_lat, nrm, gains, rope, w, wuq, wukv):
    b, lt, d = xc.shape
    tm = ROW_TILE
    nt = lt // tm
    lat_tiles = n_lat // tm

    def mod_idx(i, t):
        return (layer * 8 + jnp.where(t >= lat_tiles, n_batch, i), 0, 0)

    widths = [512, 512, 512, 512, 512, 256, 256, 1024, 1024, 512, N_BRANCH * d]
    return pl.pallas_call(
        _inproj_kernel,
        grid=(b, nt),
        in_specs=[pl.BlockSpec((1, tm, d), lambda i, t: (i, t, 0)),
                  pl.BlockSpec((1, 1, mod_rows.shape[-1]), mod_idx),
                  _const_spec(nrm.shape), _const_spec(gains.shape),
                  pl.BlockSpec((tm, 512), lambda i, t: (t, 0)),
                  _const_spec(w.shape), _const_spec(wuq.shape), _const_spec(wukv.shape)],
        out_specs=[pl.BlockSpec((1, tm, wd), lambda i, t: (i, t, 0)) for wd in widths],
        out_shape=[jax.ShapeDtypeStruct((b, lt, wd), BF16) for wd in widths],
        compiler_params=_cparams(("parallel", "parallel")),
        name="inproj",
    )(xc, mod_rows, nrm, gains, rope, w, wuq, wukv)


def _s5_kernel(u_ref, m_ref, ws_ref, wo_ref, are_ref, aim_ref, y_ref,
               s_sc, hfre_sc, hfim_sc, hbre_sc, hbim_sc, *, n_chunks, n_lat_chunks):
    rows = S5_ROWS
    u = u_ref[0]
    s_sc[...] = jnp.dot(u, ws_ref[0], preferred_element_type=F32)
    a_re = jnp.broadcast_to(are_ref[0], (rows, LANES))
    a_im = jnp.broadcast_to(aim_ref[0], (rows, LANES))
    is_fwd = lax.broadcasted_iota(jnp.int32, (rows, LANES), 1) < S5_STATE
    n_ctx_chunks = n_chunks - n_lat_chunks

    def step(i, carry):
        h_re, h_im = carry
        cf = jnp.where(i < n_ctx_chunks, n_lat_chunks + i, i - n_ctx_chunks)
        cb = n_chunks - 1 - i
        rf = pl.ds(pl.multiple_of(cf * rows, rows), rows)
        rb = pl.ds(pl.multiple_of(cb * rows, rows), rows)
        hfre_sc[rf, :] = h_re
        hfim_sc[rf, :] = h_im
        hbre_sc[rb, :] = h_re
        hbim_sc[rb, :] = h_im
        s_re = jnp.where(is_fwd, s_sc[rf, 0:LANES], s_sc[rb, 0:LANES])
        s_im = jnp.where(is_fwd, s_sc[rf, LANES:2 * LANES], s_sc[rb, LANES:2 * LANES])
        return (a_re * h_re - a_im * h_im + s_re, a_re * h_im + a_im * h_re + s_im)

    zero = jnp.zeros((rows, LANES), F32)
    lax.fori_loop(0, n_chunks, step, (zero, zero))

    fwd_lane = lax.broadcasted_iota(jnp.int32, hfre_sc.shape, 1) < S5_STATE
    h_in = jnp.concatenate([jnp.where(fwd_lane, hfre_sc[...], hbre_sc[...]),
                            jnp.where(fwd_lane, hfim_sc[...], hbim_sc[...])], axis=-1).astype(BF16)
    y_ref[0] = (jnp.dot(u, m_ref[0], preferred_element_type=F32)
                + jnp.dot(h_in, wo_ref[0], preferred_element_type=F32))


def _s5_prep(lam_re, lam_im, log_step, b_re, b_im, c_re, c_im, d_skip):
    hp = lax.Precision.HIGHEST
    t_len, n_g, n_p, n_c = S5_CHUNK, lam_re.shape[1], S5_STATE, S5_GROUP
    lam_re, lam_im = lam_re.astype(F32), lam_im.astype(F32)
    dt = jnp.exp(log_step.astype(F32))[..., None]
    mag = jnp.exp(lam_re * dt)
    ab_re, ab_im = mag * jnp.cos(lam_im * dt), mag * jnp.sin(lam_im * dt)
    den = lam_re * lam_re + lam_im * lam_im
    nr, ni = ab_re - 1.0, ab_im
    coef_re = (nr * lam_re + ni * lam_im) / den
    coef_im = (ni * lam_re - nr * lam_im) / den
    b_re, b_im = b_re.astype(F32), b_im.astype(F32)
    bb_re = coef_re[..., None] * b_re - coef_im[..., None] * b_im
    bb_im = coef_re[..., None] * b_im + coef_im[..., None] * b_re
    kk = jnp.arange(t_len + 1, dtype=F32)[:, None, None, None]
    pmag = jnp.exp(lam_re * dt * kk)
    pw_re, pw_im = pmag * jnp.cos(lam_im * dt * kk), pmag * jnp.sin(lam_im * dt * kk)
    c_re, c_im = c_re.astype(F32), c_im.astype(F32)
    cp_re = c_re[None] * pw_re[:, :, :, None, :] - c_im[None] * pw_im[:, :, :, None, :]
    cp_im = c_re[None] * pw_im[:, :, :, None, :] + c_im[None] * pw_re[:, :, :, None, :]
    kern = (jnp.einsum('tdgop,dgpc->tdgoc', cp_re, bb_re, precision=hp)
            - jnp.einsum('tdgop,dgpc->tdgoc', cp_im, bb_im, precision=hp))
    s_idx = np.arange(t_len)[:, None]
    t_idx = np.arange(t_len)[None, :]
    lag_f = np.clip(t_idx - s_idx, 0, t_len)
    lag_b = np.clip(s_idx - t_idx, 0, t_len)
    mf = kern[lag_f, 0] * jnp.asarray(t_idx >= s_idx, F32)[:, :, None, None, None]
    mb = kern[lag_b, 1] * jnp.asarray(s_idx >= t_idx, F32)[:, :, None, None, None]
    m_full = jnp.transpose(mf + mb, (2, 0, 4, 1, 3))
    eye_t = jnp.eye(t_len, dtype=F32)[None, :, None, :, None]
    eye_c = jnp.eye(n_c, dtype=F32)[None, None, :, None, :]
    m_full = m_full + eye_t * eye_c * d_skip.astype(F32).reshape(n_g, 1, n_c, 1, 1)
    m_full = m_full.reshape(n_g, t_len * n_c, t_len * n_c)

    def bpow(pw_r, pw_i, direction):
        re = pw_r[..., None] * bb_re[direction][None] - pw_i[..., None] * bb_im[direction][None]
        im = pw_r[..., None] * bb_im[direction][None] + pw_i[..., None] * bb_re[direction][None]
        return jnp.transpose(re, (1, 0, 3, 2)), jnp.transpose(im, (1, 0, 3, 2))

    rev = np.arange(t_len - 1, -1, -1)
    fwd = np.arange(t_len)
    f_re, f_im = bpow(pw_re[rev, 0], pw_im[rev, 0], 0)
    g_re, g_im = bpow(pw_re[fwd, 1], pw_im[fwd, 1], 1)
    ws = jnp.concatenate([f_re, g_re, f_im, g_im], axis=-1).reshape(n_g, t_len * n_c, 4 * n_p)

    def cpow(idx, direction):
        return (jnp.transpose(cp_re[idx, direction], (1, 3, 0, 2)),
                jnp.transpose(cp_im[idx, direction], (1, 3, 0, 2)))

    of_re, of_im = cpow(np.arange(1, t_len + 1), 0)
    ob_re, ob_im = cpow(np.arange(t_len, 0, -1), 1)
    wo = jnp.concatenate([of_re, ob_re, -of_im, -ob_im], axis=1).reshape(n_g, 4 * n_p, t_len * n_c)
    a_re = jnp.concatenate([pw_re[t_len, 0], pw_re[t_len, 1]], axis=-1)[:, None, :]
    a_im = jnp.concatenate([pw_im[t_len, 0], pw_im[t_len, 1]], axis=-1)[:, None, :]
    return m_full.astype(BF16), ws.astype(BF16), wo.astype(BF16), a_re, a_im


def _s5(u, n_lat, ops):
    m_full, ws, wo, a_re, a_im = ops
    b, lt, width = u.shape
    n_g = width // S5_GROUP
    n_chunks = lt // S5_CHUNK
    cw = S5_CHUNK * S5_GROUP
    ug = u.reshape(b, n_chunks, S5_CHUNK, n_g, S5_GROUP).transpose(3, 1, 0, 2, 4)
    ug = jnp.pad(ug.reshape(n_g, n_chunks, b, cw), ((0, 0), (0, 0), (0, S5_ROWS - b), (0, 0)))
    rows = n_chunks * S5_ROWS
    ug = ug.reshape(n_g, rows, cw)
    grp = lambda g: (g, 0, 0)
    y = pl.pallas_call(
        functools.partial(_s5_kernel, n_chunks=n_chunks, n_lat_chunks=n_lat // S5_CHUNK),
        grid=(n_g,),
        in_specs=[pl.BlockSpec((1, rows, cw), grp),
                  pl.BlockSpec((1, cw, cw), grp), pl.BlockSpec((1, cw, 4 * S5_STATE), grp),
                  pl.BlockSpec((1, 4 * S5_STATE, cw), grp),
                  pl.BlockSpec((1, 1, LANES), grp), pl.BlockSpec((1, 1, LANES), grp)],
        out_specs=pl.BlockSpec((1, rows, cw), grp),
        out_shape=jax.ShapeDtypeStruct((n_g, rows, cw), F32),
        scratch_shapes=[pltpu.VMEM((rows, 4 * S5_STATE), F32)] + [pltpu.VMEM((rows, LANES), F32)] * 4,
        compiler_params=_cparams(("parallel",)),
        name="s5_scan",
    )(ug, m_full, ws, wo, a_re, a_im)
    y = y.reshape(n_g, n_chunks, S5_ROWS, S5_CHUNK, S5_GROUP)[:, :, :b]
    return y.transpose(2, 1, 3, 0, 4).reshape(b, lt, width)


ATTN_ROW_BLOCKS = 2
ATTN_KEY_TILE = 256


def _attn_kernel(*refs, mode, tq, tk, n_lat, n_extra, lam_init):
    q_ref, k_ref, v_ref = refs[:3]
    extra = refs[3:3 + n_extra]
    o_ref, acc_sc, sa_sc, sb_sc = refs[-4:]
    n_keys = k_ref.shape[1]
    n_ctx = n_keys - n_lat

    q = q_ref[0]
    if mode == "da":
        lo = lax.broadcasted_iota(jnp.int32, q.shape, 1) < DA_DK
        zero = jnp.zeros_like(q)
        qs = jnp.concatenate([jnp.where(lo, q, zero), jnp.where(lo, zero, q)], axis=0)
    elif mode == "gq":
        qs = jnp.concatenate([q[:, 0:GQ_DH], q[:, GQ_DH:2 * GQ_DH]], axis=0)
    else:
        qs = q
    n_rb = ATTN_ROW_BLOCKS
    rb = qs.shape[0] // n_rb
    q_blocks = [qs[r * rb:(r + 1) * rb] for r in range(n_rb)]

    def qk(r, start, size):
        kc = k_ref[0, pl.ds(start, size), :]
        s = lax.dot_general(q_blocks[r], kc, (((1,), (1,)), ((), ())), preferred_element_type=F32)
        return s, jnp.max(s, axis=-1, keepdims=True)

    def softmax_pv(r, s_buf, mx, m_old, start, size):
        m_new = jnp.maximum(m_old, mx)
        alpha = jnp.exp2(m_old - m_new)
        rows = pl.ds(r * rb, rb)
        kt = min(ATTN_KEY_TILE, size)
        pv, lsum = None, None
        for c0 in range(0, size, kt):
            p = jnp.exp2((s_buf[:, c0:c0 + kt] - m_new).astype(BF16))
            p32 = p.astype(F32)
            for i in range(kt // LANES):
                t = p32[:, i * LANES:(i + 1) * LANES]
                lsum = t if lsum is None else lsum + t
            t = jnp.dot(p, v_ref[0, pl.ds(start + c0, kt), :], preferred_element_type=F32)
            pv = t if pv is None else pv + t
        acc_sc[rows, 0:LANES] = alpha * acc_sc[rows, 0:LANES] + pv
        acc_sc[rows, LANES:2 * LANES] = alpha * acc_sc[rows, LANES:2 * LANES] + lsum
        return m_new

    def stage(cur_buf, nxt_buf, nxt_start, nxt_size, cur_start, cur_size, ms, mxs):
        new_ms, new_mxs = [], []
        for r in range(n_rb):
            rows = pl.ds(r * rb, rb)
            s_n, mx_n = qk(r, nxt_start, nxt_size)
            nxt_buf[rows, 0:nxt_size] = s_n
            new_mxs.append(mx_n)
            new_ms.append(softmax_pv(r, cur_buf.at[rows, :], mxs[r], ms[r], cur_start, cur_size))
        return new_ms, new_mxs

    acc_sc[...] = jnp.zeros_like(acc_sc)
    chunks = [(j * tk, tk) for j in range(n_lat // tk)] + [(n_lat, n_ctx)]
    bufs = (sa_sc, sb_sc)
    ms = [jnp.full((rb, 1), -1e30, F32) for _ in range(n_rb)]
    mxs = []
    for r in range(n_rb):
        s, mx = qk(r, *chunks[0])
        sa_sc[pl.ds(r * rb, rb), 0:chunks[0][1]] = s
        mxs.append(mx)
    for j in range(len(chunks) - 1):
        ms, mxs = stage(bufs[j % 2], bufs[(j + 1) % 2], *chunks[j + 1], *chunks[j], ms, mxs)
    last = len(chunks) - 1
    for r in range(n_rb):
        softmax_pv(r, bufs[last % 2].at[pl.ds(r * rb, rb), :], mxs[r], ms[r], *chunks[last])
    acc = acc_sc[...]
    o = acc[:, 0:LANES] / jnp.sum(acc[:, LANES:2 * LANES], axis=-1, keepdims=True)

    if mode == "da":
        lam_ref, sub_ref = extra
        lam = lam_ref[...]
        lam_full = (jnp.exp(jnp.sum(lam[0:1] * lam[1:2], axis=-1, keepdims=True))
                    - jnp.exp(jnp.sum(lam[2:3] * lam[3:4], axis=-1, keepdims=True)) + lam_init)
        dlt = o[0:tq] - lam_full * o[tq:2 * tq]
        ms_d = jnp.mean(dlt * dlt, axis=-1, keepdims=True)
        o_ref[0] = (dlt * lax.rsqrt(ms_d + NORM_EPS) * sub_ref[...] * (1.0 - lam_init)).astype(o_ref.dtype)
    elif mode == "gq":
        o_ref[0] = jnp.concatenate([o[0:tq], o[tq:2 * tq]], axis=-1).astype(o_ref.dtype)
    else:
        o_ref[0] = o.astype(o_ref.dtype)


def _pick_tk(n_lat):
    for tk in (1024, 512, 256):
        if n_lat % tk == 0:
            return tk
    raise ValueError("latent length must be a multiple of 256")


def _attention(mode, q, k, v, n_lat, with_ctx, extra=(), lam_init=0.0):
    b, lt, _ = q.shape
    n_ctx = lt - n_lat
    if mode == "da":
        heads, qw, kw, ow, g = DA_HEADS, 128, 128, 128, 2
    elif mode == "gq":
        heads, qw, kw, ow, g = GQ_KV, 256, 128, 256, 2
    else:
        heads, qw, kw, ow, g = ML_HEADS, 256, 256, 128, 1
    tk = _pick_tk(n_lat)
    n_args = 3 + len(extra)

    def call(ctx_only, prev):
        if ctx_only:
            tq, q0, n_tiles, kv_rows, kv_blk = n_ctx, n_lat // n_ctx, 1, n_ctx, n_lat // n_ctx
        else:
            tq = 2 * ROW_TILE // g if n_lat % (2 * ROW_TILE // g) == 0 else ROW_TILE
            q0, n_tiles, kv_rows, kv_blk = 0, n_lat // tq, lt, 0
        in_specs = [pl.BlockSpec((1, tq, qw), lambda i, h, t: (i, t + q0, h)),
                    pl.BlockSpec((1, kv_rows, kw), lambda i, h, t: (i, kv_blk, h)),
                    pl.BlockSpec((1, kv_rows, 128), lambda i, h, t: (i, kv_blk, h))]
        in_specs += [_const_spec(e.shape) for e in extra]
        args = [q, k, v, *extra]
        inner = functools.partial(_attn_kernel, mode=mode, tq=tq, tk=tk, n_lat=0 if ctx_only else n_lat,
                                  n_extra=len(extra), lam_init=lam_init)
        if prev is None:
            body, aliases = inner, {}
        else:
            in_specs.append(pl.BlockSpec(memory_space=pl.ANY))
            args.append(prev)
            aliases = {n_args: 0}

            def body(*refs):
                inner(*refs[:n_args], *refs[n_args + 1:])
        return pl.pallas_call(
            body,
            grid=(b, heads, n_tiles),
            in_specs=in_specs,
            out_specs=pl.BlockSpec((1, tq, ow), lambda i, h, t: (i, t + q0, h)),
            out_shape=jax.ShapeDtypeStruct((b, lt if with_ctx else n_lat, BRANCH_W), BF16),
            scratch_shapes=[pltpu.VMEM((g * tq, 2 * LANES), F32),
                            pltpu.VMEM((g * tq, max(tk, n_ctx)), F32),
                            pltpu.VMEM((g * tq, max(tk, n_ctx)), F32)],
            input_output_aliases=aliases,
            compiler_params=_cparams(("parallel", "parallel", "arbitrary")),
            name="attn_" + mode + ("_ctx" if ctx_only else ""),
        )(*args)

    out = call(False, None)
    return call(True, out) if with_ctx else out


def _gelu_tanh(x):
    return 0.5 * x * (1.0 + jnp.tanh(math.sqrt(2.0 / math.pi) * (x + 0.044715 * (x * x * x))))


def _merge_kernel(x_ref, mod_ref, y_ref, da_ref, gq_ref, ml_ref, gate_ref,
                  wglu_ref, bglu_ref, wbr_ref, wout_ref, o_ref):
    d = x_ref.shape[-1]
    g = _gelu_tanh(y_ref[0])
    glu = jax.nn.sigmoid(jnp.dot(g.astype(BF16), wglu_ref[...], preferred_element_type=F32)
                         + bglu_ref[...])
    branches = [(g * glu).astype(BF16), da_ref[0], gq_ref[0], ml_ref[0]]
    acc = None
    for n, br in enumerate(branches):
        term = gate_ref[0, :, n * d:(n + 1) * d].astype(F32) * jnp.dot(
            br, wbr_ref[n], preferred_element_type=F32)
        acc = term if acc is None else acc + term
    out = jnp.dot(acc.astype(BF16), wout_ref[...], preferred_element_type=F32)
    g1 = mod_ref[0][:, 2 * d:3 * d]
    o_ref[0] = x_ref[0] + g1 * out


def _ffn_kernel(x_ref, mod_ref, nrm_ref, win_ref, wout_ref, o_ref, *, ff_chunk):
    d = x_ref.shape[-1]
    d_ff = wout_ref.shape[0]
    x = x_ref[0]
    mod = mod_ref[0]
    shift, scale, g2 = mod[:, 3 * d:4 * d], mod[:, 4 * d:5 * d], mod[:, 5 * d:6 * d]
    ms = jnp.mean(x * x, axis=-1, keepdims=True)
    hb = ((x * lax.rsqrt(ms + NORM_EPS) * nrm_ref[...]) * (1.0 + scale) + shift).astype(BF16)
    out = None
    for c0 in range(0, d_ff, ff_chunk):
        gate = jnp.dot(hb, win_ref[:, c0:c0 + ff_chunk], preferred_element_type=F32)
        up = jnp.dot(hb, win_ref[:, d_ff + c0:d_ff + c0 + ff_chunk], preferred_element_type=F32)
        act = (gate * jax.nn.sigmoid(gate) * up).astype(BF16)
        part = jnp.dot(act, wout_ref[c0:c0 + ff_chunk, :], preferred_element_type=F32)
        out = part if out is None else out + part
    o_ref[0] = x + g2 * out


def _mod_index(layer, n_batch, lat_tiles):
    def idx(i, t):
        return (layer * 8 + jnp.where(t >= lat_tiles, n_batch, i), 0, 0)
    return idx


def _merge(xc, mod_rows, layer, n_lat, n_rows, y, da, gq, ml, gates, wglu, bglu, wbr, wout):
    b, _, d = xc.shape
    tm = ROW_TILE
    row = lambda i, t: (i, t, 0)
    return pl.pallas_call(
        _merge_kernel,
        grid=(b, n_rows // tm),
        in_specs=[pl.BlockSpec((1, tm, d), row),
                  pl.BlockSpec((1, 1, mod_rows.shape[-1]), _mod_index(layer, b, n_lat // tm)),
                  pl.BlockSpec((1, tm, BRANCH_W), row), pl.BlockSpec((1, tm, BRANCH_W), row),
                  pl.BlockSpec((1, tm, BRANCH_W), row), pl.BlockSpec((1, tm, BRANCH_W), row),
                  pl.BlockSpec((1, tm, N_BRANCH * d), row),
                  _const_spec(wglu.shape), _const_spec(bglu.shape),
                  _const_spec(wbr.shape), _const_spec(wout.shape)],
        out_specs=pl.BlockSpec((1, tm, d), row),
        out_shape=jax.ShapeDtypeStruct((b, n_rows, d), F32),
        compiler_params=_cparams(("parallel", "parallel")),
        name="merge",
    )(xc, mod_rows, y, da, gq, ml, gates, wglu, bglu, wbr, wout)


def _ffn(x1, mod_rows, layer, n_lat, nrm, win, wout):
    b, n_rows, d = x1.shape
    tm = ROW_TILE
    d_ff = wout.shape[0]
    ff_chunk = d_ff // 2 if (d_ff // 2) % LANES == 0 else d_ff
    row = lambda i, t: (i, t, 0)
    return pl.pallas_call(
        functools.partial(_ffn_kernel, ff_chunk=ff_chunk),
        grid=(b, n_rows // tm),
        in_specs=[pl.BlockSpec((1, tm, d), row),
                  pl.BlockSpec((1, 1, mod_rows.shape[-1]), _mod_index(layer, b, n_lat // tm)),
                  _const_spec(nrm.shape), _const_spec(win.shape), _const_spec(wout.shape)],
        out_specs=pl.BlockSpec((1, tm, d), row),
        out_shape=jax.ShapeDtypeStruct((b, n_rows, d), F32),
        compiler_params=_cparams(("parallel", "parallel")),
        name="ffn",
    )(x1, mod_rows, nrm, win, wout)


def _inproj_columns(d):
    de64, de128 = _deinterleave(64), _deinterleave(128)
    off_da = 512
    off_gq = off_da + 3 * DA_HEADS * DA_DV
    off_ml = off_gq + (GQ_HEADS + 2 * GQ_KV) * GQ_DH
    off_gate = off_ml + ML_QRANK + ML_KVRANK + ML_ROPE
    cols = [np.arange(512)]
    for part in range(2):
        for hd in range(DA_HEADS):
            for comp in range(2):
                cols.append(off_da + part * 512 + hd * 128 + comp * 64 + de64)
    cols.append(off_da + 1024 + np.arange(512))
    for hd in range(GQ_HEADS):
        cols.append(off_gq + hd * 128 + de128)
    for hd in range(GQ_KV):
        cols.append(off_gq + 512 + hd * 128 + de128)
    cols.append(off_gq + 768 + np.arange(256))
    cols.append(off_ml + np.arange(ML_QRANK + ML_KVRANK))
    cols.append(off_ml + ML_QRANK + ML_KVRANK + de64)
    cols.append(off_ml + ML_QRANK + ML_KVRANK + de64)
    cols.append(off_gate + np.arange(N_BRANCH * d))
    return np.concatenate(cols)


def _rope_table(n_lat, n_ctx):
    t = np.arange(n_lat)
    r = (t // GRID_W).astype(np.float32)
    col = (t % GRID_W).astype(np.float32)

    def angles(rot_dim):
        half = rot_dim // 2
        inv = jnp.asarray(ROPE_THETA, F32) ** (-jnp.arange(0, half, 2, dtype=F32) / half)
        return jnp.concatenate([jnp.asarray(r)[:, None] * inv, jnp.asarray(col)[:, None] * inv], axis=-1)

    a64, a128 = angles(64), angles(128)
    c64, s64 = jnp.cos(a64), jnp.sin(a64)
    c128, s128 = jnp.cos(a128), jnp.sin(a128)
    lat = jnp.concatenate([c64, c64, c64, c64, -s64, s64, -s64, s64, c128, c128, -s128, s128], axis=-1)
    ctx = jnp.concatenate([jnp.ones((n_ctx, 128), F32), jnp.zeros((n_ctx, 128), F32),
                           jnp.ones((n_ctx, 128), F32), jnp.zeros((n_ctx, 128), F32)], axis=-1)
    return jnp.concatenate([lat, ctx], axis=0)


def kernel(x, c, ctx, c_ctx, w_mod, b_mod, norm_mix, w_in, s5_lam_re, s5_lam_im, s5_log_step, s5_b_re, s5_b_im, s5_c_re, s5_c_im, s5_d, s5_w_glu, s5_b_glu, da_q_norm, da_k_norm, da_lam, da_subln, gq_q_norm, gq_k_norm, ml_cq_norm, ml_ckv_norm, ml_w_uq, ml_w_ukv, ml_q_norm, ml_k_norm, w_branch, w_out, norm_ffn, w_ffn_in, w_ffn_out):
    b, n_lat, d = x.shape
    n_ctx = ctx.shape[1]
    depth = w_mod.shape[0]
    assert n_ctx == ROW_TILE and n_lat % ROW_TILE == 0 and b < 8 and d == 1024

    cc = jnp.zeros((8, d), F32).at[:b].set(c).at[b].set(c_ctx)
    mod_rows = _modulation(cc, w_mod, b_mod).reshape(depth * 8, 1, 6 * d)
    rope = _rope_table(n_lat, n_ctx)
    xc = jnp.concatenate([x, ctx], axis=1)

    de64, de128 = _deinterleave(64), _deinterleave(128)
    cols = _inproj_columns(d)
    uq_cols = np.concatenate([hd * 192 + np.arange(128) for hd in range(ML_HEADS)]
                             + [hd * 192 + 128 + de64 for hd in range(ML_HEADS)])
    ukv_cols = np.concatenate([hd * 256 + np.arange(128) for hd in range(ML_HEADS)]
                              + [hd * 256 + 128 + np.arange(128) for hd in range(ML_HEADS)])
    log2e = math.log2(math.e)
    da_scale, gq_scale = DA_DK ** -0.5 * log2e, GQ_DH ** -0.5 * log2e
    ml_scale = (ML_NOPE + ML_ROPE) ** -0.5 * log2e

    for i in range(depth):
        last = i == depth - 1
        lam_init = 0.8 - 0.6 * math.exp(-0.3 * i)
        w = w_in[i][:, cols].astype(BF16)
        wuq = ml_w_uq[i][:, uq_cols].astype(BF16)
        wukv = ml_w_ukv[i][:, ukv_cols].astype(BF16)
        gains = jnp.concatenate([
            jnp.tile(da_q_norm[i][de64], 8) * da_scale, jnp.tile(da_k_norm[i][de64], 8),
            jnp.tile(gq_q_norm[i][de128], 4) * gq_scale, jnp.tile(gq_k_norm[i][de128], 2),
            ml_cq_norm[i], ml_ckv_norm[i],
            jnp.tile(ml_q_norm[i][:ML_NOPE], 4) * ml_scale,
            jnp.tile(ml_q_norm[i][ML_NOPE:][de64], 4) * ml_scale,
            jnp.tile(ml_k_norm[i][:ML_NOPE], 4), jnp.tile(ml_k_norm[i][ML_NOPE:][de64], 2),
        ]).astype(F32)[None, :]

        (u, daq, dak, dav, gqq, gqk, gqv, mlq, mlk, mlv, gates) = _inproj(
            xc, mod_rows, i, b, n_lat, norm_mix[i][None, :], gains, rope, w, wuq, wukv)

        s5_ops = _s5_prep(s5_lam_re[i], s5_lam_im[i], s5_log_step[i], s5_b_re[i], s5_b_im[i],
                          s5_c_re[i], s5_c_im[i], s5_d[i])
        y = _s5(u, n_lat, s5_ops)

        n_rows = n_lat if last else n_lat + n_ctx
        da = _attention("da", daq, dak, dav, n_lat, not last,
                        extra=(da_lam[i].astype(F32), da_subln[i].astype(F32)[None, :]),
                        lam_init=lam_init)
        gq = _attention("gq", gqq, gqk, gqv, n_lat, not last)
        ml = _attention("ml", mlq, mlk, mlv, n_lat, not last)

        x1 = _merge(xc, mod_rows, i, n_lat, n_rows, y, da, gq, ml, gates,
                    s5_w_glu[i].astype(BF16), s5_b_glu[i].astype(F32)[None, :],
                    w_branch[i].astype(BF16), w_out[i].astype(BF16))
        xc = _ffn(x1, mod_rows, i, n_lat, norm_ffn[i][None, :],
                  w_ffn_in[i].astype(BF16), w_ffn_out[i].astype(BF16))
    return xc
```

```python
import functools
import math

import numpy as np
import jax
import jax.numpy as jnp
from jax import lax
from jax.experimental import pallas as pl
from jax.experimental.pallas import tpu as pltpu

GRID_W = 64
ROPE_THETA = 10000.0
NORM_EPS = 1e-6

S5_GROUP = 16
S5_STATE = 64
S5_CHUNK = 16
S5_ROWS = 8

DA_HEADS = 4
DA_DK = 64
DA_DV = 128
GQ_HEADS = 4
GQ_KV = 2
GQ_DH = 128
ML_HEADS = 4
ML_QRANK = 256
ML_KVRANK = 128
ML_NOPE = 128
ML_ROPE = 64
ML_DV = 128
N_BRANCH = 4
BRANCH_W = 512

LANES = 128
ROW_TILE = 256
VMEM_LIMIT = 56 * 1024 * 1024

F32 = jnp.float32
BF16 = jnp.bfloat16


def _deinterleave(n):
    return np.concatenate([np.arange(0, n, 2), np.arange(1, n, 2)])


def _cparams(sem):
    return pltpu.CompilerParams(dimension_semantics=sem, vmem_limit_bytes=VMEM_LIMIT)


def _const_spec(shape):
    nd = len(shape)
    return pl.BlockSpec(shape, lambda *_: (0,) * nd)


def _mod_kernel(cc_ref, w_ref, b_ref, o_ref):
    a = cc_ref[...]
    a = a * jax.nn.sigmoid(a)
    o_ref[0] = jnp.dot(a, w_ref[0], preferred_element_type=F32,
                       precision=lax.Precision.HIGHEST) + b_ref[0]


def _modulation(cc, w_mod, b_mod):
    depth, d, n = w_mod.shape
    tn = n // 4
    return pl.pallas_call(
        _mod_kernel,
        grid=(depth, n // tn),
        in_specs=[pl.BlockSpec((8, d), lambda i, j: (0, 0)),
                  pl.BlockSpec((1, d, tn), lambda i, j: (i, 0, j)),
                  pl.BlockSpec((1, 1, tn), lambda i, j: (i, 0, j))],
        out_specs=pl.BlockSpec((1, 8, tn), lambda i, j: (i, 0, j)),
        out_shape=jax.ShapeDtypeStruct((depth, 8, n), F32),
        compiler_params=_cparams(("arbitrary", "arbitrary")),
        name="modulation",
    )(cc, w_mod, b_mod.reshape(depth, 1, n))


def _seg_rms(z, seg):
    width = z.shape[-1]
    if seg == 2 * LANES:
        outs = []
        for g in range(width // seg):
            zg = z[:, g * seg:(g + 1) * seg]
            ms = jnp.sum(zg * zg, axis=-1, keepdims=True) * (1.0 / seg)
            outs.append(zg * lax.rsqrt(ms + NORM_EPS))
        return outs[0] if len(outs) == 1 else jnp.concatenate(outs, axis=-1)
    outs = []
    for g in range(width // LANES):
        zg = z[:, g * LANES:(g + 1) * LANES]
        zz = zg * zg
        if seg == LANES:
            ms = jnp.sum(zz, axis=-1, keepdims=True) * (1.0 / seg)
        else:
            lo = lax.broadcasted_iota(jnp.int32, zz.shape, 1) < seg
            s_lo = jnp.sum(jnp.where(lo, zz, 0.0), axis=-1, keepdims=True)
            s_hi = jnp.sum(jnp.where(lo, 0.0, zz), axis=-1, keepdims=True)
            ms = jnp.where(lo, s_lo, s_hi) * (1.0 / seg)
        outs.append(zg * lax.rsqrt(ms + NORM_EPS))
    return outs[0] if len(outs) == 1 else jnp.concatenate(outs, axis=-1)


def _rope(x, cos, sin, unit):
    outs = []
    for g in range(x.shape[-1] // LANES):
        xg = x[:, g * LANES:(g + 1) * LANES]
        if unit == LANES:
            rot = pltpu.roll(xg, LANES // 2, 1)
        else:
            lane = lax.broadcasted_iota(jnp.int32, xg.shape, 1)
            rot = jnp.where((lane & (unit // 2)) == 0,
                            pltpu.roll(xg, LANES - unit // 2, 1), pltpu.roll(xg, unit // 2, 1))
        outs.append(xg * cos + rot * sin)
    return outs[0] if len(outs) == 1 else jnp.concatenate(outs, axis=-1)


_C_S5 = 0
_C_DAQ = 512
_C_DAK = 1024
_C_DAV = 1536
_C_GQQ = 2048
_C_GQK = 2560
_C_GQV = 2816
_C_ML = 3072
_C_GATE = 3584
_C_END = 3584 + 4096

_G_DAQ, _G_DAK, _G_GQQ, _G_GQK = 0, 512, 1024, 1536
_G_CQ, _G_CKV, _G_QN, _G_QR, _G_KN, _G_KR, _G_END = 1792, 2048, 2176, 2688, 2944, 3456, 3584


def _inproj_kernel(x_ref, mod_ref, nrm_ref, gains_ref, rope_ref, w_ref, wuq_ref, wukv_ref,
                   u_ref, daq_ref, dak_ref, dav_ref, gqq_ref, gqk_ref, gqv_ref,
                   mlq_ref, mlk_ref, mlv_ref, gate_ref):
    d = x_ref.shape[-1]
    x = x_ref[0]
    ms = jnp.mean(x * x, axis=-1, keepdims=True)
    mod = mod_ref[0]
    shift, scale = mod[:, 0:d], mod[:, d:2 * d]
    h = (x * lax.rsqrt(ms + NORM_EPS) * nrm_ref[...]) * (1.0 + scale) + shift
    hb = h.astype(BF16)

    def proj(c0, c1):
        return jnp.dot(hb, w_ref[:, c0:c1], preferred_element_type=F32)

    def gain(g0, g1):
        return gains_ref[:, g0:g1]

    cos64, sin64 = rope_ref[:, 0:128], rope_ref[:, 128:256]
    cos128, sin128 = rope_ref[:, 256:384], rope_ref[:, 384:512]

    u_ref[0] = proj(_C_S5, _C_DAQ).astype(BF16)

    q = _seg_rms(proj(_C_DAQ, _C_DAK), DA_DK) * gain(_G_DAQ, _G_DAK)
    daq_ref[0] = _rope(q, cos64, sin64, DA_DK).astype(BF16)
    k = _seg_rms(proj(_C_DAK, _C_DAV), DA_DK) * gain(_G_DAK, _G_GQQ)
    dak_ref[0] = _rope(k, cos64, sin64, DA_DK).astype(BF16)
    dav_ref[0] = proj(_C_DAV, _C_GQQ).astype(BF16)

    q = _seg_rms(proj(_C_GQQ, _C_GQK), GQ_DH) * gain(_G_GQQ, _G_GQK)
    gqq_ref[0] = _rope(q, cos128, sin128, GQ_DH).astype(BF16)
    k = _seg_rms(proj(_C_GQK, _C_GQV), GQ_DH) * gain(_G_GQK, _G_CQ)
    gqk_ref[0] = _rope(k, cos128, sin128, GQ_DH).astype(BF16)
    gqv_ref[0] = proj(_C_GQV, _C_ML).astype(BF16)

    zc = proj(_C_ML, _C_GATE)
    cq = (_seg_rms(zc[:, 0:256], 256) * gain(_G_CQ, _G_CKV)).astype(BF16)
    ckv = (_seg_rms(zc[:, 256:384], 128) * gain(_G_CKV, _G_QN)).astype(BF16)
    qq = jnp.dot(cq, wuq_ref[...], preferred_element_type=F32)
    kv = jnp.dot(ckv, wukv_ref[...], preferred_element_type=F32)
    qn = (_seg_rms(qq[:, 0:512], ML_NOPE) * gain(_G_QN, _G_QR)).astype(BF16)
    qr = _seg_rms(qq[:, 512:768], ML_ROPE) * gain(_G_QR, _G_KN)
    qr = _rope(qr, cos64, sin64, ML_ROPE)
    kn = (_seg_rms(kv[:, 0:512], ML_NOPE) * gain(_G_KN, _G_KR)).astype(BF16)
    kr = _seg_rms(zc[:, 384:512], ML_ROPE) * gain(_G_KR, _G_END)
    kr = _rope(kr, cos64, sin64, ML_ROPE).astype(BF16)
    lo = lax.broadcasted_iota(jnp.int32, (x.shape[0], LANES), 1) < ML_ROPE
    for hd in range(ML_HEADS):
        pair = qr[:, (hd // 2) * LANES:(hd // 2 + 1) * LANES]
        keep = lo if hd % 2 == 0 else jnp.logical_not(lo)
        mlq_ref[0, :, hd * 256:hd * 256 + 128] = qn[:, hd * 128:(hd + 1) * 128]
        mlq_ref[0, :, hd * 256 + 128:(hd + 1) * 256] = jnp.where(keep, pair, 0.0).astype(BF16)
        mlk_ref[0, :, hd * 256:hd * 256 + 128] = kn[:, hd * 128:(hd + 1) * 128]
        mlk_ref[0, :, hd * 256 + 128:(hd + 1) * 256] = kr
    mlv_ref[0] = kv[:, 512:1024].astype(BF16)

    for j in range(N_BRANCH):
        c0 = _C_GATE + j * d
        gate_ref[0, :, j * d:(j + 1) * d] = jax.nn.sigmoid(proj(c0, c0 + d)).astype(BF16)


def _inproj(xc, mod_rows, layer, n_batch, n_lat, nrm, gains, rope, w, wuq, wukv):
    b, lt, d = xc.shape
    tm = ROW_TILE
    nt = lt // tm
    lat_tiles = n_lat // tm

    def mod_idx(i, t):
        return (layer * 8 + jnp.where(t >= lat_tiles, n_batch, i), 0, 0)

    widths = [512, 512, 512, 512, 512, 256, 256, 1024, 1024, 512, N_BRANCH * d]
    return pl.pallas_call(
        _inproj_kernel,
        grid=(b, nt),
        in_specs=[pl.BlockSpec((1, tm, d), lambda i, t: (i, t, 0)),
                  pl.BlockSpec((1, 1, mod_rows.shape[-1]), mod_idx),
                  _const_spec(nrm.shape), _const_spec(gains.shape),
                  pl.BlockSpec((tm, 512), lambda i, t: (t, 0)),
                  _const_spec(w.shape), _const_spec(wuq.shape), _const_spec(wukv.shape)],
        out_specs=[pl.BlockSpec((1, tm, wd), lambda i, t: (i, t, 0)) for wd in widths],
        out_shape=[jax.ShapeDtypeStruct((b, lt, wd), BF16) for wd in widths],
        compiler_params=_cparams(("parallel", "parallel")),
        name="inproj",
    )(xc, mod_rows, nrm, gains, rope, w, wuq, wukv)


def _s5_kernel(u_ref, m_ref, ws_ref, wo_ref, are_ref, aim_ref, y_ref,
               s_sc, hfre_sc, hfim_sc, hbre_sc, hbim_sc, *, n_chunks, n_lat_chunks):
    rows = S5_ROWS
    u = u_ref[0]
    s_sc[...] = jnp.dot(u, ws_ref[0], preferred_element_type=F32)
    a_re = jnp.broadcast_to(are_ref[0], (rows, LANES))
    a_im = jnp.broadcast_to(aim_ref[0], (rows, LANES))
    is_fwd = lax.broadcasted_iota(jnp.int32, (rows, LANES), 1) < S5_STATE
    n_ctx_chunks = n_chunks - n_lat_chunks

    def step(i, carry):
        h_re, h_im = carry
        cf = jnp.where(i < n_ctx_chunks, n_lat_chunks + i, i - n_ctx_chunks)
        cb = n_chunks - 1 - i
        rf = pl.ds(pl.multiple_of(cf * rows, rows), rows)
        rb = pl.ds(pl.multiple_of(cb * rows, rows), rows)
        hfre_sc[rf, :] = h_re
        hfim_sc[rf, :] = h_im
        hbre_sc[rb, :] = h_re
        hbim_sc[rb, :] = h_im
        s_re = jnp.where(is_fwd, s_sc[rf, 0:LANES], s_sc[rb, 0:LANES])
        s_im = jnp.where(is_fwd, s_sc[rf, LANES:2 * LANES], s_sc[rb, LANES:2 * LANES])
        return (a_re * h_re - a_im * h_im + s_re, a_re * h_im + a_im * h_re + s_im)

    zero = jnp.zeros((rows, LANES), F32)
    lax.fori_loop(0, n_chunks, step, (zero, zero))

    fwd_lane = lax.broadcasted_iota(jnp.int32, hfre_sc.shape, 1) < S5_STATE
    h_in = jnp.concatenate([jnp.where(fwd_lane, hfre_sc[...], hbre_sc[...]),
                            jnp.where(fwd_lane, hfim_sc[...], hbim_sc[...])], axis=-1).astype(BF16)
    y_ref[0] = (jnp.dot(u, m_ref[0], preferred_element_type=F32)
                + jnp.dot(h_in, wo_ref[0], preferred_element_type=F32))


def _s5_prep(lam_re, lam_im, log_step, b_re, b_im, c_re, c_im, d_skip):
    hp = lax.Precision.HIGHEST
    t_len, n_g, n_p, n_c = S5_CHUNK, lam_re.shape[1], S5_STATE, S5_GROUP
    lam_re, lam_im = lam_re.astype(F32), lam_im.astype(F32)
    dt = jnp.exp(log_step.astype(F32))[..., None]
    mag = jnp.exp(lam_re * dt)
    ab_re, ab_im = mag * jnp.cos(lam_im * dt), mag * jnp.sin(lam_im * dt)
    den = lam_re * lam_re + lam_im * lam_im
    nr, ni = ab_re - 1.0, ab_im
    coef_re = (nr * lam_re + ni * lam_im) / den
    coef_im = (ni * lam_re - nr * lam_im) / den
    b_re, b_im = b_re.astype(F32), b_im.astype(F32)
    bb_re = coef_re[..., None] * b_re - coef_im[..., None] * b_im
    bb_im = coef_re[..., None] * b_im + coef_im[..., None] * b_re
    kk = jnp.arange(t_len + 1, dtype=F32)[:, None, None, None]
    pmag = jnp.exp(lam_re * dt * kk)
    pw_re, pw_im = pmag * jnp.cos(lam_im * dt * kk), pmag * jnp.sin(lam_im * dt * kk)
    c_re, c_im = c_re.astype(F32), c_im.astype(F32)
    cp_re = c_re[None] * pw_re[:, :, :, None, :] - c_im[None] * pw_im[:, :, :, None, :]
    cp_im = c_re[None] * pw_im[:, :, :, None, :] + c_im[None] * pw_re[:, :, :, None, :]
    kern = (jnp.einsum('tdgop,dgpc->tdgoc', cp_re, bb_re, precision=hp)
            - jnp.einsum('tdgop,dgpc->tdgoc', cp_im, bb_im, precision=hp))
    s_idx = np.arange(t_len)[:, None]
    t_idx = np.arange(t_len)[None, :]
    lag_f = np.clip(t_idx - s_idx, 0, t_len)
    lag_b = np.clip(s_idx - t_idx, 0, t_len)
    mf = kern[lag_f, 0] * jnp.asarray(t_idx >= s_idx, F32)[:, :, None, None, None]
    mb = kern[lag_b, 1] * jnp.asarray(s_idx >= t_idx, F32)[:, :, None, None, None]
    m_full = jnp.transpose(mf + mb, (2, 0, 4, 1, 3))
    eye_t = jnp.eye(t_len, dtype=F32)[None, :, None, :, None]
    eye_c = jnp.eye(n_c, dtype=F32)[None, None, :, None, :]
    m_full = m_full + eye_t * eye_c * d_skip.astype(F32).reshape(n_g, 1, n_c, 1, 1)
    m_full = m_full.reshape(n_g, t_len * n_c, t_len * n_c)

    def bpow(pw_r, pw_i, direction):
        re = pw_r[..., None] * bb_re[direction][None] - pw_i[..., None] * bb_im[direction][None]
        im = pw_r[..., None] * bb_im[direction][None] + pw_i[..., None] * bb_re[direction][None]
        return jnp.transpose(re, (1, 0, 3, 2)), jnp.transpose(im, (1, 0, 3, 2))

    rev = np.arange(t_len - 1, -1, -1)
    fwd = np.arange(t_len)
    f_re, f_im = bpow(pw_re[rev, 0], pw_im[rev, 0], 0)
    g_re, g_im = bpow(pw_re[fwd, 1], pw_im[fwd, 1], 1)
    ws = jnp.concatenate([f_re, g_re, f_im, g_im], axis=-1).reshape(n_g, t_len * n_c, 4 * n_p)

    def cpow(idx, direction):
        return (jnp.transpose(cp_re[idx, direction], (1, 3, 0, 2)),
                jnp.transpose(cp_im[idx, direction], (1, 3, 0, 2)))

    of_re, of_im = cpow(np.arange(1, t_len + 1), 0)
    ob_re, ob_im = cpow(np.arange(t_len, 0, -1), 1)
    wo = jnp.concatenate([of_re, ob_re, -of_im, -ob_im], axis=1).reshape(n_g, 4 * n_p, t_len * n_c)
    a_re = jnp.concatenate([pw_re[t_len, 0], pw_re[t_len, 1]], axis=-1)[:, None, :]
    a_im = jnp.concatenate([pw_im[t_len, 0], pw_im[t_len, 1]], axis=-1)[:, None, :]
    return m_full.astype(BF16), ws.astype(BF16), wo.astype(BF16), a_re, a_im


def _s5(u, n_lat, ops):
    m_full, ws, wo, a_re, a_im = ops
    b, lt, width = u.shape
    n_g = width // S5_GROUP
    n_chunks = lt // S5_CHUNK
    cw = S5_CHUNK * S5_GROUP
    ug = u.reshape(b, n_chunks, S5_CHUNK, n_g, S5_GROUP).transpose(3, 1, 0, 2, 4)
    ug = jnp.pad(ug.reshape(n_g, n_chunks, b, cw), ((0, 0), (0, 0), (0, S5_ROWS - b), (0, 0)))
    rows = n_chunks * S5_ROWS
    ug = ug.reshape(n_g, rows, cw)
    grp = lambda g: (g, 0, 0)
    y = pl.pallas_call(
        functools.partial(_s5_kernel, n_chunks=n_chunks, n_lat_chunks=n_lat // S5_CHUNK),
        grid=(n_g,),
        in_specs=[pl.BlockSpec((1, rows, cw), grp),
                  pl.BlockSpec((1, cw, cw), grp), pl.BlockSpec((1, cw, 4 * S5_STATE), grp),
                  pl.BlockSpec((1, 4 * S5_STATE, cw), grp),
                  pl.BlockSpec((1, 1, LANES), grp), pl.BlockSpec((1, 1, LANES), grp)],
        out_specs=pl.BlockSpec((1, rows, cw), grp),
        out_shape=jax.ShapeDtypeStruct((n_g, rows, cw), F32),
        scratch_shapes=[pltpu.VMEM((rows, 4 * S5_STATE), F32)] + [pltpu.VMEM((rows, LANES), F32)] * 4,
        compiler_params=_cparams(("parallel",)),
        name="s5_scan",
    )(ug, m_full, ws, wo, a_re, a_im)
    y = y.reshape(n_g, n_chunks, S5_ROWS, S5_CHUNK, S5_GROUP)[:, :, :b]
    return y.transpose(2, 1, 3, 0, 4).reshape(b, lt, width)


ATTN_ROW_BLOCKS = 2
ATTN_KEY_TILE = 256


def _attn_kernel(*refs, mode, tq, tk, n_lat, n_extra, lam_init):
    q_ref, k_ref, v_ref = refs[:3]
    extra = refs[3:3 + n_extra]
    o_ref, acc_sc, sa_sc, sb_sc = refs[-4:]
    n_keys = k_ref.shape[1]
    n_ctx = n_keys - n_lat

    q = q_ref[0]
    if mode == "da":
        lo = lax.broadcasted_iota(jnp.int32, q.shape, 1) < DA_DK
        zero = jnp.zeros_like(q)
        qs = jnp.concatenate([jnp.where(lo, q, zero), jnp.where(lo, zero, q)], axis=0)
    elif mode == "gq":
        qs = jnp.concatenate([q[:, 0:GQ_DH], q[:, GQ_DH:2 * GQ_DH]], axis=0)
    else:
        qs = q
    n_rb = ATTN_ROW_BLOCKS
    rb = qs.shape[0] // n_rb
    q_blocks = [qs[r * rb:(r + 1) * rb] for r in range(n_rb)]

    def qk(r, start, size):
        kc = k_ref[0, pl.ds(start, size), :]
        s = lax.dot_general(q_blocks[r], kc, (((1,), (1,)), ((), ())), preferred_element_type=F32)
        return s, jnp.max(s, axis=-1, keepdims=True)

    def softmax_pv(r, s_buf, mx, m_old, start, size):
        m_new = jnp.maximum(m_old, mx)
        alpha = jnp.exp2(m_old - m_new)
        rows = pl.ds(r * rb, rb)
        kt = min(ATTN_KEY_TILE, size)
        pv, lsum = None, None
        for c0 in range(0, size, kt):
            p = jnp.exp2((s_buf[:, c0:c0 + kt] - m_new).astype(BF16))
            p32 = p.astype(F32)
            for i in range(kt // LANES):
                t = p32[:, i * LANES:(i + 1) * LANES]
                lsum = t if lsum is None else lsum + t
            t = jnp.dot(p, v_ref[0, pl.ds(start + c0, kt), :], preferred_element_type=F32)
            pv = t if pv is None else pv + t
        acc_sc[rows, 0:LANES] = alpha * acc_sc[rows, 0:LANES] + pv
        acc_sc[rows, LANES:2 * LANES] = alpha * acc_sc[rows, LANES:2 * LANES] + lsum
        return m_new

    def stage(cur_buf, nxt_buf, nxt_start, nxt_size, cur_start, cur_size, ms, mxs):
        new_ms, new_mxs = [], []
        for r in range(n_rb):
            rows = pl.ds(r * rb, rb)
            s_n, mx_n = qk(r, nxt_start, nxt_size)
            nxt_buf[rows, 0:nxt_size] = s_n
            new_mxs.append(mx_n)
            new_ms.append(softmax_pv(r, cur_buf.at[rows, :], mxs[r], ms[r], cur_start, cur_size))
        return new_ms, new_mxs

    acc_sc[...] = jnp.zeros_like(acc_sc)
    chunks = [(j * tk, tk) for j in range(n_lat // tk)] + [(n_lat, n_ctx)]
    bufs = (sa_sc, sb_sc)
    ms = [jnp.full((rb, 1), -1e30, F32) for _ in range(n_rb)]
    mxs = []
    for r in range(n_rb):
        s, mx = qk(r, *chunks[0])
        sa_sc[pl.ds(r * rb, rb), 0:chunks[0][1]] = s
        mxs.append(mx)
    for j in range(len(chunks) - 1):
        ms, mxs = stage(bufs[j % 2], bufs[(j + 1) % 2], *chunks[j + 1], *chunks[j], ms, mxs)
    last = len(chunks) - 1
    for r in range(n_rb):
        softmax_pv(r, bufs[last % 2].at[pl.ds(r * rb, rb), :], mxs[r], ms[r], *chunks[last])
    acc = acc_sc[...]
    o = acc[:, 0:LANES] / jnp.sum(acc[:, LANES:2 * LANES], axis=-1, keepdims=True)

    if mode == "da":
        lam_ref, sub_ref = extra
        lam = lam_ref[...]
        lam_full = (jnp.exp(jnp.sum(lam[0:1] * lam[1:2], axis=-1, keepdims=True))
                    - jnp.exp(jnp.sum(lam[2:3] * lam[3:4], axis=-1, keepdims=True)) + lam_init)
        dlt = o[0:tq] - lam_full * o[tq:2 * tq]
        ms_d = jnp.mean(dlt * dlt, axis=-1, keepdims=True)
        o_ref[0] = (dlt * lax.rsqrt(ms_d + NORM_EPS) * sub_ref[...] * (1.0 - lam_init)).astype(o_ref.dtype)
    elif mode == "gq":
        o_ref[0] = jnp.concatenate([o[0:tq], o[tq:2 * tq]], axis=-1).astype(o_ref.dtype)
    else:
        o_ref[0] = o.astype(o_ref.dtype)


def _pick_tk(n_lat):
    for tk in (1024, 512, 256):
        if n_lat % tk == 0:
            return tk
    raise ValueError("latent length must be a multiple of 256")


def _attention(mode, q, k, v, n_lat, with_ctx, extra=(), lam_init=0.0):
    b, lt, _ = q.shape
    n_ctx = lt - n_lat
    if mode == "da":
        heads, qw, kw, ow, g = DA_HEADS, 128, 128, 128, 2
    elif mode == "gq":
        heads, qw, kw, ow, g = GQ_KV, 256, 128, 256, 2
    else:
        heads, qw, kw, ow, g = ML_HEADS, 256, 256, 128, 1
    tk = _pick_tk(n_lat)

    def call(ctx_only):
        if ctx_only:
            tq, q0, n_tiles, kv_rows, kv_blk = n_ctx, n_lat // n_ctx, 1, n_ctx, n_lat // n_ctx
        else:
            tq = 2 * ROW_TILE // g if n_lat % (2 * ROW_TILE // g) == 0 else ROW_TILE
            q0, n_tiles, kv_rows, kv_blk = 0, n_lat // tq, lt, 0
        in_specs = [pl.BlockSpec((1, tq, qw), lambda i, h, t: (i, t + q0, h)),
                    pl.BlockSpec((1, kv_rows, kw), lambda i, h, t: (i, kv_blk, h)),
                    pl.BlockSpec((1, kv_rows, 128), lambda i, h, t: (i, kv_blk, h))]
        in_specs += [_const_spec(e.shape) for e in extra]
        return pl.pallas_call(
            functools.partial(_attn_kernel, mode=mode, tq=tq, tk=tk, n_lat=0 if ctx_only else n_lat,
                              n_extra=len(extra), lam_init=lam_init),
            grid=(b, heads, n_tiles),
            in_specs=in_specs,
            out_specs=pl.BlockSpec((1, tq, ow), lambda i, h, t: (i, t, h)),
            out_shape=jax.ShapeDtypeStruct((b, n_tiles * tq, BRANCH_W), BF16),
            scratch_shapes=[pltpu.VMEM((g * tq, 2 * LANES), F32),
                            pltpu.VMEM((g * tq, max(tk, n_ctx)), F32),
                            pltpu.VMEM((g * tq, max(tk, n_ctx)), F32)],
            compiler_params=_cparams(("parallel", "parallel", "arbitrary")),
            name="attn_" + mode + ("_ctx" if ctx_only else ""),
        )(q, k, v, *extra)

    return call(False), (call(True) if with_ctx else None)


def _gelu_tanh(x):
    return 0.5 * x * (1.0 + jnp.tanh(math.sqrt(2.0 / math.pi) * (x + 0.044715 * (x * x * x))))


def _merge_kernel(*refs, lat_tiles, with_ctx):
    x_ref, mod_ref, y_ref, gate_ref, wglu_ref, bglu_ref, wbr_ref, wout_ref = refs[:8]
    lat_refs = refs[8:11]
    ctx_refs = refs[11:14] if with_ctx else None
    o_ref = refs[-1]
    d = x_ref.shape[-1]
    g = _gelu_tanh(y_ref[0])
    glu = jax.nn.sigmoid(jnp.dot(g.astype(BF16), wglu_ref[...], preferred_element_type=F32)
                         + bglu_ref[...])
    branches = [(g * glu).astype(BF16)]
    for n in range(3):
        if with_ctx:
            is_ctx = pl.program_id(1) >= lat_tiles
            branches.append(jnp.where(is_ctx, ctx_refs[n][0], lat_refs[n][0]))
        else:
            branches.append(lat_refs[n][0])
    acc = None
    for n, br in enumerate(branches):
        term = gate_ref[0, :, n * d:(n + 1) * d].astype(F32) * jnp.dot(
            br, wbr_ref[n], preferred_element_type=F32)
        acc = term if acc is None else acc + term
    out = jnp.dot(acc.astype(BF16), wout_ref[...], preferred_element_type=F32)
    g1 = mod_ref[0][:, 2 * d:3 * d]
    o_ref[0] = x_ref[0] + g1 * out


def _ffn_kernel(x_ref, mod_ref, nrm_ref, win_ref, wout_ref, o_ref, *, ff_chunk):
    d = x_ref.shape[-1]
    d_ff = wout_ref.shape[0]
    x = x_ref[0]
    mod = mod_ref[0]
    shift, scale, g2 = mod[:, 3 * d:4 * d], mod[:, 4 * d:5 * d], mod[:, 5 * d:6 * d]
    ms = jnp.mean(x * x, axis=-1, keepdims=True)
    hb = ((x * lax.rsqrt(ms + NORM_EPS) * nrm_ref[...]) * (1.0 + scale) + shift).astype(BF16)
    out = None
    for c0 in range(0, d_ff, ff_chunk):
        gate = jnp.dot(hb, win_ref[:, c0:c0 + ff_chunk], preferred_element_type=F32)
        up = jnp.dot(hb, win_ref[:, d_ff + c0:d_ff + c0 + ff_chunk], preferred_element_type=F32)
        act = (gate * jax.nn.sigmoid(gate) * up).astype(BF16)
        part = jnp.dot(act, wout_ref[c0:c0 + ff_chunk, :], preferred_element_type=F32)
        out = part if out is None else out + part
    o_ref[0] = x + g2 * out


def _mod_index(layer, n_batch, lat_tiles):
    def idx(i, t):
        return (layer * 8 + jnp.where(t >= lat_tiles, n_batch, i), 0, 0)
    return idx


def _merge(xc, mod_rows, layer, n_lat, n_rows, y, attn, gates, wglu, bglu, wbr, wout):
    b, _, d = xc.shape
    tm = ROW_TILE
    lat_tiles = n_lat // tm
    with_ctx = n_rows > n_lat
    row = lambda i, t: (i, t, 0)
    lat_row = lambda i, t: (i, jnp.minimum(t, lat_tiles - 1), 0)
    in_specs = [pl.BlockSpec((1, tm, d), row),
                pl.BlockSpec((1, 1, mod_rows.shape[-1]), _mod_index(layer, b, lat_tiles)),
                pl.BlockSpec((1, tm, BRANCH_W), row),
                pl.BlockSpec((1, tm, N_BRANCH * d), row),
                _const_spec(wglu.shape), _const_spec(bglu.shape),
                _const_spec(wbr.shape), _const_spec(wout.shape)]
    in_specs += [pl.BlockSpec((1, tm, BRANCH_W), lat_row)] * 3
    args = [xc, mod_rows, y, gates, wglu, bglu, wbr, wout] + [a[0] for a in attn]
    if with_ctx:
        in_specs += [pl.BlockSpec((1, tm, BRANCH_W), lambda i, t: (i, 0, 0))] * 3
        args += [a[1] for a in attn]
    return pl.pallas_call(
        functools.partial(_merge_kernel, lat_tiles=lat_tiles, with_ctx=with_ctx),
        grid=(b, n_rows // tm),
        in_specs=in_specs,
        out_specs=pl.BlockSpec((1, tm, d), row),
        out_shape=jax.ShapeDtypeStruct((b, n_rows, d), F32),
        compiler_params=_cparams(("parallel", "parallel")),
        name="merge",
    )(*args)


def _ffn(x1, mod_rows, layer, n_lat, nrm, win, wout):
    b, n_rows, d = x1.shape
    tm = ROW_TILE
    d_ff = wout.shape[0]
    ff_chunk = d_ff // 2 if (d_ff // 2) % LANES == 0 else d_ff
    row = lambda i, t: (i, t, 0)
    return pl.pallas_call(
        functools.partial(_ffn_kernel, ff_chunk=ff_chunk),
        grid=(b, n_rows // tm),
        in_specs=[pl.BlockSpec((1, tm, d), row),
                  pl.BlockSpec((1, 1, mod_rows.shape[-1]), _mod_index(layer, b, n_lat // tm)),
                  _const_spec(nrm.shape), _const_spec(win.shape), _const_spec(wout.shape)],
        out_specs=pl.BlockSpec((1, tm, d), row),
        out_shape=jax.ShapeDtypeStruct((b, n_rows, d), F32),
        compiler_params=_cparams(("parallel", "parallel")),
        name="ffn",
    )(x1, mod_rows, nrm, win, wout)


def _inproj_columns(d):
    de64, de128 = _deinterleave(64), _deinterleave(128)
    off_da = 512
    off_gq = off_da + 3 * DA_HEADS * DA_DV
    off_ml = off_gq + (GQ_HEADS + 2 * GQ_KV) * GQ_DH
    off_gate = off_ml + ML_QRANK + ML_KVRANK + ML_ROPE
    cols = [np.arange(512)]
    for part in range(2):
        for hd in range(DA_HEADS):
            for comp in range(2):
                cols.append(off_da + part * 512 + hd * 128 + comp * 64 + de64)
    cols.append(off_da + 1024 + np.arange(512))
    for hd in range(GQ_HEADS):
        cols.append(off_gq + hd * 128 + de128)
    for hd in range(GQ_KV):
        cols.append(off_gq + 512 + hd * 128 + de128)
    cols.append(off_gq + 768 + np.arange(256))
    cols.append(off_ml + np.arange(ML_QRANK + ML_KVRANK))
    cols.append(off_ml + ML_QRANK + ML_KVRANK + de64)
    cols.append(off_ml + ML_QRANK + ML_KVRANK + de64)
    cols.append(off_gate + np.arange(N_BRANCH * d))
    return np.concatenate(cols)


def _rope_table(n_lat, n_ctx):
    t = np.arange(n_lat)
    r = (t // GRID_W).astype(np.float32)
    col = (t % GRID_W).astype(np.float32)

    def angles(rot_dim):
        half = rot_dim // 2
        inv = jnp.asarray(ROPE_THETA, F32) ** (-jnp.arange(0, half, 2, dtype=F32) / half)
        return jnp.concatenate([jnp.asarray(r)[:, None] * inv, jnp.asarray(col)[:, None] * inv], axis=-1)

    a64, a128 = angles(64), angles(128)
    c64, s64 = jnp.cos(a64), jnp.sin(a64)
    c128, s128 = jnp.cos(a128), jnp.sin(a128)
    lat = jnp.concatenate([c64, c64, c64, c64, -s64, s64, -s64, s64, c128, c128, -s128, s128], axis=-1)
    ctx = jnp.concatenate([jnp.ones((n_ctx, 128), F32), jnp.zeros((n_ctx, 128), F32),
                           jnp.ones((n_ctx, 128), F32), jnp.zeros((n_ctx, 128), F32)], axis=-1)
    return jnp.concatenate([lat, ctx], axis=0)


def kernel(x, c, ctx, c_ctx, w_mod, b_mod, norm_mix, w_in, s5_lam_re, s5_lam_im, s5_log_step, s5_b_re, s5_b_im, s5_c_re, s5_c_im, s5_d, s5_w_glu, s5_b_glu, da_q_norm, da_k_norm, da_lam, da_subln, gq_q_norm, gq_k_norm, ml_cq_norm, ml_ckv_norm, ml_w_uq, ml_w_ukv, ml_q_norm, ml_k_norm, w_branch, w_out, norm_ffn, w_ffn_in, w_ffn_out):
    b, n_lat, d = x.shape
    n_ctx = ctx.shape[1]
    depth = w_mod.shape[0]
    assert n_ctx == ROW_TILE and n_lat % ROW_TILE == 0 and b < 8 and d == 1024

    cc = jnp.zeros((8, d), F32).at[:b].set(c).at[b].set(c_ctx)
    mod_rows = _modulation(cc, w_mod, b_mod).reshape(depth * 8, 1, 6 * d)
    rope = _rope_table(n_lat, n_ctx)
    xc = jnp.concatenate([x, ctx], axis=1)

    de64, de128 = _deinterleave(64), _deinterleave(128)
    cols = _inproj_columns(d)
    uq_cols = np.concatenate([hd * 192 + np.arange(128) for hd in range(ML_HEADS)]
                             + [hd * 192 + 128 + de64 for hd in range(ML_HEADS)])
    ukv_cols = np.concatenate([hd * 256 + np.arange(128) for hd in range(ML_HEADS)]
                              + [hd * 256 + 128 + np.arange(128) for hd in range(ML_HEADS)])
    log2e = math.log2(math.e)
    da_scale, gq_scale = DA_DK ** -0.5 * log2e, GQ_DH ** -0.5 * log2e
    ml_scale = (ML_NOPE + ML_ROPE) ** -0.5 * log2e

    for i in range(depth):
        last = i == depth - 1
        lam_init = 0.8 - 0.6 * math.exp(-0.3 * i)
        w = w_in[i][:, cols].astype(BF16)
        wuq = ml_w_uq[i][:, uq_cols].astype(BF16)
        wukv = ml_w_ukv[i][:, ukv_cols].astype(BF16)
        gains = jnp.concatenate([
            jnp.tile(da_q_norm[i][de64], 8) * da_scale, jnp.tile(da_k_norm[i][de64], 8),
            jnp.tile(gq_q_norm[i][de128], 4) * gq_scale, jnp.tile(gq_k_norm[i][de128], 2),
            ml_cq_norm[i], ml_ckv_norm[i],
            jnp.tile(ml_q_norm[i][:ML_NOPE], 4) * ml_scale,
            jnp.tile(ml_q_norm[i][ML_NOPE:][de64], 4) * ml_scale,
            jnp.tile(ml_k_norm[i][:ML_NOPE], 4), jnp.tile(ml_k_norm[i][ML_NOPE:][de64], 2),
        ]).astype(F32)[None, :]

        (u, daq, dak, dav, gqq, gqk, gqv, mlq, mlk, mlv, gates) = _inproj(
            xc, mod_rows, i, b, n_lat, norm_mix[i][None, :], gains, rope, w, wuq, wukv)

        s5_ops = _s5_prep(s5_lam_re[i], s5_lam_im[i], s5_log_step[i], s5_b_re[i], s5_b_im[i],
                          s5_c_re[i], s5_c_im[i], s5_d[i])
        y = _s5(u, n_lat, s5_ops)

        n_rows = n_lat if last else n_lat + n_ctx
        da = _attention("da", daq, dak, dav, n_lat, not last,
                        extra=(da_lam[i].astype(F32), da_subln[i].astype(F32)[None, :]),
                        lam_init=lam_init)
        gq = _attention("gq", gqq, gqk, gqv, n_lat, not last)
        ml = _attention("ml", mlq, mlk, mlv, n_lat, not last)

        x1 = _merge(xc, mod_rows, i, n_lat, n_rows, y, (da, gq, ml), gates,
                    s5_w_glu[i].astype(BF16), s5_b_glu[i].astype(F32)[None, :],
                    w_branch[i].astype(BF16), w_out[i].astype(BF16))
        xc = _ffn(x1, mod_rows, i, n_lat, norm_ffn[i][None, :],
                  w_ffn_in[i].astype(BF16), w_ffn_out[i].astype(BF16))
    return xc
```

```python
import functools
import math

import numpy as np
import jax
import jax.numpy as jnp
from jax import lax
from jax.experimental import pallas as pl
from jax.experimental.pallas import tpu as pltpu

GRID_W = 64
ROPE_THETA = 10000.0
NORM_EPS = 1e-6

S5_GROUP = 16
S5_STATE = 64
S5_CHUNK = 8

DA_HEADS = 4
DA_DK = 64
DA_DV = 128
GQ_HEADS = 4
GQ_KV = 2
GQ_DH = 128
ML_HEADS = 4
ML_QRANK = 256
ML_KVRANK = 128
ML_NOPE = 128
ML_ROPE = 64
ML_DV = 128
N_BRANCH = 4
BRANCH_W = 512

LANES = 128
S5_TILE_GROUPS = LANES // S5_GROUP
ROW_TILE = 256
VMEM_LIMIT = 56 * 1024 * 1024

F32 = jnp.float32
BF16 = jnp.bfloat16


def _deinterleave(n):
    return np.concatenate([np.arange(0, n, 2), np.arange(1, n, 2)])


def _cparams(sem):
    return pltpu.CompilerParams(dimension_semantics=sem, vmem_limit_bytes=VMEM_LIMIT)


def _const_spec(shape):
    nd = len(shape)
    return pl.BlockSpec(shape, lambda *_: (0,) * nd)


def _mod_kernel(cc_ref, w_ref, b_ref, o_ref):
    a = cc_ref[...]
    a = a * jax.nn.sigmoid(a)
    o_ref[0] = jnp.dot(a, w_ref[0], preferred_element_type=F32,
                       precision=lax.Precision.HIGHEST) + b_ref[0]


def _modulation(cc, w_mod, b_mod):
    depth, d, n = w_mod.shape
    tn = n // 4
    return pl.pallas_call(
        _mod_kernel,
        grid=(depth, n // tn),
        in_specs=[pl.BlockSpec((8, d), lambda i, j: (0, 0)),
                  pl.BlockSpec((1, d, tn), lambda i, j: (i, 0, j)),
                  pl.BlockSpec((1, 1, tn), lambda i, j: (i, 0, j))],
        out_specs=pl.BlockSpec((1, 8, tn), lambda i, j: (i, 0, j)),
        out_shape=jax.ShapeDtypeStruct((depth, 8, n), F32),
        compiler_params=_cparams(("arbitrary", "arbitrary")),
        name="modulation",
    )(cc, w_mod, b_mod.reshape(depth, 1, n))


def _seg_rms(z, seg):
    width = z.shape[-1]
    if seg == 2 * LANES:
        outs = []
        for g in range(width // seg):
            zg = z[:, g * seg:(g + 1) * seg]
            ms = jnp.sum(zg * zg, axis=-1, keepdims=True) * (1.0 / seg)
            outs.append(zg * lax.rsqrt(ms + NORM_EPS))
        return outs[0] if len(outs) == 1 else jnp.concatenate(outs, axis=-1)
    outs = []
    for g in range(width // LANES):
        zg = z[:, g * LANES:(g + 1) * LANES]
        zz = zg * zg
        if seg == LANES:
            ms = jnp.sum(zz, axis=-1, keepdims=True) * (1.0 / seg)
        else:
            lo = lax.broadcasted_iota(jnp.int32, zz.shape, 1) < seg
            s_lo = jnp.sum(jnp.where(lo, zz, 0.0), axis=-1, keepdims=True)
            s_hi = jnp.sum(jnp.where(lo, 0.0, zz), axis=-1, keepdims=True)
            ms = jnp.where(lo, s_lo, s_hi) * (1.0 / seg)
        outs.append(zg * lax.rsqrt(ms + NORM_EPS))
    return outs[0] if len(outs) == 1 else jnp.concatenate(outs, axis=-1)


def _rope(x, cos, sin, unit):
    outs = []
    for g in range(x.shape[-1] // LANES):
        xg = x[:, g * LANES:(g + 1) * LANES]
        if unit == LANES:
            rot = pltpu.roll(xg, LANES // 2, 1)
        else:
            lane = lax.broadcasted_iota(jnp.int32, xg.shape, 1)
            rot = jnp.where((lane & (unit // 2)) == 0,
                            pltpu.roll(xg, LANES - unit // 2, 1), pltpu.roll(xg, unit // 2, 1))
        outs.append(xg * cos + rot * sin)
    return outs[0] if len(outs) == 1 else jnp.concatenate(outs, axis=-1)


_C_S5 = 0
_C_DAQ = 512
_C_DAK = 1024
_C_DAV = 1536
_C_GQQ = 2048
_C_GQK = 2560
_C_GQV = 2816
_C_ML = 3072
_C_GATE = 3584
_C_END = 3584 + 4096

_G_DAQ, _G_DAK, _G_GQQ, _G_GQK = 0, 512, 1024, 1536
_G_CQ, _G_CKV, _G_QN, _G_QR, _G_KN, _G_KR, _G_END = 1792, 2048, 2176, 2688, 2944, 3456, 3584


def _inproj_kernel(x_ref, mod_ref, nrm_ref, gains_ref, rope_ref, w_ref, wuq_ref, wukv_ref,
                   u_ref, daq_ref, dak_ref, dav_ref, gqq_ref, gqk_ref, gqv_ref,
                   mlq_ref, mlk_ref, mlv_ref, gate_ref, u_sc):
    d = x_ref.shape[-1]
    x = x_ref[0]
    ms = jnp.mean(x * x, axis=-1, keepdims=True)
    mod = mod_ref[0]
    shift, scale = mod[:, 0:d], mod[:, d:2 * d]
    h = (x * lax.rsqrt(ms + NORM_EPS) * nrm_ref[...]) * (1.0 + scale) + shift
    hb = h.astype(BF16)

    def proj(c0, c1):
        return jnp.dot(hb, w_ref[:, c0:c1], preferred_element_type=F32)

    def gain(g0, g1):
        return gains_ref[:, g0:g1]

    cos64, sin64 = rope_ref[:, 0:128], rope_ref[:, 128:256]
    cos128, sin128 = rope_ref[:, 256:384], rope_ref[:, 384:512]

    zu = proj(_C_S5, _C_DAQ)
    per_tile = x.shape[0] // S5_CHUNK
    for j in range(u_sc.shape[0]):
        u_sc[j] = zu[:, j * LANES:(j + 1) * LANES]
        for s in range(S5_CHUNK):
            u_ref[0, s * per_tile:(s + 1) * per_tile, j * LANES:(j + 1) * LANES] = (
                u_sc[j, pl.ds(s, per_tile, stride=S5_CHUNK), :].astype(BF16))

    q = _seg_rms(proj(_C_DAQ, _C_DAK), DA_DK) * gain(_G_DAQ, _G_DAK)
    daq_ref[0] = _rope(q, cos64, sin64, DA_DK).astype(BF16)
    k = _seg_rms(proj(_C_DAK, _C_DAV), DA_DK) * gain(_G_DAK, _G_GQQ)
    dak_ref[0] = _rope(k, cos64, sin64, DA_DK).astype(BF16)
    dav_ref[0] = proj(_C_DAV, _C_GQQ).astype(BF16)

    q = _seg_rms(proj(_C_GQQ, _C_GQK), GQ_DH) * gain(_G_GQQ, _G_GQK)
    gqq_ref[0] = _rope(q, cos128, sin128, GQ_DH).astype(BF16)
    k = _seg_rms(proj(_C_GQK, _C_GQV), GQ_DH) * gain(_G_GQK, _G_CQ)
    gqk_ref[0] = _rope(k, cos128, sin128, GQ_DH).astype(BF16)
    gqv_ref[0] = proj(_C_GQV, _C_ML).astype(BF16)

    zc = proj(_C_ML, _C_GATE)
    cq = (_seg_rms(zc[:, 0:256], 256) * gain(_G_CQ, _G_CKV)).astype(BF16)
    ckv = (_seg_rms(zc[:, 256:384], 128) * gain(_G_CKV, _G_QN)).astype(BF16)
    qq = jnp.dot(cq, wuq_ref[...], preferred_element_type=F32)
    kv = jnp.dot(ckv, wukv_ref[...], preferred_element_type=F32)
    qn = (_seg_rms(qq[:, 0:512], ML_NOPE) * gain(_G_QN, _G_QR)).astype(BF16)
    qr = _seg_rms(qq[:, 512:768], ML_ROPE) * gain(_G_QR, _G_KN)
    qr = _rope(qr, cos64, sin64, ML_ROPE)
    kn = (_seg_rms(kv[:, 0:512], ML_NOPE) * gain(_G_KN, _G_KR)).astype(BF16)
    kr = _seg_rms(zc[:, 384:512], ML_ROPE) * gain(_G_KR, _G_END)
    kr = _rope(kr, cos64, sin64, ML_ROPE).astype(BF16)
    lo = lax.broadcasted_iota(jnp.int32, (x.shape[0], LANES), 1) < ML_ROPE
    for hd in range(ML_HEADS):
        pair = qr[:, (hd // 2) * LANES:(hd // 2 + 1) * LANES]
        keep = lo if hd % 2 == 0 else jnp.logical_not(lo)
        mlq_ref[0, :, hd * 256:hd * 256 + 128] = qn[:, hd * 128:(hd + 1) * 128]
        mlq_ref[0, :, hd * 256 + 128:(hd + 1) * 256] = jnp.where(keep, pair, 0.0).astype(BF16)
        mlk_ref[0, :, hd * 256:hd * 256 + 128] = kn[:, hd * 128:(hd + 1) * 128]
        mlk_ref[0, :, hd * 256 + 128:(hd + 1) * 256] = kr
    mlv_ref[0] = kv[:, 512:1024].astype(BF16)

    for j in range(N_BRANCH):
        c0 = _C_GATE + j * d
        gate_ref[0, :, j * d:(j + 1) * d] = jax.nn.sigmoid(proj(c0, c0 + d)).astype(BF16)


def _inproj(xc, mod_rows, layer, n_batch, n_lat, nrm, gains, rope, w, wuq, wukv):
    b, lt, d = xc.shape
    tm = ROW_TILE
    nt = lt // tm
    lat_tiles = n_lat // tm

    def mod_idx(i, t):
        return (layer * 8 + jnp.where(t >= lat_tiles, n_batch, i), 0, 0)

    widths = [512, 512, 512, 512, 512, 256, 256, 1024, 1024, 512, N_BRANCH * d]
    return pl.pallas_call(
        _inproj_kernel,
        grid=(b, nt),
        in_specs=[pl.BlockSpec((1, tm, d), lambda i, t: (i, t, 0)),
                  pl.BlockSpec((1, 1, mod_rows.shape[-1]), mod_idx),
                  _const_spec(nrm.shape), _const_spec(gains.shape),
                  pl.BlockSpec((tm, 512), lambda i, t: (t, 0)),
                  _const_spec(w.shape), _const_spec(wuq.shape), _const_spec(wukv.shape)],
        out_specs=[pl.BlockSpec((1, tm, wd), lambda i, t: (i, t, 0)) for wd in widths],
        out_shape=[jax.ShapeDtypeStruct((b, lt, wd), BF16) for wd in widths],
        scratch_shapes=[pltpu.VMEM((widths[0] // LANES, tm, LANES), F32)],
        compiler_params=_cparams(("parallel", "parallel")),
        name="inproj",
    )(xc, mod_rows, nrm, gains, rope, w, wuq, wukv)


def _s5_kernel(u_ref, m_ref, ws_ref, wo_ref, are_ref, aim_ref, y_ref, s_sc, *, n_chunks, n_lat_chunks):
    t_len = u_ref.shape[2]
    quarter = s_sc.shape[1] // 4
    a = jnp.concatenate([u_ref[0, :, s].reshape(n_chunks, LANES) for s in range(t_len)], axis=-1)
    s_sc[...] = jnp.dot(a, ws_ref[0], preferred_element_type=F32)
    af_re, ab_re = are_ref[0][:, 0:quarter], are_ref[0][:, quarter:2 * quarter]
    af_im, ab_im = aim_ref[0][:, 0:quarter], aim_ref[0][:, quarter:2 * quarter]
    n_ctx_chunks = n_chunks - n_lat_chunks

    def step(i, carry):
        hf_re, hf_im, hb_re, hb_im = carry
        rf = pl.ds(jnp.where(i < n_ctx_chunks, n_lat_chunks + i, i - n_ctx_chunks), 1)
        rb = pl.ds(n_chunks - 1 - i, 1)
        sf_re, sf_im = s_sc[rf, 0:quarter], s_sc[rf, quarter:2 * quarter]
        sb_re, sb_im = s_sc[rb, 2 * quarter:3 * quarter], s_sc[rb, 3 * quarter:4 * quarter]
        s_sc[rf, 0:quarter] = hf_re
        s_sc[rf, quarter:2 * quarter] = hf_im
        s_sc[rb, 2 * quarter:3 * quarter] = hb_re
        s_sc[rb, 3 * quarter:4 * quarter] = hb_im
        return (af_re * hf_re - af_im * hf_im + sf_re, af_re * hf_im + af_im * hf_re + sf_im,
                ab_re * hb_re - ab_im * hb_im + sb_re, ab_re * hb_im + ab_im * hb_re + sb_im)

    zero = jnp.zeros((1, quarter), F32)
    lax.fori_loop(0, n_chunks, step, (zero, zero, zero, zero))

    y = (jnp.dot(a, m_ref[0], preferred_element_type=F32)
         + jnp.dot(s_sc[...].astype(BF16), wo_ref[0], preferred_element_type=F32))
    for t in range(t_len):
        y_ref[0, 0, :, t] = y[:, t * LANES:(t + 1) * LANES].reshape(y_ref.shape[2], y_ref.shape[4], LANES)


def _s5_prep(lam_re, lam_im, log_step, b_re, b_im, c_re, c_im, d_skip):
    hp = lax.Precision.HIGHEST
    t_len, n_g, n_p, n_c = S5_CHUNK, lam_re.shape[1], S5_STATE, S5_GROUP
    lam_re, lam_im = lam_re.astype(F32), lam_im.astype(F32)
    dt = jnp.exp(log_step.astype(F32))[..., None]
    mag = jnp.exp(lam_re * dt)
    ab_re, ab_im = mag * jnp.cos(lam_im * dt), mag * jnp.sin(lam_im * dt)
    den = lam_re * lam_re + lam_im * lam_im
    nr, ni = ab_re - 1.0, ab_im
    coef_re = (nr * lam_re + ni * lam_im) / den
    coef_im = (ni * lam_re - nr * lam_im) / den
    b_re, b_im = b_re.astype(F32), b_im.astype(F32)
    bb_re = coef_re[..., None] * b_re - coef_im[..., None] * b_im
    bb_im = coef_re[..., None] * b_im + coef_im[..., None] * b_re
    kk = jnp.arange(t_len + 1, dtype=F32)[:, None, None, None]
    pmag = jnp.exp(lam_re * dt * kk)
    pw_re, pw_im = pmag * jnp.cos(lam_im * dt * kk), pmag * jnp.sin(lam_im * dt * kk)
    c_re, c_im = c_re.astype(F32), c_im.astype(F32)
    cp_re = c_re[None] * pw_re[:, :, :, None, :] - c_im[None] * pw_im[:, :, :, None, :]
    cp_im = c_re[None] * pw_im[:, :, :, None, :] + c_im[None] * pw_re[:, :, :, None, :]
    kern = (jnp.einsum('tdgop,dgpc->tdgoc', cp_re, bb_re, precision=hp)
            - jnp.einsum('tdgop,dgpc->tdgoc', cp_im, bb_im, precision=hp))
    s_idx = np.arange(t_len)[:, None]
    t_idx = np.arange(t_len)[None, :]
    lag_f = np.clip(t_idx - s_idx, 0, t_len)
    lag_b = np.clip(s_idx - t_idx, 0, t_len)
    mf = kern[lag_f, 0] * jnp.asarray(t_idx >= s_idx, F32)[:, :, None, None, None]
    mb = kern[lag_b, 1] * jnp.asarray(s_idx >= t_idx, F32)[:, :, None, None, None]
    m_full = jnp.transpose(mf + mb, (2, 0, 4, 1, 3))
    eye_t = jnp.eye(t_len, dtype=F32)[None, :, None, :, None]
    eye_c = jnp.eye(n_c, dtype=F32)[None, None, :, None, :]
    m_full = m_full + eye_t * eye_c * d_skip.astype(F32).reshape(n_g, 1, n_c, 1, 1)

    def bpow(pw_r, pw_i, direction):
        re = pw_r[..., None] * bb_re[direction][None] - pw_i[..., None] * bb_im[direction][None]
        im = pw_r[..., None] * bb_im[direction][None] + pw_i[..., None] * bb_re[direction][None]
        return jnp.transpose(re, (1, 0, 3, 2)), jnp.transpose(im, (1, 0, 3, 2))

    rev = np.arange(t_len - 1, -1, -1)
    fwd = np.arange(t_len)
    f_re, f_im = bpow(pw_re[rev, 0], pw_im[rev, 0], 0)
    g_re, g_im = bpow(pw_re[fwd, 1], pw_im[fwd, 1], 1)

    def cpow(idx, direction):
        return (jnp.transpose(cp_re[idx, direction], (1, 3, 0, 2)),
                jnp.transpose(cp_im[idx, direction], (1, 3, 0, 2)))

    of_re, of_im = cpow(np.arange(1, t_len + 1), 0)
    ob_re, ob_im = cpow(np.arange(t_len, 0, -1), 1)

    gt = S5_TILE_GROUPS
    n_j = n_g // gt
    eye_g = jnp.eye(gt, dtype=F32)
    m_blk = jnp.einsum('jgscto,gh->jsgctho', m_full.reshape(n_j, gt, t_len, n_c, t_len, n_c), eye_g)
    m_blk = m_blk.reshape(n_j, t_len * LANES, t_len * LANES)
    ws_blk = jnp.stack([jnp.einsum('jgscp,gh->jsgchp', part.reshape(n_j, gt, t_len, n_c, n_p), eye_g)
                        for part in (f_re, f_im, g_re, g_im)], axis=4)
    ws_blk = ws_blk.reshape(n_j, t_len * LANES, 4 * gt * n_p)
    wo_blk = jnp.stack([jnp.einsum('jgpto,gh->jgptho', part.reshape(n_j, gt, n_p, t_len, n_c), eye_g)
                        for part in (of_re, -of_im, ob_re, -ob_im)], axis=1)
    wo_blk = wo_blk.reshape(n_j, 4 * gt * n_p, t_len * LANES)
    a_re = jnp.concatenate([pw_re[t_len, 0].reshape(n_j, 1, gt * n_p),
                            pw_re[t_len, 1].reshape(n_j, 1, gt * n_p)], axis=-1)
    a_im = jnp.concatenate([pw_im[t_len, 0].reshape(n_j, 1, gt * n_p),
                            pw_im[t_len, 1].reshape(n_j, 1, gt * n_p)], axis=-1)
    return m_blk.astype(BF16), ws_blk.astype(BF16), wo_blk.astype(BF16), a_re, a_im


def _s5(u, n_lat, ops):
    m_blk, ws_blk, wo_blk, a_re, a_im = ops
    b, lt, width = u.shape
    n_tiles = lt // ROW_TILE
    per_tile = ROW_TILE // S5_CHUNK
    n_chunks = n_tiles * per_tile
    n_state = ws_blk.shape[-1]
    blk = (1, n_tiles, S5_CHUNK, per_tile, LANES)
    tile = lambda j, i: (i, 0, 0, 0, j)
    wsel = lambda j, i: (j, 0, 0)
    y = pl.pallas_call(
        functools.partial(_s5_kernel, n_chunks=n_chunks, n_lat_chunks=n_lat // S5_CHUNK),
        grid=(width // LANES, b),
        in_specs=[pl.BlockSpec(blk, tile),
                  pl.BlockSpec((1,) + m_blk.shape[1:], wsel), pl.BlockSpec((1,) + ws_blk.shape[1:], wsel),
                  pl.BlockSpec((1,) + wo_blk.shape[1:], wsel),
                  pl.BlockSpec((1, 1, n_state // 2), wsel), pl.BlockSpec((1, 1, n_state // 2), wsel)],
        out_specs=pl.BlockSpec((1,) + blk, lambda j, i: (j, i, 0, 0, 0, 0)),
        out_shape=jax.ShapeDtypeStruct((width // LANES, b, n_tiles, S5_CHUNK, per_tile, LANES), F32),
        scratch_shapes=[pltpu.VMEM((n_chunks, n_state), F32)],
        compiler_params=_cparams(("parallel", "parallel")),
        name="s5_scan",
    )(u.reshape(b, n_tiles, S5_CHUNK, per_tile, width), m_blk, ws_blk, wo_blk, a_re, a_im)
    return y.reshape(width // LANES, b, lt, LANES)


ATTN_ROW_BLOCKS = 2
ATTN_KEY_TILE = 256


def _attn_kernel(*refs, mode, tq, tk, n_lat, n_extra, lam_init):
    q_ref, k_ref, v_ref = refs[:3]
    extra = refs[3:3 + n_extra]
    o_ref, acc_sc, sa_sc, sb_sc = refs[-4:]
    n_keys = k_ref.shape[1]
    n_ctx = n_keys - n_lat

    q = q_ref[0]
    if mode == "da":
        lo = lax.broadcasted_iota(jnp.int32, q.shape, 1) < DA_DK
        zero = jnp.zeros_like(q)
        qs = jnp.concatenate([jnp.where(lo, q, zero), jnp.where(lo, zero, q)], axis=0)
    elif mode == "gq":
        qs = jnp.concatenate([q[:, 0:GQ_DH], q[:, GQ_DH:2 * GQ_DH]], axis=0)
    else:
        qs = q
    n_rb = ATTN_ROW_BLOCKS
    rb = qs.shape[0] // n_rb
    q_blocks = [qs[r * rb:(r + 1) * rb] for r in range(n_rb)]

    def qk(r, start, size):
        kc = k_ref[0, pl.ds(start, size), :]
        s = lax.dot_general(q_blocks[r], kc, (((1,), (1,)), ((), ())), preferred_element_type=F32)
        return s, jnp.max(s, axis=-1, keepdims=True)

    def softmax_pv(r, s_buf, mx, m_old, start, size):
        m_new = jnp.maximum(m_old, mx)
        alpha = jnp.exp2(m_old - m_new)
        rows = pl.ds(r * rb, rb)
        kt = min(ATTN_KEY_TILE, size)
        pv, lsum = None, None
        for c0 in range(0, size, kt):
            p = jnp.exp2((s_buf[:, c0:c0 + kt] - m_new).astype(BF16))
            p32 = p.astype(F32)
            for i in range(kt // LANES):
                t = p32[:, i * LANES:(i + 1) * LANES]
                lsum = t if lsum is None else lsum + t
            t = jnp.dot(p, v_ref[0, pl.ds(start + c0, kt), :], preferred_element_type=F32)
            pv = t if pv is None else pv + t
        acc_sc[rows, 0:LANES] = alpha * acc_sc[rows, 0:LANES] + pv
        acc_sc[rows, LANES:2 * LANES] = alpha * acc_sc[rows, LANES:2 * LANES] + lsum
        return m_new

    def stage(cur_buf, nxt_buf, nxt_start, nxt_size, cur_start, cur_size, ms, mxs):
        new_ms, new_mxs = [], []
        for r in range(n_rb):
            rows = pl.ds(r * rb, rb)
            s_n, mx_n = qk(r, nxt_start, nxt_size)
            nxt_buf[rows, 0:nxt_size] = s_n
            new_mxs.append(mx_n)
            new_ms.append(softmax_pv(r, cur_buf.at[rows, :], mxs[r], ms[r], cur_start, cur_size))
        return new_ms, new_mxs

    acc_sc[...] = jnp.zeros_like(acc_sc)
    chunks = [(j * tk, tk) for j in range(n_lat // tk)] + [(n_lat, n_ctx)]
    bufs = (sa_sc, sb_sc)
    ms = [jnp.full((rb, 1), -1e30, F32) for _ in range(n_rb)]
    mxs = []
    for r in range(n_rb):
        s, mx = qk(r, *chunks[0])
        sa_sc[pl.ds(r * rb, rb), 0:chunks[0][1]] = s
        mxs.append(mx)
    for j in range(len(chunks) - 1):
        ms, mxs = stage(bufs[j % 2], bufs[(j + 1) % 2], *chunks[j + 1], *chunks[j], ms, mxs)
    last = len(chunks) - 1
    for r in range(n_rb):
        softmax_pv(r, bufs[last % 2].at[pl.ds(r * rb, rb), :], mxs[r], ms[r], *chunks[last])
    acc = acc_sc[...]
    o = acc[:, 0:LANES] / jnp.sum(acc[:, LANES:2 * LANES], axis=-1, keepdims=True)

    if mode == "da":
        lam_ref, sub_ref = extra
        lam = lam_ref[...]
        lam_full = (jnp.exp(jnp.sum(lam[0:1] * lam[1:2], axis=-1, keepdims=True))
                    - jnp.exp(jnp.sum(lam[2:3] * lam[3:4], axis=-1, keepdims=True)) + lam_init)
        dlt = o[0:tq] - lam_full * o[tq:2 * tq]
        ms_d = jnp.mean(dlt * dlt, axis=-1, keepdims=True)
        o_ref[0] = (dlt * lax.rsqrt(ms_d + NORM_EPS) * sub_ref[...] * (1.0 - lam_init)).astype(o_ref.dtype)
    elif mode == "gq":
        o_ref[0] = jnp.concatenate([o[0:tq], o[tq:2 * tq]], axis=-1).astype(o_ref.dtype)
    else:
        o_ref[0] = o.astype(o_ref.dtype)


def _pick_tk(n_lat):
    for tk in (1024, 512, 256):
        if n_lat % tk == 0:
            return tk
    raise ValueError("latent length must be a multiple of 256")


def _attention(mode, q, k, v, n_lat, with_ctx, extra=(), lam_init=0.0):
    b, lt, _ = q.shape
    n_ctx = lt - n_lat
    if mode == "da":
        heads, qw, kw, ow, g = DA_HEADS, 128, 128, 128, 2
    elif mode == "gq":
        heads, qw, kw, ow, g = GQ_KV, 256, 128, 256, 2
    else:
        heads, qw, kw, ow, g = ML_HEADS, 256, 256, 128, 1
    tk = _pick_tk(n_lat)

    def call(ctx_only):
        if ctx_only:
            tq, q0, n_tiles, kv_rows, kv_blk = n_ctx, n_lat // n_ctx, 1, n_ctx, n_lat // n_ctx
        else:
            tq = 2 * ROW_TILE // g if n_lat % (2 * ROW_TILE // g) == 0 else ROW_TILE
            q0, n_tiles, kv_rows, kv_blk = 0, n_lat // tq, lt, 0
        in_specs = [pl.BlockSpec((1, tq, qw), lambda i, h, t: (i, t + q0, h)),
                    pl.BlockSpec((1, kv_rows, kw), lambda i, h, t: (i, kv_blk, h)),
                    pl.BlockSpec((1, kv_rows, 128), lambda i, h, t: (i, kv_blk, h))]
        in_specs += [_const_spec(e.shape) for e in extra]
        return pl.pallas_call(
            functools.partial(_attn_kernel, mode=mode, tq=tq, tk=tk, n_lat=0 if ctx_only else n_lat,
                              n_extra=len(extra), lam_init=lam_init),
            grid=(b, heads, n_tiles),
            in_specs=in_specs,
            out_specs=pl.BlockSpec((1, tq, ow), lambda i, h, t: (i, t, h)),
            out_shape=jax.ShapeDtypeStruct((b, n_tiles * tq, BRANCH_W), BF16),
            scratch_shapes=[pltpu.VMEM((g * tq, 2 * LANES), F32),
                            pltpu.VMEM((g * tq, max(tk, n_ctx)), F32),
                            pltpu.VMEM((g * tq, max(tk, n_ctx)), F32)],
            compiler_params=_cparams(("parallel", "parallel", "arbitrary")),
            name="attn_" + mode + ("_ctx" if ctx_only else ""),
        )(q, k, v, *extra)

    return call(False), (call(True) if with_ctx else None)


def _gelu_tanh(x):
    return 0.5 * x * (1.0 + jnp.tanh(math.sqrt(2.0 / math.pi) * (x + 0.044715 * (x * x * x))))


def _merge_kernel(*refs, lat_tiles, with_ctx):
    x_ref, mod_ref, y_ref, gate_ref, wglu_ref, bglu_ref, wbr_ref, wout_ref = refs[:8]
    lat_refs = refs[8:11]
    ctx_refs = refs[11:14] if with_ctx else None
    o_ref, y_sc = refs[-2:]
    d = x_ref.shape[-1]
    per_tile = y_sc.shape[0] // S5_CHUNK
    for j in range(y_ref.shape[0]):
        for n in range(per_tile):
            y_sc[n * S5_CHUNK:(n + 1) * S5_CHUNK, j * LANES:(j + 1) * LANES] = (
                y_ref[j, 0, pl.ds(n, S5_CHUNK, stride=per_tile), :])
    g = _gelu_tanh(y_sc[...])
    glu = jax.nn.sigmoid(jnp.dot(g.astype(BF16), wglu_ref[...], preferred_element_type=F32)
                         + bglu_ref[...])
    branches = [(g * glu).astype(BF16)]
    for n in range(3):
        if with_ctx:
            is_ctx = pl.program_id(1) >= lat_tiles
            branches.append(jnp.where(is_ctx, ctx_refs[n][0], lat_refs[n][0]))
        else:
            branches.append(lat_refs[n][0])
    acc = None
    for n, br in enumerate(branches):
        term = gate_ref[0, :, n * d:(n + 1) * d].astype(F32) * jnp.dot(
            br, wbr_ref[n], preferred_element_type=F32)
        acc = term if acc is None else acc + term
    out = jnp.dot(acc.astype(BF16), wout_ref[...], preferred_element_type=F32)
    g1 = mod_ref[0][:, 2 * d:3 * d]
    o_ref[0] = x_ref[0] + g1 * out


def _ffn_kernel(x_ref, mod_ref, nrm_ref, win_ref, wout_ref, o_ref, *, ff_chunk):
    d = x_ref.shape[-1]
    d_ff = wout_ref.shape[0]
    x = x_ref[0]
    mod = mod_ref[0]
    shift, scale, g2 = mod[:, 3 * d:4 * d], mod[:, 4 * d:5 * d], mod[:, 5 * d:6 * d]
    ms = jnp.mean(x * x, axis=-1, keepdims=True)
    hb = ((x * lax.rsqrt(ms + NORM_EPS) * nrm_ref[...]) * (1.0 + scale) + shift).astype(BF16)
    out = None
    for c0 in range(0, d_ff, ff_chunk):
        gate = jnp.dot(hb, win_ref[:, c0:c0 + ff_chunk], preferred_element_type=F32)
        up = jnp.dot(hb, win_ref[:, d_ff + c0:d_ff + c0 + ff_chunk], preferred_element_type=F32)
        act = (gate * jax.nn.sigmoid(gate) * up).astype(BF16)
        part = jnp.dot(act, wout_ref[c0:c0 + ff_chunk, :], preferred_element_type=F32)
        out = part if out is None else out + part
    o_ref[0] = x + g2 * out


def _mod_index(layer, n_batch, lat_tiles):
    def idx(i, t):
        return (layer * 8 + jnp.where(t >= lat_tiles, n_batch, i), 0, 0)
    return idx


def _merge(xc, mod_rows, layer, n_lat, n_rows, y, attn, gates, wglu, bglu, wbr, wout):
    b, _, d = xc.shape
    tm = ROW_TILE
    lat_tiles = n_lat // tm
    with_ctx = n_rows > n_lat
    row = lambda i, t: (i, t, 0)
    lat_row = lambda i, t: (i, jnp.minimum(t, lat_tiles - 1), 0)
    in_specs = [pl.BlockSpec((1, tm, d), row),
                pl.BlockSpec((1, 1, mod_rows.shape[-1]), _mod_index(layer, b, lat_tiles)),
                pl.BlockSpec((BRANCH_W // LANES, 1, tm, LANES), lambda i, t: (0, i, t, 0)),
                pl.BlockSpec((1, tm, N_BRANCH * d), row),
                _const_spec(wglu.shape), _const_spec(bglu.shape),
                _const_spec(wbr.shape), _const_spec(wout.shape)]
    in_specs += [pl.BlockSpec((1, tm, BRANCH_W), lat_row)] * 3
    args = [xc, mod_rows, y, gates, wglu, bglu, wbr, wout] + [a[0] for a in attn]
    if with_ctx:
        in_specs += [pl.BlockSpec((1, tm, BRANCH_W), lambda i, t: (i, 0, 0))] * 3
        args += [a[1] for a in attn]
    return pl.pallas_call(
        functools.partial(_merge_kernel, lat_tiles=lat_tiles, with_ctx=with_ctx),
        grid=(b, n_rows // tm),
        in_specs=in_specs,
        out_specs=pl.BlockSpec((1, tm, d), row),
        out_shape=jax.ShapeDtypeStruct((b, n_rows, d), F32),
        scratch_shapes=[pltpu.VMEM((tm, BRANCH_W), F32)],
        compiler_params=_cparams(("parallel", "parallel")),
        name="merge",
    )(*args)


def _ffn(x1, mod_rows, layer, n_lat, nrm, win, wout):
    b, n_rows, d = x1.shape
    tm = ROW_TILE
    d_ff = wout.shape[0]
    ff_chunk = d_ff // 2 if (d_ff // 2) % LANES == 0 else d_ff
    row = lambda i, t: (i, t, 0)
    return pl.pallas_call(
        functools.partial(_ffn_kernel, ff_chunk=ff_chunk),
        grid=(b, n_rows // tm),
        in_specs=[pl.BlockSpec((1, tm, d), row),
                  pl.BlockSpec((1, 1, mod_rows.shape[-1]), _mod_index(layer, b, n_lat // tm)),
                  _const_spec(nrm.shape), _const_spec(win.shape), _const_spec(wout.shape)],
        out_specs=pl.BlockSpec((1, tm, d), row),
        out_shape=jax.ShapeDtypeStruct((b, n_rows, d), F32),
        compiler_params=_cparams(("parallel", "parallel")),
        name="ffn",
    )(x1, mod_rows, nrm, win, wout)


def _inproj_columns(d):
    de64, de128 = _deinterleave(64), _deinterleave(128)
    off_da = 512
    off_gq = off_da + 3 * DA_HEADS * DA_DV
    off_ml = off_gq + (GQ_HEADS + 2 * GQ_KV) * GQ_DH
    off_gate = off_ml + ML_QRANK + ML_KVRANK + ML_ROPE
    cols = [np.arange(512)]
    for part in range(2):
        for hd in range(DA_HEADS):
            for comp in range(2):
                cols.append(off_da + part * 512 + hd * 128 + comp * 64 + de64)
    cols.append(off_da + 1024 + np.arange(512))
    for hd in range(GQ_HEADS):
        cols.append(off_gq + hd * 128 + de128)
    for hd in range(GQ_KV):
        cols.append(off_gq + 512 + hd * 128 + de128)
    cols.append(off_gq + 768 + np.arange(256))
    cols.append(off_ml + np.arange(ML_QRANK + ML_KVRANK))
    cols.append(off_ml + ML_QRANK + ML_KVRANK + de64)
    cols.append(off_ml + ML_QRANK + ML_KVRANK + de64)
    cols.append(off_gate + np.arange(N_BRANCH * d))
    return np.concatenate(cols)


def _rope_table(n_lat, n_ctx):
    t = np.arange(n_lat)
    r = (t // GRID_W).astype(np.float32)
    col = (t % GRID_W).astype(np.float32)

    def angles(rot_dim):
        half = rot_dim // 2
        inv = jnp.asarray(ROPE_THETA, F32) ** (-jnp.arange(0, half, 2, dtype=F32) / half)
        return jnp.concatenate([jnp.asarray(r)[:, None] * inv, jnp.asarray(col)[:, None] * inv], axis=-1)

    a64, a128 = angles(64), angles(128)
    c64, s64 = jnp.cos(a64), jnp.sin(a64)
    c128, s128 = jnp.cos(a128), jnp.sin(a128)
    lat = jnp.concatenate([c64, c64, c64, c64, -s64, s64, -s64, s64, c128, c128, -s128, s128], axis=-1)
    ctx = jnp.concatenate([jnp.ones((n_ctx, 128), F32), jnp.zeros((n_ctx, 128), F32),
                           jnp.ones((n_ctx, 128), F32), jnp.zeros((n_ctx, 128), F32)], axis=-1)
    return jnp.concatenate([lat, ctx], axis=0)


def kernel(x, c, ctx, c_ctx, w_mod, b_mod, norm_mix, w_in, s5_lam_re, s5_lam_im, s5_log_step, s5_b_re, s5_b_im, s5_c_re, s5_c_im, s5_d, s5_w_glu, s5_b_glu, da_q_norm, da_k_norm, da_lam, da_subln, gq_q_norm, gq_k_norm, ml_cq_norm, ml_ckv_norm, ml_w_uq, ml_w_ukv, ml_q_norm, ml_k_norm, w_branch, w_out, norm_ffn, w_ffn_in, w_ffn_out):
    b, n_lat, d = x.shape
    n_ctx = ctx.shape[1]
    depth = w_mod.shape[0]
    assert n_ctx == ROW_TILE and n_lat % ROW_TILE == 0 and b < 8 and d == 1024

    cc = jnp.zeros((8, d), F32).at[:b].set(c).at[b].set(c_ctx)
    mod_rows = _modulation(cc, w_mod, b_mod).reshape(depth * 8, 1, 6 * d)
    rope = _rope_table(n_lat, n_ctx)
    xc = jnp.concatenate([x, ctx], axis=1)

    de64, de128 = _deinterleave(64), _deinterleave(128)
    cols = _inproj_columns(d)
    uq_cols = np.concatenate([hd * 192 + np.arange(128) for hd in range(ML_HEADS)]
                             + [hd * 192 + 128 + de64 for hd in range(ML_HEADS)])
    ukv_cols = np.concatenate([hd * 256 + np.arange(128) for hd in range(ML_HEADS)]
                              + [hd * 256 + 128 + np.arange(128) for hd in range(ML_HEADS)])
    log2e = math.log2(math.e)
    da_scale, gq_scale = DA_DK ** -0.5 * log2e, GQ_DH ** -0.5 * log2e
    ml_scale = (ML_NOPE + ML_ROPE) ** -0.5 * log2e

    for i in range(depth):
        last = i == depth - 1
        lam_init = 0.8 - 0.6 * math.exp(-0.3 * i)
        w = w_in[i][:, cols].astype(BF16)
        wuq = ml_w_uq[i][:, uq_cols].astype(BF16)
        wukv = ml_w_ukv[i][:, ukv_cols].astype(BF16)
        gains = jnp.concatenate([
            jnp.tile(da_q_norm[i][de64], 8) * da_scale, jnp.tile(da_k_norm[i][de64], 8),
            jnp.tile(gq_q_norm[i][de128], 4) * gq_scale, jnp.tile(gq_k_norm[i][de128], 2),
            ml_cq_norm[i], ml_ckv_norm[i],
            jnp.tile(ml_q_norm[i][:ML_NOPE], 4) * ml_scale,
            jnp.tile(ml_q_norm[i][ML_NOPE:][de64], 4) * ml_scale,
            jnp.tile(ml_k_norm[i][:ML_NOPE], 4), jnp.tile(ml_k_norm[i][ML_NOPE:][de64], 2),
        ]).astype(F32)[None, :]

        (u, daq, dak, dav, gqq, gqk, gqv, mlq, mlk, mlv, gates) = _inproj(
            xc, mod_rows, i, b, n_lat, norm_mix[i][None, :], gains, rope, w, wuq, wukv)

        s5_ops = _s5_prep(s5_lam_re[i], s5_lam_im[i], s5_log_step[i], s5_b_re[i], s5_b_im[i],
                          s5_c_re[i], s5_c_im[i], s5_d[i])
        y = _s5(u, n_lat, s5_ops)

        n_rows = n_lat if last else n_lat + n_ctx
        da = _attention("da", daq, dak, dav, n_lat, not last,
                        extra=(da_lam[i].astype(F32), da_subln[i].astype(F32)[None, :]),
                        lam_init=lam_init)
        gq = _attention("gq", gqq, gqk, gqv, n_lat, not last)
        ml = _attention("ml", mlq, mlk, mlv, n_lat, not last)

        x1 = _merge(xc, mod_rows, i, n_lat, n_rows, y, (da, gq, ml), gates,
                    s5_w_glu[i].astype(BF16), s5_b_glu[i].astype(F32)[None, :],
                    w_branch[i].astype(BF16), w_out[i].astype(BF16))
        xc = _ffn(x1, mod_rows, i, n_lat, norm_ffn[i][None, :],
                  w_ffn_in[i].astype(BF16), w_ffn_out[i].astype(BF16))
    return xc
```

```python
import functools
import math

import numpy as np
import jax
import jax.numpy as jnp
from jax import lax
from jax.experimental import pallas as pl
from jax.experimental.pallas import tpu as pltpu

GRID_W = 64
ROPE_THETA = 10000.0
NORM_EPS = 1e-6

S5_GROUP = 16
S5_STATE = 64
S5_CHUNK = 8

DA_HEADS = 4
DA_DK = 64
DA_DV = 128
GQ_HEADS = 4
GQ_KV = 2
GQ_DH = 128
ML_HEADS = 4
ML_QRANK = 256
ML_KVRANK = 128
ML_NOPE = 128
ML_ROPE = 64
ML_DV = 128
N_BRANCH = 4
BRANCH_W = 512

LANES = 128
S5_TILE_GROUPS = LANES // S5_GROUP
ROW_TILE = 256
VMEM_LIMIT = 56 * 1024 * 1024

F32 = jnp.float32
BF16 = jnp.bfloat16


def _deinterleave(n):
    return np.concatenate([np.arange(0, n, 2), np.arange(1, n, 2)])


def _cparams(sem):
    return pltpu.CompilerParams(dimension_semantics=sem, vmem_limit_bytes=VMEM_LIMIT)


def _const_spec(shape):
    nd = len(shape)
    return pl.BlockSpec(shape, lambda *_: (0,) * nd)


def _mod_kernel(cc_ref, w_ref, b_ref, o_ref):
    a = cc_ref[...]
    a = a * jax.nn.sigmoid(a)
    o_ref[0] = jnp.dot(a, w_ref[0], preferred_element_type=F32,
                       precision=lax.Precision.HIGHEST) + b_ref[0]


def _modulation(cc, w_mod, b_mod):
    depth, d, n = w_mod.shape
    tn = n // 4
    return pl.pallas_call(
        _mod_kernel,
        grid=(depth, n // tn),
        in_specs=[pl.BlockSpec((8, d), lambda i, j: (0, 0)),
                  pl.BlockSpec((1, d, tn), lambda i, j: (i, 0, j)),
                  pl.BlockSpec((1, 1, tn), lambda i, j: (i, 0, j))],
        out_specs=pl.BlockSpec((1, 8, tn), lambda i, j: (i, 0, j)),
        out_shape=jax.ShapeDtypeStruct((depth, 8, n), F32),
        compiler_params=_cparams(("arbitrary", "arbitrary")),
        name="modulation",
    )(cc, w_mod, b_mod.reshape(depth, 1, n))


def _seg_rms(z, seg):
    width = z.shape[-1]
    if seg == 2 * LANES:
        outs = []
        for g in range(width // seg):
            zg = z[:, g * seg:(g + 1) * seg]
            ms = jnp.sum(zg * zg, axis=-1, keepdims=True) * (1.0 / seg)
            outs.append(zg * lax.rsqrt(ms + NORM_EPS))
        return outs[0] if len(outs) == 1 else jnp.concatenate(outs, axis=-1)
    outs = []
    for g in range(width // LANES):
        zg = z[:, g * LANES:(g + 1) * LANES]
        zz = zg * zg
        if seg == LANES:
            ms = jnp.sum(zz, axis=-1, keepdims=True) * (1.0 / seg)
        else:
            lo = lax.broadcasted_iota(jnp.int32, zz.shape, 1) < seg
            s_lo = jnp.sum(jnp.where(lo, zz, 0.0), axis=-1, keepdims=True)
            s_hi = jnp.sum(jnp.where(lo, 0.0, zz), axis=-1, keepdims=True)
            ms = jnp.where(lo, s_lo, s_hi) * (1.0 / seg)
        outs.append(zg * lax.rsqrt(ms + NORM_EPS))
    return outs[0] if len(outs) == 1 else jnp.concatenate(outs, axis=-1)


def _rope(x, cos, sin, unit):
    outs = []
    for g in range(x.shape[-1] // LANES):
        xg = x[:, g * LANES:(g + 1) * LANES]
        if unit == LANES:
            rot = pltpu.roll(xg, LANES // 2, 1)
        else:
            lane = lax.broadcasted_iota(jnp.int32, xg.shape, 1)
            rot = jnp.where((lane & (unit // 2)) == 0,
                            pltpu.roll(xg, LANES - unit // 2, 1), pltpu.roll(xg, unit // 2, 1))
        outs.append(xg * cos + rot * sin)
    return outs[0] if len(outs) == 1 else jnp.concatenate(outs, axis=-1)


_C_S5 = 0
_C_DAQ = 512
_C_DAK = 1024
_C_DAV = 1536
_C_GQQ = 2048
_C_GQK = 2560
_C_GQV = 2816
_C_ML = 3072
_C_GATE = 3584
_C_END = 3584 + 4096

_G_DAQ, _G_DAK, _G_GQQ, _G_GQK = 0, 512, 1024, 1536
_G_CQ, _G_CKV, _G_QN, _G_QR, _G_KN, _G_KR, _G_END = 1792, 2048, 2176, 2688, 2944, 3456, 3584


def _inproj_kernel(x_ref, mod_ref, nrm_ref, gains_ref, rope_ref, w_ref, wuq_ref, wukv_ref,
                   u_ref, daq_ref, dak_ref, dav_ref, gqq_ref, gqk_ref, gqv_ref,
                   mlq_ref, mlk_ref, mlv_ref, gate_ref, u_sc):
    d = x_ref.shape[-1]
    x = x_ref[0]
    ms = jnp.mean(x * x, axis=-1, keepdims=True)
    mod = mod_ref[0]
    shift, scale = mod[:, 0:d], mod[:, d:2 * d]
    h = (x * lax.rsqrt(ms + NORM_EPS) * nrm_ref[...]) * (1.0 + scale) + shift
    hb = h.astype(BF16)

    def proj(c0, c1):
        return jnp.dot(hb, w_ref[:, c0:c1], preferred_element_type=F32)

    def gain(g0, g1):
        return gains_ref[:, g0:g1]

    cos64, sin64 = rope_ref[:, 0:128], rope_ref[:, 128:256]
    cos128, sin128 = rope_ref[:, 256:384], rope_ref[:, 384:512]

    zu = proj(_C_S5, _C_DAQ)
    per_tile = x.shape[0] // S5_CHUNK
    for j in range(u_sc.shape[0]):
        u_sc[j] = zu[:, j * LANES:(j + 1) * LANES]
        for s in range(S5_CHUNK):
            u_ref[0, s * per_tile:(s + 1) * per_tile, j * LANES:(j + 1) * LANES] = (
                u_sc[j, pl.ds(s, per_tile, stride=S5_CHUNK), :].astype(BF16))

    q = _seg_rms(proj(_C_DAQ, _C_DAK), DA_DK) * gain(_G_DAQ, _G_DAK)
    daq_ref[0] = _rope(q, cos64, sin64, DA_DK).astype(BF16)
    k = _seg_rms(proj(_C_DAK, _C_DAV), DA_DK) * gain(_G_DAK, _G_GQQ)
    dak_ref[0] = _rope(k, cos64, sin64, DA_DK).astype(BF16)
    dav_ref[0] = proj(_C_DAV, _C_GQQ).astype(BF16)

    q = _seg_rms(proj(_C_GQQ, _C_GQK), GQ_DH) * gain(_G_GQQ, _G_GQK)
    gqq_ref[0] = _rope(q, cos128, sin128, GQ_DH).astype(BF16)
    k = _seg_rms(proj(_C_GQK, _C_GQV), GQ_DH) * gain(_G_GQK, _G_CQ)
    gqk_ref[0] = _rope(k, cos128, sin128, GQ_DH).astype(BF16)
    gqv_ref[0] = proj(_C_GQV, _C_ML).astype(BF16)

    zc = proj(_C_ML, _C_GATE)
    cq = (_seg_rms(zc[:, 0:256], 256) * gain(_G_CQ, _G_CKV)).astype(BF16)
    ckv = (_seg_rms(zc[:, 256:384], 128) * gain(_G_CKV, _G_QN)).astype(BF16)
    qq = jnp.dot(cq, wuq_ref[...], preferred_element_type=F32)
    kv = jnp.dot(ckv, wukv_ref[...], preferred_element_type=F32)
    qn = (_seg_rms(qq[:, 0:512], ML_NOPE) * gain(_G_QN, _G_QR)).astype(BF16)
    qr = _seg_rms(qq[:, 512:768], ML_ROPE) * gain(_G_QR, _G_KN)
    qr = _rope(qr, cos64, sin64, ML_ROPE)
    kn = (_seg_rms(kv[:, 0:512], ML_NOPE) * gain(_G_KN, _G_KR)).astype(BF16)
    kr = _seg_rms(zc[:, 384:512], ML_ROPE) * gain(_G_KR, _G_END)
    kr = _rope(kr, cos64, sin64, ML_ROPE).astype(BF16)
    lo = lax.broadcasted_iota(jnp.int32, (x.shape[0], LANES), 1) < ML_ROPE
    for hd in range(ML_HEADS):
        pair = qr[:, (hd // 2) * LANES:(hd // 2 + 1) * LANES]
        keep = lo if hd % 2 == 0 else jnp.logical_not(lo)
        mlq_ref[0, :, hd * 256:hd * 256 + 128] = qn[:, hd * 128:(hd + 1) * 128]
        mlq_ref[0, :, hd * 256 + 128:(hd + 1) * 256] = jnp.where(keep, pair, 0.0).astype(BF16)
        mlk_ref[0, :, hd * 256:hd * 256 + 128] = kn[:, hd * 128:(hd + 1) * 128]
        mlk_ref[0, :, hd * 256 + 128:(hd + 1) * 256] = kr
    mlv_ref[0] = kv[:, 512:1024].astype(BF16)

    for j in range(N_BRANCH):
        c0 = _C_GATE + j * d
        gate_ref[0, :, j * d:(j + 1) * d] = jax.nn.sigmoid(proj(c0, c0 + d)).astype(BF16)


def _inproj(xc, mod_rows, layer, n_batch, n_lat, nrm, gains, rope, w, wuq, wukv):
    b, lt, d = xc.shape
    tm = ROW_TILE
    nt = lt // tm
    lat_tiles = n_lat // tm

    def mod_idx(i, t):
        return (layer * 8 + jnp.where(t >= lat_tiles, n_batch, i), 0, 0)

    widths = [512, 512, 512, 512, 512, 256, 256, 1024, 1024, 512, N_BRANCH * d]
    return pl.pallas_call(
        _inproj_kernel,
        grid=(b, nt),
        in_specs=[pl.BlockSpec((1, tm, d), lambda i, t: (i, t, 0)),
                  pl.BlockSpec((1, 1, mod_rows.shape[-1]), mod_idx),
                  _const_spec(nrm.shape), _const_spec(gains.shape),
                  pl.BlockSpec((tm, 512), lambda i, t: (t, 0)),
                  _const_spec(w.shape), _const_spec(wuq.shape), _const_spec(wukv.shape)],
        out_specs=[pl.BlockSpec((1, tm, wd), lambda i, t: (i, t, 0)) for wd in widths],
        out_shape=[jax.ShapeDtypeStruct((b, lt, wd), BF16) for wd in widths],
        scratch_shapes=[pltpu.VMEM((widths[0] // LANES, tm, LANES), F32)],
        compiler_params=_cparams(("parallel", "parallel")),
        name="inproj",
    )(xc, mod_rows, nrm, gains, rope, w, wuq, wukv)


def _shift(n):
    assert n & (n - 1) == 0
    return n.bit_length() - 1


def _s5_expand(mg_ref, wsg_ref, wog_ref, m_sc, ws_sc, wo_sc):
    n_rows, cw = mg_ref.shape[1], mg_ref.shape[2]
    n_c, gt = S5_GROUP, n_rows // cw
    n_sp = wsg_ref.shape[2]
    n_p = n_sp // 4
    n_state = gt * n_sp

    def iota2(shape):
        return lax.broadcasted_iota(jnp.int32, shape, 0), lax.broadcasted_iota(jnp.int32, shape, 1)

    def div(i, n):
        return lax.shift_right_logical(i, _shift(n))

    def mod(i, n):
        return i & (n - 1)

    def onehot(cond):
        return jnp.where(cond, 1.0, 0.0).astype(BF16)

    def lane_group(i):
        return mod(div(i, n_c), gt)

    def state_group(i):
        return mod(div(i, n_p), gt)

    r, q = iota2((n_rows, n_rows))
    regroup = onehot(q == lane_group(r) * cw + div(r, gt * n_c) * n_c + mod(r, n_c))
    same_lane_group = lane_group(r) == lane_group(q)

    k, q = iota2((cw, n_rows))
    spread = onehot((div(q, gt * n_c) == div(k, n_c)) & (mod(q, n_c) == mod(k, n_c)))
    t_m = jnp.dot(regroup, mg_ref[0], preferred_element_type=F32).astype(BF16)
    m_sc[...] = jnp.where(same_lane_group, jnp.dot(t_m, spread, preferred_element_type=F32), 0.0).astype(BF16)

    k, q = iota2((n_sp, n_state))
    spread_s = onehot((div(q, gt * n_p) == div(k, n_p)) & (mod(q, n_p) == mod(k, n_p)))
    r, q = iota2((n_rows, n_state))
    t_s = jnp.dot(regroup, wsg_ref[0], preferred_element_type=F32).astype(BF16)
    ws_sc[...] = jnp.where(lane_group(r) == state_group(q),
                           jnp.dot(t_s, spread_s, preferred_element_type=F32), 0.0).astype(BF16)

    r, k = iota2((n_state, n_sp))
    gather_s = onehot((div(r, gt * n_p) == div(k, n_p)) & (mod(r, n_p) == mod(k, n_p)))
    t_o = lax.dot_general(wog_ref[0], regroup, (((1,), (1,)), ((), ())),
                          preferred_element_type=F32).astype(BF16)
    r, q = iota2((n_state, n_rows))
    wo_sc[...] = jnp.where(state_group(r) == lane_group(q),
                           jnp.dot(gather_s, t_o, preferred_element_type=F32), 0.0).astype(BF16)


def _s5_kernel(u_ref, mg_ref, wsg_ref, wog_ref, are_ref, aim_ref, y_ref, s_sc, m_sc, ws_sc, wo_sc,
               *, n_chunks, n_lat_chunks):
    @pl.when(pl.program_id(1) == 0)
    def _():
        _s5_expand(mg_ref, wsg_ref, wog_ref, m_sc, ws_sc, wo_sc)

    t_len = u_ref.shape[2]
    quarter = s_sc.shape[1] // 4
    a = jnp.concatenate([u_ref[0, :, s].reshape(n_chunks, LANES) for s in range(t_len)], axis=-1)
    s_sc[...] = jnp.dot(a, ws_sc[...], preferred_element_type=F32)
    af_re, ab_re = are_ref[0][:, 0:quarter], are_ref[0][:, quarter:2 * quarter]
    af_im, ab_im = aim_ref[0][:, 0:quarter], aim_ref[0][:, quarter:2 * quarter]
    n_ctx_chunks = n_chunks - n_lat_chunks

    def step(i, carry):
        hf_re, hf_im, hb_re, hb_im = carry
        rf = pl.ds(jnp.where(i < n_ctx_chunks, n_lat_chunks + i, i - n_ctx_chunks), 1)
        rb = pl.ds(n_chunks - 1 - i, 1)
        sf_re, sf_im = s_sc[rf, 0:quarter], s_sc[rf, quarter:2 * quarter]
        sb_re, sb_im = s_sc[rb, 2 * quarter:3 * quarter], s_sc[rb, 3 * quarter:4 * quarter]
        s_sc[rf, 0:quarter] = hf_re
        s_sc[rf, quarter:2 * quarter] = hf_im
        s_sc[rb, 2 * quarter:3 * quarter] = hb_re
        s_sc[rb, 3 * quarter:4 * quarter] = hb_im
        return (af_re * hf_re - af_im * hf_im + sf_re, af_re * hf_im + af_im * hf_re + sf_im,
                ab_re * hb_re - ab_im * hb_im + sb_re, ab_re * hb_im + ab_im * hb_re + sb_im)

    zero = jnp.zeros((1, quarter), F32)
    lax.fori_loop(0, n_chunks, step, (zero, zero, zero, zero))

    y = (jnp.dot(a, m_sc[...], preferred_element_type=F32)
         + jnp.dot(s_sc[...].astype(BF16), wo_sc[...], preferred_element_type=F32))
    for t in range(t_len):
        y_ref[0, 0, :, t] = y[:, t * LANES:(t + 1) * LANES].reshape(y_ref.shape[2], y_ref.shape[4], LANES)


def _s5_prep(lam_re, lam_im, log_step, b_re, b_im, c_re, c_im, d_skip):
    hp = lax.Precision.HIGHEST
    t_len, n_g, n_p, n_c = S5_CHUNK, lam_re.shape[1], S5_STATE, S5_GROUP
    lam_re, lam_im = lam_re.astype(F32), lam_im.astype(F32)
    dt = jnp.exp(log_step.astype(F32))[..., None]
    mag = jnp.exp(lam_re * dt)
    ab_re, ab_im = mag * jnp.cos(lam_im * dt), mag * jnp.sin(lam_im * dt)
    den = lam_re * lam_re + lam_im * lam_im
    nr, ni = ab_re - 1.0, ab_im
    coef_re = (nr * lam_re + ni * lam_im) / den
    coef_im = (ni * lam_re - nr * lam_im) / den
    b_re, b_im = b_re.astype(F32), b_im.astype(F32)
    bb_re = coef_re[..., None] * b_re - coef_im[..., None] * b_im
    bb_im = coef_re[..., None] * b_im + coef_im[..., None] * b_re
    kk = jnp.arange(t_len + 1, dtype=F32)[:, None, None, None]
    pmag = jnp.exp(lam_re * dt * kk)
    pw_re, pw_im = pmag * jnp.cos(lam_im * dt * kk), pmag * jnp.sin(lam_im * dt * kk)
    c_re, c_im = c_re.astype(F32), c_im.astype(F32)
    cp_re = c_re[None] * pw_re[:, :, :, None, :] - c_im[None] * pw_im[:, :, :, None, :]
    cp_im = c_re[None] * pw_im[:, :, :, None, :] + c_im[None] * pw_re[:, :, :, None, :]
    kern = (jnp.einsum('tdgop,dgpc->tdgoc', cp_re, bb_re, precision=hp)
            - jnp.einsum('tdgop,dgpc->tdgoc', cp_im, bb_im, precision=hp))
    s_idx = np.arange(t_len)[:, None]
    t_idx = np.arange(t_len)[None, :]
    lag_f = np.clip(t_idx - s_idx, 0, t_len)
    lag_b = np.clip(s_idx - t_idx, 0, t_len)
    mf = kern[lag_f, 0] * jnp.asarray(t_idx >= s_idx, F32)[:, :, None, None, None]
    mb = kern[lag_b, 1] * jnp.asarray(s_idx >= t_idx, F32)[:, :, None, None, None]
    m_full = jnp.transpose(mf + mb, (2, 0, 4, 1, 3))
    eye_t = jnp.eye(t_len, dtype=F32)[None, :, None, :, None]
    eye_c = jnp.eye(n_c, dtype=F32)[None, None, :, None, :]
    m_full = m_full + eye_t * eye_c * d_skip.astype(F32).reshape(n_g, 1, n_c, 1, 1)

    def bpow(pw_r, pw_i, direction):
        re = pw_r[..., None] * bb_re[direction][None] - pw_i[..., None] * bb_im[direction][None]
        im = pw_r[..., None] * bb_im[direction][None] + pw_i[..., None] * bb_re[direction][None]
        return jnp.transpose(re, (1, 0, 3, 2)), jnp.transpose(im, (1, 0, 3, 2))

    rev = np.arange(t_len - 1, -1, -1)
    fwd = np.arange(t_len)
    f_re, f_im = bpow(pw_re[rev, 0], pw_im[rev, 0], 0)
    g_re, g_im = bpow(pw_re[fwd, 1], pw_im[fwd, 1], 1)

    def cpow(idx, direction):
        return (jnp.transpose(cp_re[idx, direction], (1, 3, 0, 2)),
                jnp.transpose(cp_im[idx, direction], (1, 3, 0, 2)))

    of_re, of_im = cpow(np.arange(1, t_len + 1), 0)
    ob_re, ob_im = cpow(np.arange(t_len, 0, -1), 1)

    gt = S5_TILE_GROUPS
    n_j = n_g // gt
    cw = t_len * n_c
    mg = m_full.reshape(n_j, gt * cw, cw)
    wsg = jnp.stack([f_re, f_im, g_re, g_im], axis=3).reshape(n_j, gt * cw, 4 * n_p)
    wog = jnp.stack([of_re, -of_im, ob_re, -ob_im], axis=1).reshape(n_j, gt, 4 * n_p, cw)
    wog = jnp.transpose(wog, (0, 2, 1, 3)).reshape(n_j, 4 * n_p, gt * cw)
    a_re = jnp.concatenate([pw_re[t_len, 0].reshape(n_j, 1, gt * n_p),
                            pw_re[t_len, 1].reshape(n_j, 1, gt * n_p)], axis=-1)
    a_im = jnp.concatenate([pw_im[t_len, 0].reshape(n_j, 1, gt * n_p),
                            pw_im[t_len, 1].reshape(n_j, 1, gt * n_p)], axis=-1)
    return mg.astype(BF16), wsg.astype(BF16), wog.astype(BF16), a_re, a_im


def _s5(u, n_lat, ops):
    mg, wsg, wog, a_re, a_im = ops
    b, lt, width = u.shape
    n_tiles = lt // ROW_TILE
    per_tile = ROW_TILE // S5_CHUNK
    n_chunks = n_tiles * per_tile
    n_rows = mg.shape[1]
    n_state = S5_TILE_GROUPS * wsg.shape[2]
    blk = (1, n_tiles, S5_CHUNK, per_tile, LANES)
    tile = lambda j, i: (i, 0, 0, 0, j)
    wsel = lambda j, i: (j, 0, 0)
    y = pl.pallas_call(
        functools.partial(_s5_kernel, n_chunks=n_chunks, n_lat_chunks=n_lat // S5_CHUNK),
        grid=(width // LANES, b),
        in_specs=[pl.BlockSpec(blk, tile),
                  pl.BlockSpec((1,) + mg.shape[1:], wsel), pl.BlockSpec((1,) + wsg.shape[1:], wsel),
                  pl.BlockSpec((1,) + wog.shape[1:], wsel),
                  pl.BlockSpec((1, 1, n_state // 2), wsel), pl.BlockSpec((1, 1, n_state // 2), wsel)],
        out_specs=pl.BlockSpec((1,) + blk, lambda j, i: (j, i, 0, 0, 0, 0)),
        out_shape=jax.ShapeDtypeStruct((width // LANES, b, n_tiles, S5_CHUNK, per_tile, LANES), F32),
        scratch_shapes=[pltpu.VMEM((n_chunks, n_state), F32), pltpu.VMEM((n_rows, n_rows), BF16),
                        pltpu.VMEM((n_rows, n_state), BF16), pltpu.VMEM((n_state, n_rows), BF16)],
        compiler_params=_cparams(("arbitrary", "arbitrary")),
        name="s5_scan",
    )(u.reshape(b, n_tiles, S5_CHUNK, per_tile, width), mg, wsg, wog, a_re, a_im)
    return y.reshape(width // LANES, b, lt, LANES)


ATTN_ROW_BLOCKS = 2
ATTN_KEY_TILE = 256


def _attn_kernel(*refs, mode, tq, tk, n_lat, n_extra, lam_init):
    q_ref, k_ref, v_ref = refs[:3]
    extra = refs[3:3 + n_extra]
    o_ref, acc_sc, sa_sc, sb_sc = refs[-4:]
    n_keys = k_ref.shape[1]
    n_ctx = n_keys - n_lat

    q = q_ref[0]
    if mode == "da":
        lo = lax.broadcasted_iota(jnp.int32, q.shape, 1) < DA_DK
        zero = jnp.zeros_like(q)
        qs = jnp.concatenate([jnp.where(lo, q, zero), jnp.where(lo, zero, q)], axis=0)
    elif mode == "gq":
        qs = jnp.concatenate([q[:, 0:GQ_DH], q[:, GQ_DH:2 * GQ_DH]], axis=0)
    else:
        qs = q
    n_rb = ATTN_ROW_BLOCKS
    rb = qs.shape[0] // n_rb
    q_blocks = [qs[r * rb:(r + 1) * rb] for r in range(n_rb)]

    def qk(r, start, size):
        kc = k_ref[0, pl.ds(start, size), :]
        s = lax.dot_general(q_blocks[r], kc, (((1,), (1,)), ((), ())), preferred_element_type=F32)
        return s, jnp.max(s, axis=-1, keepdims=True)

    def softmax_pv(r, s_buf, mx, m_old, start, size):
        m_new = jnp.maximum(m_old, mx)
        alpha = jnp.exp2(m_old - m_new)
        rows = pl.ds(r * rb, rb)
        kt = min(ATTN_KEY_TILE, size)
        pv, lsum = None, None
        for c0 in range(0, size, kt):
            p = jnp.exp2((s_buf[:, c0:c0 + kt] - m_new).astype(BF16))
            p32 = p.astype(F32)
            for i in range(kt // LANES):
                t = p32[:, i * LANES:(i + 1) * LANES]
                lsum = t if lsum is None else lsum + t
            t = jnp.dot(p, v_ref[0, pl.ds(start + c0, kt), :], preferred_element_type=F32)
            pv = t if pv is None else pv + t
        acc_sc[rows, 0:LANES] = alpha * acc_sc[rows, 0:LANES] + pv
        acc_sc[rows, LANES:2 * LANES] = alpha * acc_sc[rows, LANES:2 * LANES] + lsum
        return m_new

    def stage(cur_buf, nxt_buf, nxt_start, nxt_size, cur_start, cur_size, ms, mxs):
        new_ms, new_mxs = [], []
        for r in range(n_rb):
            rows = pl.ds(r * rb, rb)
            s_n, mx_n = qk(r, nxt_start, nxt_size)
            nxt_buf[rows, 0:nxt_size] = s_n
            new_mxs.append(mx_n)
            new_ms.append(softmax_pv(r, cur_buf.at[rows, :], mxs[r], ms[r], cur_start, cur_size))
        return new_ms, new_mxs

    acc_sc[...] = jnp.zeros_like(acc_sc)
    chunks = [(j * tk, tk) for j in range(n_lat // tk)] + [(n_lat, n_ctx)]
    bufs = (sa_sc, sb_sc)
    ms = [jnp.full((rb, 1), -1e30, F32) for _ in range(n_rb)]
    mxs = []
    for r in range(n_rb):
        s, mx = qk(r, *chunks[0])
        sa_sc[pl.ds(r * rb, rb), 0:chunks[0][1]] = s
        mxs.append(mx)
    for j in range(len(chunks) - 1):
        ms, mxs = stage(bufs[j % 2], bufs[(j + 1) % 2], *chunks[j + 1], *chunks[j], ms, mxs)
    last = len(chunks) - 1
    for r in range(n_rb):
        softmax_pv(r, bufs[last % 2].at[pl.ds(r * rb, rb), :], mxs[r], ms[r], *chunks[last])
    acc = acc_sc[...]
    o = acc[:, 0:LANES] / jnp.sum(acc[:, LANES:2 * LANES], axis=-1, keepdims=True)

    if mode == "da":
        lam_ref, sub_ref = extra
        lam = lam_ref[...]
        lam_full = (jnp.exp(jnp.sum(lam[0:1] * lam[1:2], axis=-1, keepdims=True))
                    - jnp.exp(jnp.sum(lam[2:3] * lam[3:4], axis=-1, keepdims=True)) + lam_init)
        dlt = o[0:tq] - lam_full * o[tq:2 * tq]
        ms_d = jnp.mean(dlt * dlt, axis=-1, keepdims=True)
        o_ref[0] = (dlt * lax.rsqrt(ms_d + NORM_EPS) * sub_ref[...] * (1.0 - lam_init)).astype(o_ref.dtype)
    elif mode == "gq":
        o_ref[0] = jnp.concatenate([o[0:tq], o[tq:2 * tq]], axis=-1).astype(o_ref.dtype)
    else:
        o_ref[0] = o.astype(o_ref.dtype)


def _pick_tk(n_lat):
    for tk in (1024, 512, 256):
        if n_lat % tk == 0:
            return tk
    raise ValueError("latent length must be a multiple of 256")


def _attention(mode, q, k, v, n_lat, with_ctx, extra=(), lam_init=0.0):
    b, lt, _ = q.shape
    n_ctx = lt - n_lat
    if mode == "da":
        heads, qw, kw, ow, g = DA_HEADS, 128, 128, 128, 2
    elif mode == "gq":
        heads, qw, kw, ow, g = GQ_KV, 256, 128, 256, 2
    else:
        heads, qw, kw, ow, g = ML_HEADS, 256, 256, 128, 1
    tk = _pick_tk(n_lat)

    def call(ctx_only):
        if ctx_only:
            tq, q0, n_tiles, kv_rows, kv_blk = n_ctx, n_lat // n_ctx, 1, n_ctx, n_lat // n_ctx
        else:
            tq = 2 * ROW_TILE // g if n_lat % (2 * ROW_TILE // g) == 0 else ROW_TILE
            q0, n_tiles, kv_rows, kv_blk = 0, n_lat // tq, lt, 0
        in_specs = [pl.BlockSpec((1, tq, qw), lambda i, h, t: (i, t + q0, h)),
                    pl.BlockSpec((1, kv_rows, kw), lambda i, h, t: (i, kv_blk, h)),
                    pl.BlockSpec((1, kv_rows, 128), lambda i, h, t: (i, kv_blk, h))]
        in_specs += [_const_spec(e.shape) for e in extra]
        return pl.pallas_call(
            functools.partial(_attn_kernel, mode=mode, tq=tq, tk=tk, n_lat=0 if ctx_only else n_lat,
                              n_extra=len(extra), lam_init=lam_init),
            grid=(b, heads, n_tiles),
            in_specs=in_specs,
            out_specs=pl.BlockSpec((1, tq, ow), lambda i, h, t: (i, t, h)),
            out_shape=jax.ShapeDtypeStruct((b, n_tiles * tq, BRANCH_W), BF16),
            scratch_shapes=[pltpu.VMEM((g * tq, 2 * LANES), F32),
                            pltpu.VMEM((g * tq, max(tk, n_ctx)), F32),
                            pltpu.VMEM((g * tq, max(tk, n_ctx)), F32)],
            compiler_params=_cparams(("parallel", "parallel", "arbitrary")),
            name="attn_" + mode + ("_ctx" if ctx_only else ""),
        )(q, k, v, *extra)

    return call(False), (call(True) if with_ctx else None)


def _gelu_tanh(x):
    return 0.5 * x * (1.0 + jnp.tanh(math.sqrt(2.0 / math.pi) * (x + 0.044715 * (x * x * x))))


def _merge_kernel(*refs, lat_tiles, with_ctx):
    x_ref, mod_ref, y_ref, gate_ref, wglu_ref, bglu_ref, wbr_ref, wout_ref = refs[:8]
    lat_refs = refs[8:11]
    ctx_refs = refs[11:14] if with_ctx else None
    o_ref, y_sc = refs[-2:]
    d = x_ref.shape[-1]
    per_tile = y_sc.shape[0] // S5_CHUNK
    for j in range(y_ref.shape[0]):
        for n in range(per_tile):
            y_sc[n * S5_CHUNK:(n + 1) * S5_CHUNK, j * LANES:(j + 1) * LANES] = (
                y_ref[j, 0, pl.ds(n, S5_CHUNK, stride=per_tile), :])
    g = _gelu_tanh(y_sc[...])
    glu = jax.nn.sigmoid(jnp.dot(g.astype(BF16), wglu_ref[...], preferred_element_type=F32)
                         + bglu_ref[...])
    branches = [(g * glu).astype(BF16)]
    for n in range(3):
        if with_ctx:
            is_ctx = pl.program_id(1) >= lat_tiles
            branches.append(jnp.where(is_ctx, ctx_refs[n][0], lat_refs[n][0]))
        else:
            branches.append(lat_refs[n][0])
    acc = None
    for n, br in enumerate(branches):
        term = gate_ref[0, :, n * d:(n + 1) * d].astype(F32) * jnp.dot(
            br, wbr_ref[n], preferred_element_type=F32)
        acc = term if acc is None else acc + term
    out = jnp.dot(acc.astype(BF16), wout_ref[...], preferred_element_type=F32)
    g1 = mod_ref[0][:, 2 * d:3 * d]
    o_ref[0] = x_ref[0] + g1 * out


def _ffn_kernel(x_ref, mod_ref, nrm_ref, win_ref, wout_ref, o_ref, *, ff_chunk):
    d = x_ref.shape[-1]
    d_ff = wout_ref.shape[0]
    x = x_ref[0]
    mod = mod_ref[0]
    shift, scale, g2 = mod[:, 3 * d:4 * d], mod[:, 4 * d:5 * d], mod[:, 5 * d:6 * d]
    ms = jnp.mean(x * x, axis=-1, keepdims=True)
    hb = ((x * lax.rsqrt(ms + NORM_EPS) * nrm_ref[...]) * (1.0 + scale) + shift).astype(BF16)
    out = None
    for c0 in range(0, d_ff, ff_chunk):
        gate = jnp.dot(hb, win_ref[:, c0:c0 + ff_chunk], preferred_element_type=F32)
        up = jnp.dot(hb, win_ref[:, d_ff + c0:d_ff + c0 + ff_chunk], preferred_element_type=F32)
        act = (gate * jax.nn.sigmoid(gate) * up).astype(BF16)
        part = jnp.dot(act, wout_ref[c0:c0 + ff_chunk, :], preferred_element_type=F32)
        out = part if out is None else out + part
    o_ref[0] = x + g2 * out


def _mod_index(layer, n_batch, lat_tiles):
    def idx(i, t):
        return (layer * 8 + jnp.where(t >= lat_tiles, n_batch, i), 0, 0)
    return idx


def _merge(xc, mod_rows, layer, n_lat, n_rows, y, attn, gates, wglu, bglu, wbr, wout):
    b, _, d = xc.shape
    tm = ROW_TILE
    lat_tiles = n_lat // tm
    with_ctx = n_rows > n_lat
    row = lambda i, t: (i, t, 0)
    lat_row = lambda i, t: (i, jnp.minimum(t, lat_tiles - 1), 0)
    in_specs = [pl.BlockSpec((1, tm, d), row),
                pl.BlockSpec((1, 1, mod_rows.shape[-1]), _mod_index(layer, b, lat_tiles)),
                pl.BlockSpec((BRANCH_W // LANES, 1, tm, LANES), lambda i, t: (0, i, t, 0)),
                pl.BlockSpec((1, tm, N_BRANCH * d), row),
                _const_spec(wglu.shape), _const_spec(bglu.shape),
                _const_spec(wbr.shape), _const_spec(wout.shape)]
    in_specs += [pl.BlockSpec((1, tm, BRANCH_W), lat_row)] * 3
    args = [xc, mod_rows, y, gates, wglu, bglu, wbr, wout] + [a[0] for a in attn]
    if with_ctx:
        in_specs += [pl.BlockSpec((1, tm, BRANCH_W), lambda i, t: (i, 0, 0))] * 3
        args += [a[1] for a in attn]
    return pl.pallas_call(
        functools.partial(_merge_kernel, lat_tiles=lat_tiles, with_ctx=with_ctx),
        grid=(b, n_rows // tm),
        in_specs=in_specs,
        out_specs=pl.BlockSpec((1, tm, d), row),
        out_shape=jax.ShapeDtypeStruct((b, n_rows, d), F32),
        scratch_shapes=[pltpu.VMEM((tm, BRANCH_W), F32)],
        compiler_params=_cparams(("parallel", "parallel")),
        name="merge",
    )(*args)


def _ffn(x1, mod_rows, layer, n_lat, nrm, win, wout):
    b, n_rows, d = x1.shape
    tm = ROW_TILE
    d_ff = wout.shape[0]
    ff_chunk = d_ff // 2 if (d_ff // 2) % LANES == 0 else d_ff
    row = lambda i, t: (i, t, 0)
    return pl.pallas_call(
        functools.partial(_ffn_kernel, ff_chunk=ff_chunk),
        grid=(b, n_rows // tm),
        in_specs=[pl.BlockSpec((1, tm, d), row),
                  pl.BlockSpec((1, 1, mod_rows.shape[-1]), _mod_index(layer, b, n_lat // tm)),
                  _const_spec(nrm.shape), _const_spec(win.shape), _const_spec(wout.shape)],
        out_specs=pl.BlockSpec((1, tm, d), row),
        out_shape=jax.ShapeDtypeStruct((b, n_rows, d), F32),
        compiler_params=_cparams(("parallel", "parallel")),
        name="ffn",
    )(x1, mod_rows, nrm, win, wout)


def _inproj_columns(d):
    de64, de128 = _deinterleave(64), _deinterleave(128)
    off_da = 512
    off_gq = off_da + 3 * DA_HEADS * DA_DV
    off_ml = off_gq + (GQ_HEADS + 2 * GQ_KV) * GQ_DH
    off_gate = off_ml + ML_QRANK + ML_KVRANK + ML_ROPE
    cols = [np.arange(512)]
    for part in range(2):
        for hd in range(DA_HEADS):
            for comp in range(2):
                cols.append(off_da + part * 512 + hd * 128 + comp * 64 + de64)
    cols.append(off_da + 1024 + np.arange(512))
    for hd in range(GQ_HEADS):
        cols.append(off_gq + hd * 128 + de128)
    for hd in range(GQ_KV):
        cols.append(off_gq + 512 + hd * 128 + de128)
    cols.append(off_gq + 768 + np.arange(256))
    cols.append(off_ml + np.arange(ML_QRANK + ML_KVRANK))
    cols.append(off_ml + ML_QRANK + ML_KVRANK + de64)
    cols.append(off_ml + ML_QRANK + ML_KVRANK + de64)
    cols.append(off_gate + np.arange(N_BRANCH * d))
    return np.concatenate(cols)


def _rope_table(n_lat, n_ctx):
    t = np.arange(n_lat)
    r = (t // GRID_W).astype(np.float32)
    col = (t % GRID_W).astype(np.float32)

    def angles(rot_dim):
        half = rot_dim // 2
        inv = jnp.asarray(ROPE_THETA, F32) ** (-jnp.arange(0, half, 2, dtype=F32) / half)
        return jnp.concatenate([jnp.asarray(r)[:, None] * inv, jnp.asarray(col)[:, None] * inv], axis=-1)

    a64, a128 = angles(64), angles(128)
    c64, s64 = jnp.cos(a64), jnp.sin(a64)
    c128, s128 = jnp.cos(a128), jnp.sin(a128)
    lat = jnp.concatenate([c64, c64, c64, c64, -s64, s64, -s64, s64, c128, c128, -s128, s128], axis=-1)
    ctx = jnp.concatenate([jnp.ones((n_ctx, 128), F32), jnp.zeros((n_ctx, 128), F32),
                           jnp.ones((n_ctx, 128), F32), jnp.zeros((n_ctx, 128), F32)], axis=-1)
    return jnp.concatenate([lat, ctx], axis=0)


def kernel(x, c, ctx, c_ctx, w_mod, b_mod, norm_mix, w_in, s5_lam_re, s5_lam_im, s5_log_step, s5_b_re, s5_b_im, s5_c_re, s5_c_im, s5_d, s5_w_glu, s5_b_glu, da_q_norm, da_k_norm, da_lam, da_subln, gq_q_norm, gq_k_norm, ml_cq_norm, ml_ckv_norm, ml_w_uq, ml_w_ukv, ml_q_norm, ml_k_norm, w_branch, w_out, norm_ffn, w_ffn_in, w_ffn_out):
    b, n_lat, d = x.shape
    n_ctx = ctx.shape[1]
    depth = w_mod.shape[0]
    assert n_ctx == ROW_TILE and n_lat % ROW_TILE == 0 and b < 8 and d == 1024

    cc = jnp.zeros((8, d), F32).at[:b].set(c).at[b].set(c_ctx)
    mod_rows = _modulation(cc, w_mod, b_mod).reshape(depth * 8, 1, 6 * d)
    rope = _rope_table(n_lat, n_ctx)
    xc = jnp.concatenate([x, ctx], axis=1)

    de64, de128 = _deinterleave(64), _deinterleave(128)
    cols = _inproj_columns(d)
    uq_cols = np.concatenate([hd * 192 + np.arange(128) for hd in range(ML_HEADS)]
                             + [hd * 192 + 128 + de64 for hd in range(ML_HEADS)])
    ukv_cols = np.concatenate([hd * 256 + np.arange(128) for hd in range(ML_HEADS)]
                              + [hd * 256 + 128 + np.arange(128) for hd in range(ML_HEADS)])
    log2e = math.log2(math.e)
    da_scale, gq_scale = DA_DK ** -0.5 * log2e, GQ_DH ** -0.5 * log2e
    ml_scale = (ML_NOPE + ML_ROPE) ** -0.5 * log2e

    for i in range(depth):
        last = i == depth - 1
        lam_init = 0.8 - 0.6 * math.exp(-0.3 * i)
        w = w_in[i][:, cols].astype(BF16)
        wuq = ml_w_uq[i][:, uq_cols].astype(BF16)
        wukv = ml_w_ukv[i][:, ukv_cols].astype(BF16)
        gains = jnp.concatenate([
            jnp.tile(da_q_norm[i][de64], 8) * da_scale, jnp.tile(da_k_norm[i][de64], 8),
            jnp.tile(gq_q_norm[i][de128], 4) * gq_scale, jnp.tile(gq_k_norm[i][de128], 2),
            ml_cq_norm[i], ml_ckv_norm[i],
            jnp.tile(ml_q_norm[i][:ML_NOPE], 4) * ml_scale,
            jnp.tile(ml_q_norm[i][ML_NOPE:][de64], 4) * ml_scale,
            jnp.tile(ml_k_norm[i][:ML_NOPE], 4), jnp.tile(ml_k_norm[i][ML_NOPE:][de64], 2),
        ]).astype(F32)[None, :]

        (u, daq, dak, dav, gqq, gqk, gqv, mlq, mlk, mlv, gates) = _inproj(
            xc, mod_rows, i, b, n_lat, norm_mix[i][None, :], gains, rope, w, wuq, wukv)

        s5_ops = _s5_prep(s5_lam_re[i], s5_lam_im[i], s5_log_step[i], s5_b_re[i], s5_b_im[i],
                          s5_c_re[i], s5_c_im[i], s5_d[i])
        y = _s5(u, n_lat, s5_ops)

        n_rows = n_lat if last else n_lat + n_ctx
        da = _attention("da", daq, dak, dav, n_lat, not last,
                        extra=(da_lam[i].astype(F32), da_subln[i].astype(F32)[None, :]),
                        lam_init=lam_init)
        gq = _attention("gq", gqq, gqk, gqv, n_lat, not last)
        ml = _attention("ml", mlq, mlk, mlv, n_lat, not last)

        x1 = _merge(xc, mod_rows, i, n_lat, n_rows, y, (da, gq, ml), gates,
                    s5_w_glu[i].astype(BF16), s5_b_glu[i].astype(F32)[None, :],
                    w_branch[i].astype(BF16), w_out[i].astype(BF16))
        xc = _ffn(x1, mod_rows, i, n_lat, norm_ffn[i][None, :],
                  w_ffn_in[i].astype(BF16), w_ffn_out[i].astype(BF16))
    return xc
```

```python
import functools
import math

import numpy as np
import jax
import jax.numpy as jnp
from jax import lax
from jax.experimental import pallas as pl
from jax.experimental.pallas import tpu as pltpu

GRID_W = 64
ROPE_THETA = 10000.0
NORM_EPS = 1e-6

S5_GROUP = 16
S5_STATE = 64
S5_CHUNK = 8

DA_HEADS = 4
DA_DK = 64
DA_DV = 128
GQ_HEADS = 4
GQ_KV = 2
GQ_DH = 128
ML_HEADS = 4
ML_QRANK = 256
ML_KVRANK = 128
ML_NOPE = 128
ML_ROPE = 64
ML_DV = 128
N_BRANCH = 4
BRANCH_W = 512

LANES = 128
S5_TILE_GROUPS = LANES // S5_GROUP
ROW_TILE = 256
VMEM_LIMIT = 56 * 1024 * 1024

F32 = jnp.float32
BF16 = jnp.bfloat16


def _deinterleave(n):
    return np.concatenate([np.arange(0, n, 2), np.arange(1, n, 2)])


def _cparams(sem):
    return pltpu.CompilerParams(dimension_semantics=sem, vmem_limit_bytes=VMEM_LIMIT)


def _const_spec(shape):
    nd = len(shape)
    return pl.BlockSpec(shape, lambda *_: (0,) * nd)


def _mod_kernel(cc_ref, w_ref, b_ref, o_ref):
    a = cc_ref[...]
    a = a * jax.nn.sigmoid(a)
    o_ref[0] = jnp.dot(a, w_ref[0], preferred_element_type=F32,
                       precision=lax.Precision.HIGHEST) + b_ref[0]


def _modulation(cc, w_mod, b_mod):
    depth, d, n = w_mod.shape
    tn = n // 4
    return pl.pallas_call(
        _mod_kernel,
        grid=(depth, n // tn),
        in_specs=[pl.BlockSpec((8, d), lambda i, j: (0, 0)),
                  pl.BlockSpec((1, d, tn), lambda i, j: (i, 0, j)),
                  pl.BlockSpec((1, 1, tn), lambda i, j: (i, 0, j))],
        out_specs=pl.BlockSpec((1, 8, tn), lambda i, j: (i, 0, j)),
        out_shape=jax.ShapeDtypeStruct((depth, 8, n), F32),
        compiler_params=_cparams(("arbitrary", "arbitrary")),
        name="modulation",
    )(cc, w_mod, b_mod.reshape(depth, 1, n))


def _seg_rms(z, seg):
    width = z.shape[-1]
    if seg == 2 * LANES:
        outs = []
        for g in range(width // seg):
            zg = z[:, g * seg:(g + 1) * seg]
            ms = jnp.sum(zg * zg, axis=-1, keepdims=True) * (1.0 / seg)
            outs.append(zg * lax.rsqrt(ms + NORM_EPS))
        return outs[0] if len(outs) == 1 else jnp.concatenate(outs, axis=-1)
    outs = []
    for g in range(width // LANES):
        zg = z[:, g * LANES:(g + 1) * LANES]
        zz = zg * zg
        if seg == LANES:
            ms = jnp.sum(zz, axis=-1, keepdims=True) * (1.0 / seg)
        else:
            lo = lax.broadcasted_iota(jnp.int32, zz.shape, 1) < seg
            s_lo = jnp.sum(jnp.where(lo, zz, 0.0), axis=-1, keepdims=True)
            s_hi = jnp.sum(jnp.where(lo, 0.0, zz), axis=-1, keepdims=True)
            ms = jnp.where(lo, s_lo, s_hi) * (1.0 / seg)
        outs.append(zg * lax.rsqrt(ms + NORM_EPS))
    return outs[0] if len(outs) == 1 else jnp.concatenate(outs, axis=-1)


def _rope(x, cos, sin, unit):
    outs = []
    for g in range(x.shape[-1] // LANES):
        xg = x[:, g * LANES:(g + 1) * LANES]
        if unit == LANES:
            rot = pltpu.roll(xg, LANES // 2, 1)
        else:
            lane = lax.broadcasted_iota(jnp.int32, xg.shape, 1)
            rot = jnp.where((lane & (unit // 2)) == 0,
                            pltpu.roll(xg, LANES - unit // 2, 1), pltpu.roll(xg, unit // 2, 1))
        outs.append(xg * cos + rot * sin)
    return outs[0] if len(outs) == 1 else jnp.concatenate(outs, axis=-1)


_C_S5 = 0
_C_DAQ = 512
_C_DAK = 1024
_C_DAV = 1536
_C_GQQ = 2048
_C_GQK = 2560
_C_GQV = 2816
_C_ML = 3072
_C_GATE = 3584
_C_END = 3584 + 4096

_G_DAQ, _G_DAK, _G_GQQ, _G_GQK = 0, 512, 1024, 1536
_G_CQ, _G_CKV, _G_QN, _G_QR, _G_KN, _G_KR, _G_END = 1792, 2048, 2176, 2688, 2944, 3456, 3584


def _stream_specs(stream, tm, lat_tiles):
    if isinstance(stream, tuple):
        lat, ctx = stream
        d = lat.shape[-1]
        return ([pl.BlockSpec((1, tm, d), lambda i, t: (i, jnp.minimum(t, lat_tiles - 1), 0)),
                 pl.BlockSpec((1, tm, d), lambda i, t: (i, 0, 0))], [lat, ctx])
    return [pl.BlockSpec((1, tm, stream.shape[-1]), lambda i, t: (i, t, 0))], [stream]


def _stream_tile(refs, lat_tiles):
    if len(refs) == 2:
        return jnp.where(pl.program_id(1) >= lat_tiles, refs[1][0], refs[0][0])
    return refs[0][0]


def _inproj_kernel(*refs, lat_tiles, n_stream):
    (mod_ref, nrm_ref, gains_ref, rope_ref, w_ref, wuq_ref, wukv_ref,
     u_ref, daq_ref, dak_ref, dav_ref, gqq_ref, gqk_ref, gqv_ref,
     mlq_ref, mlk_ref, mlv_ref, gate_ref, u_sc) = refs[n_stream:]
    x = _stream_tile(refs[:n_stream], lat_tiles)
    d = x.shape[-1]
    ms = jnp.mean(x * x, axis=-1, keepdims=True)
    mod = mod_ref[0]
    shift, scale = mod[:, 0:d], mod[:, d:2 * d]
    h = (x * lax.rsqrt(ms + NORM_EPS) * nrm_ref[...]) * (1.0 + scale) + shift
    hb = h.astype(BF16)

    def proj(c0, c1):
        return jnp.dot(hb, w_ref[:, c0:c1], preferred_element_type=F32)

    def gain(g0, g1):
        return gains_ref[:, g0:g1]

    cos64, sin64 = rope_ref[:, 0:128], rope_ref[:, 128:256]
    cos128, sin128 = rope_ref[:, 256:384], rope_ref[:, 384:512]

    zu = proj(_C_S5, _C_DAQ)
    per_tile = x.shape[0] // S5_CHUNK
    for j in range(u_sc.shape[0]):
        u_sc[j] = zu[:, j * LANES:(j + 1) * LANES]
        for s in range(S5_CHUNK):
            u_ref[0, s * per_tile:(s + 1) * per_tile, j * LANES:(j + 1) * LANES] = (
                u_sc[j, pl.ds(s, per_tile, stride=S5_CHUNK), :].astype(BF16))

    q = _seg_rms(proj(_C_DAQ, _C_DAK), DA_DK) * gain(_G_DAQ, _G_DAK)
    daq_ref[0] = _rope(q, cos64, sin64, DA_DK).astype(BF16)
    k = _seg_rms(proj(_C_DAK, _C_DAV), DA_DK) * gain(_G_DAK, _G_GQQ)
    dak_ref[0] = _rope(k, cos64, sin64, DA_DK).astype(BF16)
    dav_ref[0] = proj(_C_DAV, _C_GQQ).astype(BF16)

    q = _seg_rms(proj(_C_GQQ, _C_GQK), GQ_DH) * gain(_G_GQQ, _G_GQK)
    gqq_ref[0] = _rope(q, cos128, sin128, GQ_DH).astype(BF16)
    k = _seg_rms(proj(_C_GQK, _C_GQV), GQ_DH) * gain(_G_GQK, _G_CQ)
    gqk_ref[0] = _rope(k, cos128, sin128, GQ_DH).astype(BF16)
    gqv_ref[0] = proj(_C_GQV, _C_ML).astype(BF16)

    zc = proj(_C_ML, _C_GATE)
    cq = (_seg_rms(zc[:, 0:256], 256) * gain(_G_CQ, _G_CKV)).astype(BF16)
    ckv = (_seg_rms(zc[:, 256:384], 128) * gain(_G_CKV, _G_QN)).astype(BF16)
    qq = jnp.dot(cq, wuq_ref[...], preferred_element_type=F32)
    kv = jnp.dot(ckv, wukv_ref[...], preferred_element_type=F32)
    qn = (_seg_rms(qq[:, 0:512], ML_NOPE) * gain(_G_QN, _G_QR)).astype(BF16)
    qr = _seg_rms(qq[:, 512:768], ML_ROPE) * gain(_G_QR, _G_KN)
    qr = _rope(qr, cos64, sin64, ML_ROPE)
    kn = (_seg_rms(kv[:, 0:512], ML_NOPE) * gain(_G_KN, _G_KR)).astype(BF16)
    kr = _seg_rms(zc[:, 384:512], ML_ROPE) * gain(_G_KR, _G_END)
    kr = _rope(kr, cos64, sin64, ML_ROPE).astype(BF16)
    lo = lax.broadcasted_iota(jnp.int32, (x.shape[0], LANES), 1) < ML_ROPE
    for hd in range(ML_HEADS):
        pair = qr[:, (hd // 2) * LANES:(hd // 2 + 1) * LANES]
        keep = lo if hd % 2 == 0 else jnp.logical_not(lo)
        mlq_ref[0, :, hd * 256:hd * 256 + 128] = qn[:, hd * 128:(hd + 1) * 128]
        mlq_ref[0, :, hd * 256 + 128:(hd + 1) * 256] = jnp.where(keep, pair, 0.0).astype(BF16)
        mlk_ref[0, :, hd * 256:hd * 256 + 128] = kn[:, hd * 128:(hd + 1) * 128]
        mlk_ref[0, :, hd * 256 + 128:(hd + 1) * 256] = kr
    mlv_ref[0] = kv[:, 512:1024].astype(BF16)

    for j in range(N_BRANCH):
        c0 = _C_GATE + j * d
        gate_ref[0, :, j * d:(j + 1) * d] = jax.nn.sigmoid(proj(c0, c0 + d)).astype(BF16)


def _inproj(stream, mod_rows, layer, n_batch, n_lat, lt, nrm, gains, rope, w, wuq, wukv):
    tm = ROW_TILE
    nt = lt // tm
    lat_tiles = n_lat // tm
    d = w.shape[0]
    x_specs, x_args = _stream_specs(stream, tm, lat_tiles)

    def mod_idx(i, t):
        return (layer * 8 + jnp.where(t >= lat_tiles, n_batch, i), 0, 0)

    widths = [512, 512, 512, 512, 512, 256, 256, 1024, 1024, 512, N_BRANCH * d]
    return pl.pallas_call(
        functools.partial(_inproj_kernel, lat_tiles=lat_tiles, n_stream=len(x_args)),
        grid=(n_batch, nt),
        in_specs=x_specs + [pl.BlockSpec((1, 1, mod_rows.shape[-1]), mod_idx),
                            _const_spec(nrm.shape), _const_spec(gains.shape),
                            pl.BlockSpec((tm, 512), lambda i, t: (t, 0)),
                            _const_spec(w.shape), _const_spec(wuq.shape), _const_spec(wukv.shape)],
        out_specs=[pl.BlockSpec((1, tm, wd), lambda i, t: (i, t, 0)) for wd in widths],
        out_shape=[jax.ShapeDtypeStruct((n_batch, lt, wd), BF16) for wd in widths],
        scratch_shapes=[pltpu.VMEM((widths[0] // LANES, tm, LANES), F32)],
        compiler_params=_cparams(("parallel", "parallel")),
        name="inproj",
    )(*x_args, mod_rows, nrm, gains, rope, w, wuq, wukv)


def _shift(n):
    assert n & (n - 1) == 0
    return n.bit_length() - 1


def _s5_expand(mg_ref, wsg_ref, wog_ref, m_sc, ws_sc, wo_sc):
    n_rows, cw = mg_ref.shape[1], mg_ref.shape[2]
    n_c, gt = S5_GROUP, n_rows // cw
    n_sp = wsg_ref.shape[2]
    n_p = n_sp // 4
    n_state = gt * n_sp

    def iota2(shape):
        return lax.broadcasted_iota(jnp.int32, shape, 0), lax.broadcasted_iota(jnp.int32, shape, 1)

    def div(i, n):
        return lax.shift_right_logical(i, _shift(n))

    def mod(i, n):
        return i & (n - 1)

    def onehot(cond):
        return jnp.where(cond, 1.0, 0.0).astype(BF16)

    def lane_group(i):
        return mod(div(i, n_c), gt)

    def state_group(i):
        return mod(div(i, n_p), gt)

    r, q = iota2((n_rows, n_rows))
    regroup = onehot(q == lane_group(r) * cw + div(r, gt * n_c) * n_c + mod(r, n_c))
    same_lane_group = lane_group(r) == lane_group(q)

    k, q = iota2((cw, n_rows))
    spread = onehot((div(q, gt * n_c) == div(k, n_c)) & (mod(q, n_c) == mod(k, n_c)))
    t_m = jnp.dot(regroup, mg_ref[0], preferred_element_type=F32).astype(BF16)
    m_sc[...] = jnp.where(same_lane_group, jnp.dot(t_m, spread, preferred_element_type=F32), 0.0).astype(BF16)

    k, q = iota2((n_sp, n_state))
    spread_s = onehot((div(q, gt * n_p) == div(k, n_p)) & (mod(q, n_p) == mod(k, n_p)))
    r, q = iota2((n_rows, n_state))
    t_s = jnp.dot(regroup, wsg_ref[0], preferred_element_type=F32).astype(BF16)
    ws_sc[...] = jnp.where(lane_group(r) == state_group(q),
                           jnp.dot(t_s, spread_s, preferred_element_type=F32), 0.0).astype(BF16)

    r, k = iota2((n_state, n_sp))
    gather_s = onehot((div(r, gt * n_p) == div(k, n_p)) & (mod(r, n_p) == mod(k, n_p)))
    t_o = lax.dot_general(wog_ref[0], regroup, (((1,), (1,)), ((), ())),
                          preferred_element_type=F32).astype(BF16)
    r, q = iota2((n_state, n_rows))
    wo_sc[...] = jnp.where(state_group(r) == lane_group(q),
                           jnp.dot(gather_s, t_o, preferred_element_type=F32), 0.0).astype(BF16)


def _s5_kernel(u_ref, mg_ref, wsg_ref, wog_ref, are_ref, aim_ref, y_ref, s_sc, m_sc, ws_sc, wo_sc,
               *, n_chunks, n_lat_chunks):
    @pl.when(pl.program_id(1) == 0)
    def _():
        _s5_expand(mg_ref, wsg_ref, wog_ref, m_sc, ws_sc, wo_sc)

    t_len = u_ref.shape[2]
    quarter = s_sc.shape[1] // 4
    a = jnp.concatenate([u_ref[0, :, s].reshape(n_chunks, LANES) for s in range(t_len)], axis=-1)
    s_sc[...] = jnp.dot(a, ws_sc[...], preferred_element_type=F32)
    af_re, ab_re = are_ref[0][:, 0:quarter], are_ref[0][:, quarter:2 * quarter]
    af_im, ab_im = aim_ref[0][:, 0:quarter], aim_ref[0][:, quarter:2 * quarter]
    n_ctx_chunks = n_chunks - n_lat_chunks

    def step(i, carry):
        hf_re, hf_im, hb_re, hb_im = carry
        rf = pl.ds(jnp.where(i < n_ctx_chunks, n_lat_chunks + i, i - n_ctx_chunks), 1)
        rb = pl.ds(n_chunks - 1 - i, 1)
        sf_re, sf_im = s_sc[rf, 0:quarter], s_sc[rf, quarter:2 * quarter]
        sb_re, sb_im = s_sc[rb, 2 * quarter:3 * quarter], s_sc[rb, 3 * quarter:4 * quarter]
        s_sc[rf, 0:quarter] = hf_re
        s_sc[rf, quarter:2 * quarter] = hf_im
        s_sc[rb, 2 * quarter:3 * quarter] = hb_re
        s_sc[rb, 3 * quarter:4 * quarter] = hb_im
        return (af_re * hf_re - af_im * hf_im + sf_re, af_re * hf_im + af_im * hf_re + sf_im,
                ab_re * hb_re - ab_im * hb_im + sb_re, ab_re * hb_im + ab_im * hb_re + sb_im)

    zero = jnp.zeros((1, quarter), F32)
    lax.fori_loop(0, n_chunks, step, (zero, zero, zero, zero))

    y = (jnp.dot(a, m_sc[...], preferred_element_type=F32)
         + jnp.dot(s_sc[...].astype(BF16), wo_sc[...], preferred_element_type=F32))
    for t in range(t_len):
        y_ref[0, 0, :, t] = y[:, t * LANES:(t + 1) * LANES].reshape(y_ref.shape[2], y_ref.shape[4], LANES)


def _s5_prep(lam_re, lam_im, log_step, b_re, b_im, c_re, c_im, d_skip):
    hp = lax.Precision.HIGHEST
    t_len, n_g, n_p, n_c = S5_CHUNK, lam_re.shape[1], S5_STATE, S5_GROUP
    lam_re, lam_im = lam_re.astype(F32), lam_im.astype(F32)
    dt = jnp.exp(log_step.astype(F32))[..., None]
    mag = jnp.exp(lam_re * dt)
    ab_re, ab_im = mag * jnp.cos(lam_im * dt), mag * jnp.sin(lam_im * dt)
    den = lam_re * lam_re + lam_im * lam_im
    nr, ni = ab_re - 1.0, ab_im
    coef_re = (nr * lam_re + ni * lam_im) / den
    coef_im = (ni * lam_re - nr * lam_im) / den
    b_re, b_im = b_re.astype(F32), b_im.astype(F32)
    bb_re = coef_re[..., None] * b_re - coef_im[..., None] * b_im
    bb_im = coef_re[..., None] * b_im + coef_im[..., None] * b_re
    kk = jnp.arange(t_len + 1, dtype=F32)[:, None, None, None]
    pmag = jnp.exp(lam_re * dt * kk)
    pw_re, pw_im = pmag * jnp.cos(lam_im * dt * kk), pmag * jnp.sin(lam_im * dt * kk)
    c_re, c_im = c_re.astype(F32), c_im.astype(F32)
    cp_re = c_re[None] * pw_re[:, :, :, None, :] - c_im[None] * pw_im[:, :, :, None, :]
    cp_im = c_re[None] * pw_im[:, :, :, None, :] + c_im[None] * pw_re[:, :, :, None, :]
    kern = (jnp.einsum('tdgop,dgpc->tdgoc', cp_re, bb_re, precision=hp)
            - jnp.einsum('tdgop,dgpc->tdgoc', cp_im, bb_im, precision=hp))
    s_idx = np.arange(t_len)[:, None]
    t_idx = np.arange(t_len)[None, :]
    lag_f = np.clip(t_idx - s_idx, 0, t_len)
    lag_b = np.clip(s_idx - t_idx, 0, t_len)
    mf = kern[lag_f, 0] * jnp.asarray(t_idx >= s_idx, F32)[:, :, None, None, None]
    mb = kern[lag_b, 1] * jnp.asarray(s_idx >= t_idx, F32)[:, :, None, None, None]
    m_full = jnp.transpose(mf + mb, (2, 0, 4, 1, 3))
    eye_t = jnp.eye(t_len, dtype=F32)[None, :, None, :, None]
    eye_c = jnp.eye(n_c, dtype=F32)[None, None, :, None, :]
    m_full = m_full + eye_t * eye_c * d_skip.astype(F32).reshape(n_g, 1, n_c, 1, 1)

    def bpow(pw_r, pw_i, direction):
        re = pw_r[..., None] * bb_re[direction][None] - pw_i[..., None] * bb_im[direction][None]
        im = pw_r[..., None] * bb_im[direction][None] + pw_i[..., None] * bb_re[direction][None]
        return jnp.transpose(re, (1, 0, 3, 2)), jnp.transpose(im, (1, 0, 3, 2))

    rev = np.arange(t_len - 1, -1, -1)
    fwd = np.arange(t_len)
    f_re, f_im = bpow(pw_re[rev, 0], pw_im[rev, 0], 0)
    g_re, g_im = bpow(pw_re[fwd, 1], pw_im[fwd, 1], 1)

    def cpow(idx, direction):
        return (jnp.transpose(cp_re[idx, direction], (1, 3, 0, 2)),
                jnp.transpose(cp_im[idx, direction], (1, 3, 0, 2)))

    of_re, of_im = cpow(np.arange(1, t_len + 1), 0)
    ob_re, ob_im = cpow(np.arange(t_len, 0, -1), 1)

    gt = S5_TILE_GROUPS
    n_j = n_g // gt
    cw = t_len * n_c
    mg = m_full.reshape(n_j, gt * cw, cw)
    wsg = jnp.stack([f_re, f_im, g_re, g_im], axis=3).reshape(n_j, gt * cw, 4 * n_p)
    wog = jnp.stack([of_re, -of_im, ob_re, -ob_im], axis=1).reshape(n_j, gt, 4 * n_p, cw)
    wog = jnp.transpose(wog, (0, 2, 1, 3)).reshape(n_j, 4 * n_p, gt * cw)
    a_re = jnp.concatenate([pw_re[t_len, 0].reshape(n_j, 1, gt * n_p),
                            pw_re[t_len, 1].reshape(n_j, 1, gt * n_p)], axis=-1)
    a_im = jnp.concatenate([pw_im[t_len, 0].reshape(n_j, 1, gt * n_p),
                            pw_im[t_len, 1].reshape(n_j, 1, gt * n_p)], axis=-1)
    return mg.astype(BF16), wsg.astype(BF16), wog.astype(BF16), a_re, a_im


def _s5(u, n_lat, ops):
    mg, wsg, wog, a_re, a_im = ops
    b, lt, width = u.shape
    n_tiles = lt // ROW_TILE
    per_tile = ROW_TILE // S5_CHUNK
    n_chunks = n_tiles * per_tile
    n_rows = mg.shape[1]
    n_state = S5_TILE_GROUPS * wsg.shape[2]
    blk = (1, n_tiles, S5_CHUNK, per_tile, LANES)
    tile = lambda j, i: (i, 0, 0, 0, j)
    wsel = lambda j, i: (j, 0, 0)
    y = pl.pallas_call(
        functools.partial(_s5_kernel, n_chunks=n_chunks, n_lat_chunks=n_lat // S5_CHUNK),
        grid=(width // LANES, b),
        in_specs=[pl.BlockSpec(blk, tile),
                  pl.BlockSpec((1,) + mg.shape[1:], wsel), pl.BlockSpec((1,) + wsg.shape[1:], wsel),
                  pl.BlockSpec((1,) + wog.shape[1:], wsel),
                  pl.BlockSpec((1, 1, n_state // 2), wsel), pl.BlockSpec((1, 1, n_state // 2), wsel)],
        out_specs=pl.BlockSpec((1,) + blk, lambda j, i: (j, i, 0, 0, 0, 0)),
        out_shape=jax.ShapeDtypeStruct((width // LANES, b, n_tiles, S5_CHUNK, per_tile, LANES), F32),
        scratch_shapes=[pltpu.VMEM((n_chunks, n_state), F32), pltpu.VMEM((n_rows, n_rows), BF16),
                        pltpu.VMEM((n_rows, n_state), BF16), pltpu.VMEM((n_state, n_rows), BF16)],
        compiler_params=_cparams(("arbitrary", "arbitrary")),
        name="s5_scan",
    )(u.reshape(b, n_tiles, S5_CHUNK, per_tile, width), mg, wsg, wog, a_re, a_im)
    return y.reshape(width // LANES, b, lt, LANES)


ATTN_ROWS = 1024
ATTN_ROW_BLOCKS = 4
ATTN_KEY_TILE = 256


def _attn_kernel(*refs, mode, tq, tk, n_lat, n_extra, lam_init):
    q_ref, k_ref, v_ref = refs[:3]
    extra = refs[3:3 + n_extra]
    o_ref, acc_sc, sa_sc, sb_sc = refs[-4:]
    n_keys = k_ref.shape[1]
    n_ctx = n_keys - n_lat

    q = q_ref[0]
    if mode == "da":
        lo = lax.broadcasted_iota(jnp.int32, q.shape, 1) < DA_DK
        zero = jnp.zeros_like(q)
        qs = jnp.concatenate([jnp.where(lo, q, zero), jnp.where(lo, zero, q)], axis=0)
    elif mode == "gq":
        qs = jnp.concatenate([q[:, 0:GQ_DH], q[:, GQ_DH:2 * GQ_DH]], axis=0)
    else:
        qs = q
    n_rb = ATTN_ROW_BLOCKS
    rb = qs.shape[0] // n_rb
    q_blocks = [qs[r * rb:(r + 1) * rb] for r in range(n_rb)]

    def qk(r, start, size):
        kc = k_ref[0, pl.ds(start, size), :]
        s = lax.dot_general(q_blocks[r], kc, (((1,), (1,)), ((), ())), preferred_element_type=F32)
        return s, jnp.max(s, axis=-1, keepdims=True)

    def softmax_pv(r, s_buf, mx, m_old, start, size):
        m_new = jnp.maximum(m_old, mx)
        alpha = jnp.exp2(m_old - m_new)
        rows = pl.ds(r * rb, rb)
        kt = min(ATTN_KEY_TILE, size)
        pv, lsum = None, None
        for c0 in range(0, size, kt):
            p = jnp.exp2((s_buf[:, c0:c0 + kt] - m_new).astype(BF16))
            p32 = p.astype(F32)
            for i in range(kt // LANES):
                t = p32[:, i * LANES:(i + 1) * LANES]
                lsum = t if lsum is None else lsum + t
            t = jnp.dot(p, v_ref[0, pl.ds(start + c0, kt), :], preferred_element_type=F32)
            pv = t if pv is None else pv + t
        acc_sc[rows, 0:LANES] = alpha * acc_sc[rows, 0:LANES] + pv
        acc_sc[rows, LANES:2 * LANES] = alpha * acc_sc[rows, LANES:2 * LANES] + lsum
        return m_new

    def stage(cur_buf, nxt_buf, nxt_start, nxt_size, cur_start, cur_size, ms, mxs):
        new_ms, new_mxs = [], []
        for r in range(n_rb):
            rows = pl.ds(r * rb, rb)
            s_n, mx_n = qk(r, nxt_start, nxt_size)
            nxt_buf[rows, 0:nxt_size] = s_n
            new_mxs.append(mx_n)
            new_ms.append(softmax_pv(r, cur_buf.at[rows, :], mxs[r], ms[r], cur_start, cur_size))
        return new_ms, new_mxs

    acc_sc[...] = jnp.zeros_like(acc_sc)
    chunks = [(j * tk, tk) for j in range(n_lat // tk)] + [(n_lat, n_ctx)]
    bufs = (sa_sc, sb_sc)
    ms = [jnp.full((rb, 1), -1e30, F32) for _ in range(n_rb)]
    mxs = []
    for r in range(n_rb):
        s, mx = qk(r, *chunks[0])
        sa_sc[pl.ds(r * rb, rb), 0:chunks[0][1]] = s
        mxs.append(mx)
    for j in range(len(chunks) - 1):
        ms, mxs = stage(bufs[j % 2], bufs[(j + 1) % 2], *chunks[j + 1], *chunks[j], ms, mxs)
    last = len(chunks) - 1
    for r in range(n_rb):
        softmax_pv(r, bufs[last % 2].at[pl.ds(r * rb, rb), :], mxs[r], ms[r], *chunks[last])
    acc = acc_sc[...]
    o = acc[:, 0:LANES] / jnp.sum(acc[:, LANES:2 * LANES], axis=-1, keepdims=True)

    if mode == "da":
        lam_ref, sub_ref = extra
        lam = lam_ref[...]
        lam_full = (jnp.exp(jnp.sum(lam[0:1] * lam[1:2], axis=-1, keepdims=True))
                    - jnp.exp(jnp.sum(lam[2:3] * lam[3:4], axis=-1, keepdims=True)) + lam_init)
        dlt = o[0:tq] - lam_full * o[tq:2 * tq]
        ms_d = jnp.mean(dlt * dlt, axis=-1, keepdims=True)
        o_ref[0] = (dlt * lax.rsqrt(ms_d + NORM_EPS) * sub_ref[...] * (1.0 - lam_init)).astype(o_ref.dtype)
    elif mode == "gq":
        o_ref[0] = jnp.concatenate([o[0:tq], o[tq:2 * tq]], axis=-1).astype(o_ref.dtype)
    else:
        o_ref[0] = o.astype(o_ref.dtype)


def _pick_tk(n_lat):
    for tk in (1024, 512, 256):
        if n_lat % tk == 0:
            return tk
    raise ValueError("latent length must be a multiple of 256")


def _attention(mode, q, k, v, n_lat, with_ctx, extra=(), lam_init=0.0):
    b, lt, _ = q.shape
    n_ctx = lt - n_lat
    if mode == "da":
        heads, qw, kw, ow, g = DA_HEADS, 128, 128, 128, 2
    elif mode == "gq":
        heads, qw, kw, ow, g = GQ_KV, 256, 128, 256, 2
    else:
        heads, qw, kw, ow, g = ML_HEADS, 256, 256, 128, 1
    tk = _pick_tk(n_lat)

    def call(ctx_only):
        if ctx_only:
            tq, q0, n_tiles, kv_rows, kv_blk = n_ctx, n_lat // n_ctx, 1, n_ctx, n_lat // n_ctx
        else:
            tq = ATTN_ROWS // g if n_lat % (ATTN_ROWS // g) == 0 else ROW_TILE
            q0, n_tiles, kv_rows, kv_blk = 0, n_lat // tq, lt, 0
        in_specs = [pl.BlockSpec((1, tq, qw), lambda i, h, t: (i, t + q0, h)),
                    pl.BlockSpec((1, kv_rows, kw), lambda i, h, t: (i, kv_blk, h)),
                    pl.BlockSpec((1, kv_rows, 128), lambda i, h, t: (i, kv_blk, h))]
        in_specs += [_const_spec(e.shape) for e in extra]
        return pl.pallas_call(
            functools.partial(_attn_kernel, mode=mode, tq=tq, tk=tk, n_lat=0 if ctx_only else n_lat,
                              n_extra=len(extra), lam_init=lam_init),
            grid=(b, heads, n_tiles),
            in_specs=in_specs,
            out_specs=pl.BlockSpec((1, tq, ow), lambda i, h, t: (i, t, h)),
            out_shape=jax.ShapeDtypeStruct((b, n_tiles * tq, BRANCH_W), BF16),
            scratch_shapes=[pltpu.VMEM((g * tq, 2 * LANES), F32),
                            pltpu.VMEM((g * tq, max(tk, n_ctx)), F32),
                            pltpu.VMEM((g * tq, max(tk, n_ctx)), F32)],
            compiler_params=_cparams(("parallel", "parallel", "arbitrary")),
            name="attn_" + mode + ("_ctx" if ctx_only else ""),
        )(q, k, v, *extra)

    return call(False), (call(True) if with_ctx else None)


def _gelu_tanh(x):
    return 0.5 * x * (1.0 + jnp.tanh(math.sqrt(2.0 / math.pi) * (x + 0.044715 * (x * x * x))))


def _merge_kernel(*refs, lat_tiles, with_ctx, n_stream):
    x = _stream_tile(refs[:n_stream], lat_tiles)
    refs = refs[n_stream:]
    mod_ref, y_ref, gate_ref, wglu_ref, bglu_ref, wbr_ref, wout_ref = refs[:7]
    lat_refs = refs[7:10]
    ctx_refs = refs[10:13] if with_ctx else None
    o_ref, y_sc = refs[-2:]
    d = x.shape[-1]
    per_tile = y_sc.shape[0] // S5_CHUNK
    for j in range(y_ref.shape[0]):
        for n in range(per_tile):
            y_sc[n * S5_CHUNK:(n + 1) * S5_CHUNK, j * LANES:(j + 1) * LANES] = (
                y_ref[j, 0, pl.ds(n, S5_CHUNK, stride=per_tile), :])
    g = _gelu_tanh(y_sc[...])
    glu = jax.nn.sigmoid(jnp.dot(g.astype(BF16), wglu_ref[...], preferred_element_type=F32)
                         + bglu_ref[...])
    branches = [(g * glu).astype(BF16)]
    for n in range(3):
        if with_ctx:
            is_ctx = pl.program_id(1) >= lat_tiles
            branches.append(jnp.where(is_ctx, ctx_refs[n][0], lat_refs[n][0]))
        else:
            branches.append(lat_refs[n][0])
    acc = None
    for n, br in enumerate(branches):
        term = gate_ref[0, :, n * d:(n + 1) * d].astype(F32) * jnp.dot(
            br, wbr_ref[n], preferred_element_type=F32)
        acc = term if acc is None else acc + term
    out = jnp.dot(acc.astype(BF16), wout_ref[...], preferred_element_type=F32)
    g1 = mod_ref[0][:, 2 * d:3 * d]
    o_ref[0] = x + g1 * out


def _ffn_kernel(x_ref, mod_ref, nrm_ref, win_ref, wout_ref, o_ref, *, ff_chunk):
    d = x_ref.shape[-1]
    d_ff = wout_ref.shape[0]
    x = x_ref[0]
    mod = mod_ref[0]
    shift, scale, g2 = mod[:, 3 * d:4 * d], mod[:, 4 * d:5 * d], mod[:, 5 * d:6 * d]
    ms = jnp.mean(x * x, axis=-1, keepdims=True)
    hb = ((x * lax.rsqrt(ms + NORM_EPS) * nrm_ref[...]) * (1.0 + scale) + shift).astype(BF16)
    out = None
    for c0 in range(0, d_ff, ff_chunk):
        gate = jnp.dot(hb, win_ref[:, c0:c0 + ff_chunk], preferred_element_type=F32)
        up = jnp.dot(hb, win_ref[:, d_ff + c0:d_ff + c0 + ff_chunk], preferred_element_type=F32)
        act = (gate * jax.nn.sigmoid(gate) * up).astype(BF16)
        part = jnp.dot(act, wout_ref[c0:c0 + ff_chunk, :], preferred_element_type=F32)
        out = part if out is None else out + part
    o_ref[0] = x + g2 * out


def _mod_index(layer, n_batch, lat_tiles):
    def idx(i, t):
        return (layer * 8 + jnp.where(t >= lat_tiles, n_batch, i), 0, 0)
    return idx


def _merge(stream, mod_rows, layer, n_lat, n_rows, y, attn, gates, wglu, bglu, wbr, wout):
    b, d = y.shape[1], wout.shape[0]
    tm = ROW_TILE
    lat_tiles = n_lat // tm
    with_ctx = n_rows > n_lat
    row = lambda i, t: (i, t, 0)
    lat_row = lambda i, t: (i, jnp.minimum(t, lat_tiles - 1), 0)
    x_specs, x_args = _stream_specs(stream, tm, lat_tiles)
    in_specs = x_specs + [
        pl.BlockSpec((1, 1, mod_rows.shape[-1]), _mod_index(layer, b, lat_tiles)),
        pl.BlockSpec((BRANCH_W // LANES, 1, tm, LANES), lambda i, t: (0, i, t, 0)),
        pl.BlockSpec((1, tm, N_BRANCH * d), row),
        _const_spec(wglu.shape), _const_spec(bglu.shape),
        _const_spec(wbr.shape), _const_spec(wout.shape)]
    in_specs += [pl.BlockSpec((1, tm, BRANCH_W), lat_row)] * 3
    args = x_args + [mod_rows, y, gates, wglu, bglu, wbr, wout] + [a[0] for a in attn]
    if with_ctx:
        in_specs += [pl.BlockSpec((1, tm, BRANCH_W), lambda i, t: (i, 0, 0))] * 3
        args += [a[1] for a in attn]
    return pl.pallas_call(
        functools.partial(_merge_kernel, lat_tiles=lat_tiles, with_ctx=with_ctx, n_stream=len(x_args)),
        grid=(b, n_rows // tm),
        in_specs=in_specs,
        out_specs=pl.BlockSpec((1, tm, d), row),
        out_shape=jax.ShapeDtypeStruct((b, n_rows, d), F32),
        scratch_shapes=[pltpu.VMEM((tm, BRANCH_W), F32)],
        compiler_params=_cparams(("parallel", "parallel")),
        name="merge",
    )(*args)


def _ffn(x1, mod_rows, layer, n_lat, nrm, win, wout):
    b, n_rows, d = x1.shape
    tm = ROW_TILE
    d_ff = wout.shape[0]
    ff_chunk = d_ff // 2 if (d_ff // 2) % LANES == 0 else d_ff
    row = lambda i, t: (i, t, 0)
    return pl.pallas_call(
        functools.partial(_ffn_kernel, ff_chunk=ff_chunk),
        grid=(b, n_rows // tm),
        in_specs=[pl.BlockSpec((1, tm, d), row),
                  pl.BlockSpec((1, 1, mod_rows.shape[-1]), _mod_index(layer, b, n_lat // tm)),
                  _const_spec(nrm.shape), _const_spec(win.shape), _const_spec(wout.shape)],
        out_specs=pl.BlockSpec((1, tm, d), row),
        out_shape=jax.ShapeDtypeStruct((b, n_rows, d), F32),
        compiler_params=_cparams(("parallel", "parallel")),
        name="ffn",
    )(x1, mod_rows, nrm, win, wout)


def _inproj_columns(d):
    de64, de128 = _deinterleave(64), _deinterleave(128)
    off_da = 512
    off_gq = off_da + 3 * DA_HEADS * DA_DV
    off_ml = off_gq + (GQ_HEADS + 2 * GQ_KV) * GQ_DH
    off_gate = off_ml + ML_QRANK + ML_KVRANK + ML_ROPE
    cols = [np.arange(512)]
    for part in range(2):
        for hd in range(DA_HEADS):
            for comp in range(2):
                cols.append(off_da + part * 512 + hd * 128 + comp * 64 + de64)
    cols.append(off_da + 1024 + np.arange(512))
    for hd in range(GQ_HEADS):
        cols.append(off_gq + hd * 128 + de128)
    for hd in range(GQ_KV):
        cols.append(off_gq + 512 + hd * 128 + de128)
    cols.append(off_gq + 768 + np.arange(256))
    cols.append(off_ml + np.arange(ML_QRANK + ML_KVRANK))
    cols.append(off_ml + ML_QRANK + ML_KVRANK + de64)
    cols.append(off_ml + ML_QRANK + ML_KVRANK + de64)
    cols.append(off_gate + np.arange(N_BRANCH * d))
    return np.concatenate(cols)


def _rope_table(n_lat, n_ctx):
    t = np.arange(n_lat)
    r = (t // GRID_W).astype(np.float32)
    col = (t % GRID_W).astype(np.float32)

    def angles(rot_dim):
        half = rot_dim // 2
        inv = jnp.asarray(ROPE_THETA, F32) ** (-jnp.arange(0, half, 2, dtype=F32) / half)
        return jnp.concatenate([jnp.asarray(r)[:, None] * inv, jnp.asarray(col)[:, None] * inv], axis=-1)

    a64, a128 = angles(64), angles(128)
    c64, s64 = jnp.cos(a64), jnp.sin(a64)
    c128, s128 = jnp.cos(a128), jnp.sin(a128)
    lat = jnp.concatenate([c64, c64, c64, c64, -s64, s64, -s64, s64, c128, c128, -s128, s128], axis=-1)
    ctx = jnp.concatenate([jnp.ones((n_ctx, 128), F32), jnp.zeros((n_ctx, 128), F32),
                           jnp.ones((n_ctx, 128), F32), jnp.zeros((n_ctx, 128), F32)], axis=-1)
    return jnp.concatenate([lat, ctx], axis=0)


def kernel(x, c, ctx, c_ctx, w_mod, b_mod, norm_mix, w_in, s5_lam_re, s5_lam_im, s5_log_step, s5_b_re, s5_b_im, s5_c_re, s5_c_im, s5_d, s5_w_glu, s5_b_glu, da_q_norm, da_k_norm, da_lam, da_subln, gq_q_norm, gq_k_norm, ml_cq_norm, ml_ckv_norm, ml_w_uq, ml_w_ukv, ml_q_norm, ml_k_norm, w_branch, w_out, norm_ffn, w_ffn_in, w_ffn_out):
    b, n_lat, d = x.shape
    n_ctx = ctx.shape[1]
    depth = w_mod.shape[0]
    assert n_ctx == ROW_TILE and n_lat % ROW_TILE == 0 and b < 8 and d == 1024

    cc = jnp.zeros((8, d), F32).at[:b].set(c).at[b].set(c_ctx)
    mod_rows = _modulation(cc, w_mod, b_mod).reshape(depth * 8, 1, 6 * d)
    rope = _rope_table(n_lat, n_ctx)
    xc = (x, ctx)

    de64, de128 = _deinterleave(64), _deinterleave(128)
    cols = _inproj_columns(d)
    uq_cols = np.concatenate([hd * 192 + np.arange(128) for hd in range(ML_HEADS)]
                             + [hd * 192 + 128 + de64 for hd in range(ML_HEADS)])
    ukv_cols = np.concatenate([hd * 256 + np.arange(128) for hd in range(ML_HEADS)]
                              + [hd * 256 + 128 + np.arange(128) for hd in range(ML_HEADS)])
    log2e = math.log2(math.e)
    da_scale, gq_scale = DA_DK ** -0.5 * log2e, GQ_DH ** -0.5 * log2e
    ml_scale = (ML_NOPE + ML_ROPE) ** -0.5 * log2e

    for i in range(depth):
        last = i == depth - 1
        lam_init = 0.8 - 0.6 * math.exp(-0.3 * i)
        w = w_in[i][:, cols].astype(BF16)
        wuq = ml_w_uq[i][:, uq_cols].astype(BF16)
        wukv = ml_w_ukv[i][:, ukv_cols].astype(BF16)
        gains = jnp.concatenate([
            jnp.tile(da_q_norm[i][de64], 8) * da_scale, jnp.tile(da_k_norm[i][de64], 8),
            jnp.tile(gq_q_norm[i][de128], 4) * gq_scale, jnp.tile(gq_k_norm[i][de128], 2),
            ml_cq_norm[i], ml_ckv_norm[i],
            jnp.tile(ml_q_norm[i][:ML_NOPE], 4) * ml_scale,
            jnp.tile(ml_q_norm[i][ML_NOPE:][de64], 4) * ml_scale,
            jnp.tile(ml_k_norm[i][:ML_NOPE], 4), jnp.tile(ml_k_norm[i][ML_NOPE:][de64], 2),
        ]).astype(F32)[None, :]

        (u, daq, dak, dav, gqq, gqk, gqv, mlq, mlk, mlv, gates) = _inproj(
            xc, mod_rows, i, b, n_lat, n_lat + n_ctx, norm_mix[i][None, :], gains, rope, w, wuq, wukv)

        s5_ops = _s5_prep(s5_lam_re[i], s5_lam_im[i], s5_log_step[i], s5_b_re[i], s5_b_im[i],
                          s5_c_re[i], s5_c_im[i], s5_d[i])
        y = _s5(u, n_lat, s5_ops)

        n_rows = n_lat if last else n_lat + n_ctx
        da = _attention("da", daq, dak, dav, n_lat, not last,
                        extra=(da_lam[i].astype(F32), da_subln[i].astype(F32)[None, :]),
                        lam_init=lam_init)
        gq = _attention("gq", gqq, gqk, gqv, n_lat, not last)
        ml = _attention("ml", mlq, mlk, mlv, n_lat, not last)

        x1 = _merge(xc, mod_rows, i, n_lat, n_rows, y, (da, gq, ml), gates,
                    s5_w_glu[i].astype(BF16), s5_b_glu[i].astype(F32)[None, :],
                    w_branch[i].astype(BF16), w_out[i].astype(BF16))
        xc = _ffn(x1, mod_rows, i, n_lat, norm_ffn[i][None, :],
                  w_ffn_in[i].astype(BF16), w_ffn_out[i].astype(BF16))
    return xc
```

```python
import functools
import math

import numpy as np
import jax
import jax.numpy as jnp
from jax import lax
from jax.experimental import pallas as pl
from jax.experimental.pallas import tpu as pltpu

GRID_W = 64
ROPE_THETA = 10000.0
NORM_EPS = 1e-6

S5_GROUP = 16
S5_STATE = 64
S5_CHUNK = 8

DA_HEADS = 4
DA_DK = 64
DA_DV = 128
GQ_HEADS = 4
GQ_KV = 2
GQ_DH = 128
ML_HEADS = 4
ML_QRANK = 256
ML_KVRANK = 128
ML_NOPE = 128
ML_ROPE = 64
ML_DV = 128
N_BRANCH = 4
BRANCH_W = 512

LANES = 128
S5_TILE_GROUPS = LANES // S5_GROUP
ROW_TILE = 256
VMEM_LIMIT = 56 * 1024 * 1024

F32 = jnp.float32
BF16 = jnp.bfloat16


def _deinterleave(n):
    return np.concatenate([np.arange(0, n, 2), np.arange(1, n, 2)])


def _cparams(sem):
    return pltpu.CompilerParams(dimension_semantics=sem, vmem_limit_bytes=VMEM_LIMIT)


def _const_spec(shape):
    nd = len(shape)
    return pl.BlockSpec(shape, lambda *_: (0,) * nd)


def _mod_kernel(cc_ref, w_ref, b_ref, o_ref):
    a = cc_ref[...]
    a = a * jax.nn.sigmoid(a)
    o_ref[0] = jnp.dot(a, w_ref[0], preferred_element_type=F32,
                       precision=lax.Precision.HIGHEST) + b_ref[0]


def _modulation(cc, w_mod, b_mod):
    depth, d, n = w_mod.shape
    tn = n // 4
    return pl.pallas_call(
        _mod_kernel,
        grid=(depth, n // tn),
        in_specs=[pl.BlockSpec((8, d), lambda i, j: (0, 0)),
                  pl.BlockSpec((1, d, tn), lambda i, j: (i, 0, j)),
                  pl.BlockSpec((1, 1, tn), lambda i, j: (i, 0, j))],
        out_specs=pl.BlockSpec((1, 8, tn), lambda i, j: (i, 0, j)),
        out_shape=jax.ShapeDtypeStruct((depth, 8, n), F32),
        compiler_params=_cparams(("arbitrary", "arbitrary")),
        name="modulation",
    )(cc, w_mod, b_mod.reshape(depth, 1, n))


def _seg_rms(z, seg):
    width = z.shape[-1]
    if seg == 2 * LANES:
        outs = []
        for g in range(width // seg):
            zg = z[:, g * seg:(g + 1) * seg]
            ms = jnp.sum(zg * zg, axis=-1, keepdims=True) * (1.0 / seg)
            outs.append(zg * lax.rsqrt(ms + NORM_EPS))
        return outs[0] if len(outs) == 1 else jnp.concatenate(outs, axis=-1)
    outs = []
    for g in range(width // LANES):
        zg = z[:, g * LANES:(g + 1) * LANES]
        zz = zg * zg
        if seg == LANES:
            ms = jnp.sum(zz, axis=-1, keepdims=True) * (1.0 / seg)
        else:
            lo = lax.broadcasted_iota(jnp.int32, zz.shape, 1) < seg
            s_lo = jnp.sum(jnp.where(lo, zz, 0.0), axis=-1, keepdims=True)
            s_hi = jnp.sum(jnp.where(lo, 0.0, zz), axis=-1, keepdims=True)
            ms = jnp.where(lo, s_lo, s_hi) * (1.0 / seg)
        outs.append(zg * lax.rsqrt(ms + NORM_EPS))
    return outs[0] if len(outs) == 1 else jnp.concatenate(outs, axis=-1)


def _rope(x, cos, sin, unit):
    outs = []
    for g in range(x.shape[-1] // LANES):
        xg = x[:, g * LANES:(g + 1) * LANES]
        if unit == LANES:
            rot = pltpu.roll(xg, LANES // 2, 1)
        else:
            lane = lax.broadcasted_iota(jnp.int32, xg.shape, 1)
            rot = jnp.where((lane & (unit // 2)) == 0,
                            pltpu.roll(xg, LANES - unit // 2, 1), pltpu.roll(xg, unit // 2, 1))
        outs.append(xg * cos + rot * sin)
    return outs[0] if len(outs) == 1 else jnp.concatenate(outs, axis=-1)


_C_S5 = 0
_C_DAQ = 512
_C_DAK = 1024
_C_DAV = 1536
_C_GQQ = 2048
_C_GQK = 2560
_C_GQV = 2816
_C_ML = 3072
_C_GATE = 3584
_C_END = 3584 + 4096

_G_DAQ, _G_DAK, _G_GQQ, _G_GQK = 0, 512, 1024, 1536
_G_CQ, _G_CKV, _G_QN, _G_QR, _G_KN, _G_KR, _G_END = 1792, 2048, 2176, 2688, 2944, 3456, 3584


def _stream_specs(stream, tm, lat_tiles):
    if isinstance(stream, tuple):
        lat, ctx = stream
        d = lat.shape[-1]
        return ([pl.BlockSpec((1, tm, d), lambda i, t: (i, jnp.minimum(t, lat_tiles - 1), 0)),
                 pl.BlockSpec((1, tm, d), lambda i, t: (i, 0, 0))], [lat, ctx])
    return [pl.BlockSpec((1, tm, stream.shape[-1]), lambda i, t: (i, t, 0))], [stream]


def _stream_tile(refs, lat_tiles):
    if len(refs) == 2:
        return jnp.where(pl.program_id(1) >= lat_tiles, refs[1][0], refs[0][0])
    return refs[0][0]


def _inproj_kernel(*refs, lat_tiles, n_stream):
    (mod_ref, nrm_ref, gains_ref, rope_ref, w_ref, wuq_ref, wukv_ref,
     u_ref, daq_ref, dak_ref, dav_ref, gqq_ref, gqk_ref, gqv_ref,
     mlq_ref, mlk_ref, mlv_ref, gate_ref, u_sc) = refs[n_stream:]
    x = _stream_tile(refs[:n_stream], lat_tiles)
    d = x.shape[-1]
    ms = jnp.mean(x * x, axis=-1, keepdims=True)
    mod = mod_ref[0]
    shift, scale = mod[:, 0:d], mod[:, d:2 * d]
    h = (x * lax.rsqrt(ms + NORM_EPS) * nrm_ref[...]) * (1.0 + scale) + shift
    hb = h.astype(BF16)

    def proj(c0, c1):
        return jnp.dot(hb, w_ref[:, c0:c1], preferred_element_type=F32)

    def gain(g0, g1):
        return gains_ref[:, g0:g1]

    cos64, sin64 = rope_ref[:, 0:128], rope_ref[:, 128:256]
    cos128, sin128 = rope_ref[:, 256:384], rope_ref[:, 384:512]

    zu = proj(_C_S5, _C_DAQ)
    per_tile = x.shape[0] // S5_CHUNK
    for j in range(u_sc.shape[0]):
        u_sc[j] = zu[:, j * LANES:(j + 1) * LANES]
        for s in range(S5_CHUNK):
            u_ref[0, s * per_tile:(s + 1) * per_tile, j * LANES:(j + 1) * LANES] = (
                u_sc[j, pl.ds(s, per_tile, stride=S5_CHUNK), :].astype(BF16))

    q = _seg_rms(proj(_C_DAQ, _C_DAK), DA_DK) * gain(_G_DAQ, _G_DAK)
    daq_ref[0] = _rope(q, cos64, sin64, DA_DK).astype(BF16)
    k = _seg_rms(proj(_C_DAK, _C_DAV), DA_DK) * gain(_G_DAK, _G_GQQ)
    dak_ref[0] = _rope(k, cos64, sin64, DA_DK).astype(BF16)
    dav_ref[0] = proj(_C_DAV, _C_GQQ).astype(BF16)

    q = _seg_rms(proj(_C_GQQ, _C_GQK), GQ_DH) * gain(_G_GQQ, _G_GQK)
    gqq_ref[0] = _rope(q, cos128, sin128, GQ_DH).astype(BF16)
    k = _seg_rms(proj(_C_GQK, _C_GQV), GQ_DH) * gain(_G_GQK, _G_CQ)
    gqk_ref[0] = _rope(k, cos128, sin128, GQ_DH).astype(BF16)
    gqv_ref[0] = proj(_C_GQV, _C_ML).astype(BF16)

    zc = proj(_C_ML, _C_GATE)
    cq = (_seg_rms(zc[:, 0:256], 256) * gain(_G_CQ, _G_CKV)).astype(BF16)
    ckv = (_seg_rms(zc[:, 256:384], 128) * gain(_G_CKV, _G_QN)).astype(BF16)
    qq = jnp.dot(cq, wuq_ref[...], preferred_element_type=F32)
    kv = jnp.dot(ckv, wukv_ref[...], preferred_element_type=F32)
    qn = (_seg_rms(qq[:, 0:512], ML_NOPE) * gain(_G_QN, _G_QR)).astype(BF16)
    qr = _seg_rms(qq[:, 512:768], ML_ROPE) * gain(_G_QR, _G_KN)
    qr = _rope(qr, cos64, sin64, ML_ROPE)
    kn = (_seg_rms(kv[:, 0:512], ML_NOPE) * gain(_G_KN, _G_KR)).astype(BF16)
    kr = _seg_rms(zc[:, 384:512], ML_ROPE) * gain(_G_KR, _G_END)
    kr = _rope(kr, cos64, sin64, ML_ROPE).astype(BF16)
    lo = lax.broadcasted_iota(jnp.int32, (x.shape[0], LANES), 1) < ML_ROPE
    for hd in range(ML_HEADS):
        pair = qr[:, (hd // 2) * LANES:(hd // 2 + 1) * LANES]
        keep = lo if hd % 2 == 0 else jnp.logical_not(lo)
        mlq_ref[0, :, hd * 256:hd * 256 + 128] = qn[:, hd * 128:(hd + 1) * 128]
        mlq_ref[0, :, hd * 256 + 128:(hd + 1) * 256] = jnp.where(keep, pair, 0.0).astype(BF16)
        mlk_ref[0, :, hd * 256:hd * 256 + 128] = kn[:, hd * 128:(hd + 1) * 128]
        mlk_ref[0, :, hd * 256 + 128:(hd + 1) * 256] = kr
    mlv_ref[0] = kv[:, 512:1024].astype(BF16)

    for j in range(N_BRANCH):
        c0 = _C_GATE + j * d
        gate_ref[0, :, j * d:(j + 1) * d] = jax.nn.sigmoid(proj(c0, c0 + d)).astype(BF16)


def _inproj(stream, mod_rows, layer, n_batch, n_lat, lt, nrm, gains, rope, w, wuq, wukv):
    tm = ROW_TILE
    nt = lt // tm
    lat_tiles = n_lat // tm
    d = w.shape[0]
    x_specs, x_args = _stream_specs(stream, tm, lat_tiles)

    def mod_idx(i, t):
        return (layer * 8 + jnp.where(t >= lat_tiles, n_batch, i), 0, 0)

    widths = [512, 512, 512, 512, 512, 256, 256, 1024, 1024, 512, N_BRANCH * d]
    return pl.pallas_call(
        functools.partial(_inproj_kernel, lat_tiles=lat_tiles, n_stream=len(x_args)),
        grid=(n_batch, nt),
        in_specs=x_specs + [pl.BlockSpec((1, 1, mod_rows.shape[-1]), mod_idx),
                            _const_spec(nrm.shape), _const_spec(gains.shape),
                            pl.BlockSpec((tm, 512), lambda i, t: (t, 0)),
                            _const_spec(w.shape), _const_spec(wuq.shape), _const_spec(wukv.shape)],
        out_specs=[pl.BlockSpec((1, tm, wd), lambda i, t: (i, t, 0)) for wd in widths],
        out_shape=[jax.ShapeDtypeStruct((n_batch, lt, wd), BF16) for wd in widths],
        scratch_shapes=[pltpu.VMEM((widths[0] // LANES, tm, LANES), F32)],
        compiler_params=_cparams(("parallel", "parallel")),
        name="inproj",
    )(*x_args, mod_rows, nrm, gains, rope, w, wuq, wukv)


def _shift(n):
    assert n & (n - 1) == 0
    return n.bit_length() - 1


def _s5_expand(mg_ref, wsg_ref, wog_ref, m_sc, ws_sc, wo_sc):
    n_rows, cw = mg_ref.shape[1], mg_ref.shape[2]
    n_c, gt = S5_GROUP, n_rows // cw
    n_sp = wsg_ref.shape[2]
    n_p = n_sp // 4
    n_state = gt * n_sp

    def iota2(shape):
        return lax.broadcasted_iota(jnp.int32, shape, 0), lax.broadcasted_iota(jnp.int32, shape, 1)

    def div(i, n):
        return lax.shift_right_logical(i, _shift(n))

    def mod(i, n):
        return i & (n - 1)

    def onehot(cond):
        return jnp.where(cond, 1.0, 0.0).astype(BF16)

    def lane_group(i):
        return mod(div(i, n_c), gt)

    def state_group(i):
        return mod(div(i, n_p), gt)

    r, q = iota2((n_rows, n_rows))
    regroup = onehot(q == lane_group(r) * cw + div(r, gt * n_c) * n_c + mod(r, n_c))
    same_lane_group = lane_group(r) == lane_group(q)

    k, q = iota2((cw, n_rows))
    spread = onehot((div(q, gt * n_c) == div(k, n_c)) & (mod(q, n_c) == mod(k, n_c)))
    t_m = jnp.dot(regroup, mg_ref[0], preferred_element_type=F32).astype(BF16)
    m_sc[...] = jnp.where(same_lane_group, jnp.dot(t_m, spread, preferred_element_type=F32), 0.0).astype(BF16)

    k, q = iota2((n_sp, n_state))
    spread_s = onehot((div(q, gt * n_p) == div(k, n_p)) & (mod(q, n_p) == mod(k, n_p)))
    r, q = iota2((n_rows, n_state))
    t_s = jnp.dot(regroup, wsg_ref[0], preferred_element_type=F32).astype(BF16)
    ws_sc[...] = jnp.where(lane_group(r) == state_group(q),
                           jnp.dot(t_s, spread_s, preferred_element_type=F32), 0.0).astype(BF16)

    r, k = iota2((n_state, n_sp))
    gather_s = onehot((div(r, gt * n_p) == div(k, n_p)) & (mod(r, n_p) == mod(k, n_p)))
    t_o = lax.dot_general(wog_ref[0], regroup, (((1,), (1,)), ((), ())),
                          preferred_element_type=F32).astype(BF16)
    r, q = iota2((n_state, n_rows))
    wo_sc[...] = jnp.where(state_group(r) == lane_group(q),
                           jnp.dot(gather_s, t_o, preferred_element_type=F32), 0.0).astype(BF16)


def _s5_kernel(u_ref, mg_ref, wsg_ref, wog_ref, are_ref, aim_ref, y_ref, s_sc, m_sc, ws_sc, wo_sc,
               *, n_chunks, n_lat_chunks):
    @pl.when(pl.program_id(1) == 0)
    def _():
        _s5_expand(mg_ref, wsg_ref, wog_ref, m_sc, ws_sc, wo_sc)

    t_len = u_ref.shape[2]
    quarter = s_sc.shape[1] // 4
    a = jnp.concatenate([u_ref[0, :, s].reshape(n_chunks, LANES) for s in range(t_len)], axis=-1)
    s_sc[...] = jnp.dot(a, ws_sc[...], preferred_element_type=F32)
    af_re, ab_re = are_ref[0][:, 0:quarter], are_ref[0][:, quarter:2 * quarter]
    af_im, ab_im = aim_ref[0][:, 0:quarter], aim_ref[0][:, quarter:2 * quarter]
    n_ctx_chunks = n_chunks - n_lat_chunks

    def step(i, carry):
        hf_re, hf_im, hb_re, hb_im = carry
        rf = pl.ds(jnp.where(i < n_ctx_chunks, n_lat_chunks + i, i - n_ctx_chunks), 1)
        rb = pl.ds(n_chunks - 1 - i, 1)
        sf_re, sf_im = s_sc[rf, 0:quarter], s_sc[rf, quarter:2 * quarter]
        sb_re, sb_im = s_sc[rb, 2 * quarter:3 * quarter], s_sc[rb, 3 * quarter:4 * quarter]
        s_sc[rf, 0:quarter] = hf_re
        s_sc[rf, quarter:2 * quarter] = hf_im
        s_sc[rb, 2 * quarter:3 * quarter] = hb_re
        s_sc[rb, 3 * quarter:4 * quarter] = hb_im
        return (af_re * hf_re - af_im * hf_im + sf_re, af_re * hf_im + af_im * hf_re + sf_im,
                ab_re * hb_re - ab_im * hb_im + sb_re, ab_re * hb_im + ab_im * hb_re + sb_im)

    zero = jnp.zeros((1, quarter), F32)
    lax.fori_loop(0, n_chunks, step, (zero, zero, zero, zero))

    y = (jnp.dot(a, m_sc[...], preferred_element_type=F32)
         + jnp.dot(s_sc[...].astype(BF16), wo_sc[...], preferred_element_type=F32))
    for t in range(t_len):
        y_ref[0, 0, :, t] = y[:, t * LANES:(t + 1) * LANES].reshape(y_ref.shape[2], y_ref.shape[4], LANES)


def _s5_prep(lam_re, lam_im, log_step, b_re, b_im, c_re, c_im, d_skip):
    hp = lax.Precision.HIGHEST
    t_len, n_g, n_p, n_c = S5_CHUNK, lam_re.shape[1], S5_STATE, S5_GROUP
    lam_re, lam_im = lam_re.astype(F32), lam_im.astype(F32)
    dt = jnp.exp(log_step.astype(F32))[..., None]
    mag = jnp.exp(lam_re * dt)
    ab_re, ab_im = mag * jnp.cos(lam_im * dt), mag * jnp.sin(lam_im * dt)
    den = lam_re * lam_re + lam_im * lam_im
    nr, ni = ab_re - 1.0, ab_im
    coef_re = (nr * lam_re + ni * lam_im) / den
    coef_im = (ni * lam_re - nr * lam_im) / den
    b_re, b_im = b_re.astype(F32), b_im.astype(F32)
    bb_re = coef_re[..., None] * b_re - coef_im[..., None] * b_im
    bb_im = coef_re[..., None] * b_im + coef_im[..., None] * b_re
    kk = jnp.arange(t_len + 1, dtype=F32)[:, None, None, None]
    pmag = jnp.exp(lam_re * dt * kk)
    pw_re, pw_im = pmag * jnp.cos(lam_im * dt * kk), pmag * jnp.sin(lam_im * dt * kk)
    c_re, c_im = c_re.astype(F32), c_im.astype(F32)
    cp_re = c_re[None] * pw_re[:, :, :, None, :] - c_im[None] * pw_im[:, :, :, None, :]
    cp_im = c_re[None] * pw_im[:, :, :, None, :] + c_im[None] * pw_re[:, :, :, None, :]
    kern = (jnp.einsum('tdgop,dgpc->tdgoc', cp_re, bb_re, precision=hp)
            - jnp.einsum('tdgop,dgpc->tdgoc', cp_im, bb_im, precision=hp))
    s_idx = np.arange(t_len)[:, None]
    t_idx = np.arange(t_len)[None, :]
    lag_f = np.clip(t_idx - s_idx, 0, t_len)
    lag_b = np.clip(s_idx - t_idx, 0, t_len)
    mf = kern[lag_f, 0] * jnp.asarray(t_idx >= s_idx, F32)[:, :, None, None, None]
    mb = kern[lag_b, 1] * jnp.asarray(s_idx >= t_idx, F32)[:, :, None, None, None]
    m_full = jnp.transpose(mf + mb, (2, 0, 4, 1, 3))
    eye_t = jnp.eye(t_len, dtype=F32)[None, :, None, :, None]
    eye_c = jnp.eye(n_c, dtype=F32)[None, None, :, None, :]
    m_full = m_full + eye_t * eye_c * d_skip.astype(F32).reshape(n_g, 1, n_c, 1, 1)

    def bpow(pw_r, pw_i, direction):
        re = pw_r[..., None] * bb_re[direction][None] - pw_i[..., None] * bb_im[direction][None]
        im = pw_r[..., None] * bb_im[direction][None] + pw_i[..., None] * bb_re[direction][None]
        return jnp.transpose(re, (1, 0, 3, 2)), jnp.transpose(im, (1, 0, 3, 2))

    rev = np.arange(t_len - 1, -1, -1)
    fwd = np.arange(t_len)
    f_re, f_im = bpow(pw_re[rev, 0], pw_im[rev, 0], 0)
    g_re, g_im = bpow(pw_re[fwd, 1], pw_im[fwd, 1], 1)

    def cpow(idx, direction):
        return (jnp.transpose(cp_re[idx, direction], (1, 3, 0, 2)),
                jnp.transpose(cp_im[idx, direction], (1, 3, 0, 2)))

    of_re, of_im = cpow(np.arange(1, t_len + 1), 0)
    ob_re, ob_im = cpow(np.arange(t_len, 0, -1), 1)

    gt = S5_TILE_GROUPS
    n_j = n_g // gt
    cw = t_len * n_c
    mg = m_full.reshape(n_j, gt * cw, cw)
    wsg = jnp.stack([f_re, f_im, g_re, g_im], axis=3).reshape(n_j, gt * cw, 4 * n_p)
    wog = jnp.stack([of_re, -of_im, ob_re, -ob_im], axis=1).reshape(n_j, gt, 4 * n_p, cw)
    wog = jnp.transpose(wog, (0, 2, 1, 3)).reshape(n_j, 4 * n_p, gt * cw)
    a_re = jnp.concatenate([pw_re[t_len, 0].reshape(n_j, 1, gt * n_p),
                            pw_re[t_len, 1].reshape(n_j, 1, gt * n_p)], axis=-1)
    a_im = jnp.concatenate([pw_im[t_len, 0].reshape(n_j, 1, gt * n_p),
                            pw_im[t_len, 1].reshape(n_j, 1, gt * n_p)], axis=-1)
    return mg.astype(BF16), wsg.astype(BF16), wog.astype(BF16), a_re, a_im


def _s5(u, n_lat, ops):
    mg, wsg, wog, a_re, a_im = ops
    b, lt, width = u.shape
    n_tiles = lt // ROW_TILE
    per_tile = ROW_TILE // S5_CHUNK
    n_chunks = n_tiles * per_tile
    n_rows = mg.shape[1]
    n_state = S5_TILE_GROUPS * wsg.shape[2]
    blk = (1, n_tiles, S5_CHUNK, per_tile, LANES)
    tile = lambda j, i: (i, 0, 0, 0, j)
    wsel = lambda j, i: (j, 0, 0)
    y = pl.pallas_call(
        functools.partial(_s5_kernel, n_chunks=n_chunks, n_lat_chunks=n_lat // S5_CHUNK),
        grid=(width // LANES, b),
        in_specs=[pl.BlockSpec(blk, tile),
                  pl.BlockSpec((1,) + mg.shape[1:], wsel), pl.BlockSpec((1,) + wsg.shape[1:], wsel),
                  pl.BlockSpec((1,) + wog.shape[1:], wsel),
                  pl.BlockSpec((1, 1, n_state // 2), wsel), pl.BlockSpec((1, 1, n_state // 2), wsel)],
        out_specs=pl.BlockSpec((1,) + blk, lambda j, i: (j, i, 0, 0, 0, 0)),
        out_shape=jax.ShapeDtypeStruct((width // LANES, b, n_tiles, S5_CHUNK, per_tile, LANES), F32),
        scratch_shapes=[pltpu.VMEM((n_chunks, n_state), F32), pltpu.VMEM((n_rows, n_rows), BF16),
                        pltpu.VMEM((n_rows, n_state), BF16), pltpu.VMEM((n_state, n_rows), BF16)],
        compiler_params=_cparams(("arbitrary", "arbitrary")),
        name="s5_scan",
    )(u.reshape(b, n_tiles, S5_CHUNK, per_tile, width), mg, wsg, wog, a_re, a_im)
    return y.reshape(width // LANES, b, lt, LANES)


ATTN_ROWS = 1024
ATTN_ROW_BLOCKS = 4
ATTN_KEY_TILE = 256


def _attn_kernel(*refs, mode, tq, tk, n_lat, n_extra, lam_init):
    q_ref, k_ref, v_ref = refs[:3]
    extra = refs[3:3 + n_extra]
    o_ref, acc_sc, sa_sc, sb_sc = refs[-4:]
    n_keys = k_ref.shape[1]
    n_ctx = n_keys - n_lat

    q = q_ref[0]
    if mode == "da":
        lo = lax.broadcasted_iota(jnp.int32, q.shape, 1) < DA_DK
        zero = jnp.zeros_like(q)
        qs = jnp.concatenate([jnp.where(lo, q, zero), jnp.where(lo, zero, q)], axis=0)
    elif mode == "gq":
        qs = jnp.concatenate([q[:, 0:GQ_DH], q[:, GQ_DH:2 * GQ_DH]], axis=0)
    else:
        qs = q
    n_rb = ATTN_ROW_BLOCKS
    rb = qs.shape[0] // n_rb
    q_blocks = [qs[r * rb:(r + 1) * rb] for r in range(n_rb)]

    def qk(r, start, size):
        kc = k_ref[0, pl.ds(start, size), :]
        s = lax.dot_general(q_blocks[r], kc, (((1,), (1,)), ((), ())), preferred_element_type=F32)
        return s, jnp.max(s, axis=-1, keepdims=True)

    def softmax_pv(r, s_buf, mx, m_old, start, size):
        m_new = jnp.maximum(m_old, mx)
        alpha = jnp.exp2(m_old - m_new)
        rows = pl.ds(r * rb, rb)
        kt = min(ATTN_KEY_TILE, size)
        pv, lsum = None, None
        for c0 in range(0, size, kt):
            p = jnp.exp2((s_buf[:, c0:c0 + kt] - m_new).astype(BF16))
            p32 = p.astype(F32)
            for i in range(kt // LANES):
                t = p32[:, i * LANES:(i + 1) * LANES]
                lsum = t if lsum is None else lsum + t
            t = jnp.dot(p, v_ref[0, pl.ds(start + c0, kt), :], preferred_element_type=F32)
            pv = t if pv is None else pv + t
        acc_sc[rows, 0:LANES] = alpha * acc_sc[rows, 0:LANES] + pv
        acc_sc[rows, LANES:2 * LANES] = alpha * acc_sc[rows, LANES:2 * LANES] + lsum
        return m_new

    def stage(cur_buf, nxt_buf, nxt_start, nxt_size, cur_start, cur_size, ms, mxs):
        new_ms, new_mxs = [], []
        for r in range(n_rb):
            rows = pl.ds(r * rb, rb)
            s_n, mx_n = qk(r, nxt_start, nxt_size)
            nxt_buf[rows, 0:nxt_size] = s_n
            new_mxs.append(mx_n)
            new_ms.append(softmax_pv(r, cur_buf.at[rows, :], mxs[r], ms[r], cur_start, cur_size))
        return new_ms, new_mxs

    acc_sc[...] = jnp.zeros_like(acc_sc)
    chunks = [(j * tk, tk) for j in range(n_lat // tk)] + [(n_lat, n_ctx)]
    bufs = (sa_sc, sb_sc)
    ms = [jnp.full((rb, 1), -1e30, F32) for _ in range(n_rb)]
    mxs = []
    for r in range(n_rb):
        s, mx = qk(r, *chunks[0])
        sa_sc[pl.ds(r * rb, rb), 0:chunks[0][1]] = s
        mxs.append(mx)
    for j in range(len(chunks) - 1):
        ms, mxs = stage(bufs[j % 2], bufs[(j + 1) % 2], *chunks[j + 1], *chunks[j], ms, mxs)
    last = len(chunks) - 1
    for r in range(n_rb):
        softmax_pv(r, bufs[last % 2].at[pl.ds(r * rb, rb), :], mxs[r], ms[r], *chunks[last])
    acc = acc_sc[...]
    o = acc[:, 0:LANES] / jnp.sum(acc[:, LANES:2 * LANES], axis=-1, keepdims=True)

    if mode == "da":
        lam_ref, sub_ref = extra
        lam = lam_ref[...]
        lam_full = (jnp.exp(jnp.sum(lam[0:1] * lam[1:2], axis=-1, keepdims=True))
                    - jnp.exp(jnp.sum(lam[2:3] * lam[3:4], axis=-1, keepdims=True)) + lam_init)
        dlt = o[0:tq] - lam_full * o[tq:2 * tq]
        ms_d = jnp.mean(dlt * dlt, axis=-1, keepdims=True)
        o_ref[0] = (dlt * lax.rsqrt(ms_d + NORM_EPS) * sub_ref[...] * (1.0 - lam_init)).astype(o_ref.dtype)
    elif mode == "gq":
        o_ref[0] = jnp.concatenate([o[0:tq], o[tq:2 * tq]], axis=-1).astype(o_ref.dtype)
    else:
        o_ref[0] = o.astype(o_ref.dtype)


def _pick_tk(n_lat):
    for tk in (1024, 512, 256):
        if n_lat % tk == 0:
            return tk
    raise ValueError("latent length must be a multiple of 256")


def _attention(mode, q, k, v, n_lat, with_ctx, extra=(), lam_init=0.0):
    b, lt, _ = q.shape
    n_ctx = lt - n_lat
    if mode == "da":
        heads, qw, kw, ow, g = DA_HEADS, 128, 128, 128, 2
    elif mode == "gq":
        heads, qw, kw, ow, g = GQ_KV, 256, 128, 256, 2
    else:
        heads, qw, kw, ow, g = ML_HEADS, 256, 256, 128, 1
    tk = _pick_tk(n_lat)

    def call(ctx_only):
        if ctx_only:
            tq, q0, n_tiles, kv_rows, kv_blk = n_ctx, n_lat // n_ctx, 1, n_ctx, n_lat // n_ctx
        else:
            tq = ATTN_ROWS // g if n_lat % (ATTN_ROWS // g) == 0 else ROW_TILE
            q0, n_tiles, kv_rows, kv_blk = 0, n_lat // tq, lt, 0
        in_specs = [pl.BlockSpec((1, tq, qw), lambda i, h, t: (i, t + q0, h)),
                    pl.BlockSpec((1, kv_rows, kw), lambda i, h, t: (i, kv_blk, h)),
                    pl.BlockSpec((1, kv_rows, 128), lambda i, h, t: (i, kv_blk, h))]
        in_specs += [_const_spec(e.shape) for e in extra]
        return pl.pallas_call(
            functools.partial(_attn_kernel, mode=mode, tq=tq, tk=tk, n_lat=0 if ctx_only else n_lat,
                              n_extra=len(extra), lam_init=lam_init),
            grid=(b, heads, n_tiles),
            in_specs=in_specs,
            out_specs=pl.BlockSpec((1, tq, ow), lambda i, h, t: (i, t, h)),
            out_shape=jax.ShapeDtypeStruct((b, n_tiles * tq, BRANCH_W), BF16),
            scratch_shapes=[pltpu.VMEM((g * tq, 2 * LANES), F32),
                            pltpu.VMEM((g * tq, max(tk, n_ctx)), F32),
                            pltpu.VMEM((g * tq, max(tk, n_ctx)), F32)],
            compiler_params=_cparams(("parallel", "parallel", "arbitrary")),
            name="attn_" + mode + ("_ctx" if ctx_only else ""),
        )(q, k, v, *extra)

    return call(False), (call(True) if with_ctx else None)


def _gelu_tanh(x):
    return 0.5 * x * (1.0 + jnp.tanh(math.sqrt(2.0 / math.pi) * (x + 0.044715 * (x * x * x))))


def _merge_ffn_kernel(*refs, lat_tiles, with_ctx, n_stream, ff_chunk):
    x = _stream_tile(refs[:n_stream], lat_tiles)
    refs = refs[n_stream:]
    (mod_ref, y_ref, gate_ref, wglu_ref, bglu_ref, wbr_ref, wout_ref,
     nrm_ref, win_ref, wffn_ref) = refs[:10]
    lat_refs = refs[10:13]
    ctx_refs = refs[13:16] if with_ctx else None
    o_ref, y_sc = refs[-2:]
    d = x.shape[-1]
    mod = mod_ref[0]
    per_tile = y_sc.shape[0] // S5_CHUNK
    for j in range(y_ref.shape[0]):
        for n in range(per_tile):
            y_sc[n * S5_CHUNK:(n + 1) * S5_CHUNK, j * LANES:(j + 1) * LANES] = (
                y_ref[j, 0, pl.ds(n, S5_CHUNK, stride=per_tile), :])
    g = _gelu_tanh(y_sc[...])
    glu = jax.nn.sigmoid(jnp.dot(g.astype(BF16), wglu_ref[...], preferred_element_type=F32)
                         + bglu_ref[...])
    branches = [(g * glu).astype(BF16)]
    for n in range(3):
        if with_ctx:
            is_ctx = pl.program_id(1) >= lat_tiles
            branches.append(jnp.where(is_ctx, ctx_refs[n][0], lat_refs[n][0]))
        else:
            branches.append(lat_refs[n][0])
    acc = None
    for n, br in enumerate(branches):
        term = gate_ref[0, :, n * d:(n + 1) * d].astype(F32) * jnp.dot(
            br, wbr_ref[n], preferred_element_type=F32)
        acc = term if acc is None else acc + term
    out = jnp.dot(acc.astype(BF16), wout_ref[...], preferred_element_type=F32)
    x = x + mod[:, 2 * d:3 * d] * out

    d_ff = wffn_ref.shape[0]
    shift, scale, g2 = mod[:, 3 * d:4 * d], mod[:, 4 * d:5 * d], mod[:, 5 * d:6 * d]
    ms = jnp.mean(x * x, axis=-1, keepdims=True)
    hb = ((x * lax.rsqrt(ms + NORM_EPS) * nrm_ref[...]) * (1.0 + scale) + shift).astype(BF16)
    out = None
    for c0 in range(0, d_ff, ff_chunk):
        gate = jnp.dot(hb, win_ref[:, c0:c0 + ff_chunk], preferred_element_type=F32)
        up = jnp.dot(hb, win_ref[:, d_ff + c0:d_ff + c0 + ff_chunk], preferred_element_type=F32)
        act = (gate * jax.nn.sigmoid(gate) * up).astype(BF16)
        part = jnp.dot(act, wffn_ref[c0:c0 + ff_chunk, :], preferred_element_type=F32)
        out = part if out is None else out + part
    o_ref[0] = x + g2 * out


def _mod_index(layer, n_batch, lat_tiles):
    def idx(i, t):
        return (layer * 8 + jnp.where(t >= lat_tiles, n_batch, i), 0, 0)
    return idx


def _merge_ffn(stream, mod_rows, layer, n_lat, n_rows, y, attn, gates, wglu, bglu, wbr, wout,
               nrm_ffn, win, wffn):
    b, d = y.shape[1], wout.shape[0]
    tm = ROW_TILE
    lat_tiles = n_lat // tm
    with_ctx = n_rows > n_lat
    d_ff = wffn.shape[0]
    ff_chunk = d_ff // 2 if (d_ff // 2) % LANES == 0 else d_ff
    row = lambda i, t: (i, t, 0)
    lat_row = lambda i, t: (i, jnp.minimum(t, lat_tiles - 1), 0)
    x_specs, x_args = _stream_specs(stream, tm, lat_tiles)
    weights = [wglu, bglu, wbr, wout, nrm_ffn, win, wffn]
    in_specs = x_specs + [
        pl.BlockSpec((1, 1, mod_rows.shape[-1]), _mod_index(layer, b, lat_tiles)),
        pl.BlockSpec((BRANCH_W // LANES, 1, tm, LANES), lambda i, t: (0, i, t, 0)),
        pl.BlockSpec((1, tm, N_BRANCH * d), row)] + [_const_spec(w.shape) for w in weights]
    in_specs += [pl.BlockSpec((1, tm, BRANCH_W), lat_row)] * 3
    args = x_args + [mod_rows, y, gates] + weights + [a[0] for a in attn]
    if with_ctx:
        in_specs += [pl.BlockSpec((1, tm, BRANCH_W), lambda i, t: (i, 0, 0))] * 3
        args += [a[1] for a in attn]
    return pl.pallas_call(
        functools.partial(_merge_ffn_kernel, lat_tiles=lat_tiles, with_ctx=with_ctx,
                          n_stream=len(x_args), ff_chunk=ff_chunk),
        grid=(b, n_rows // tm),
        in_specs=in_specs,
        out_specs=pl.BlockSpec((1, tm, d), row),
        out_shape=jax.ShapeDtypeStruct((b, n_rows, d), F32),
        scratch_shapes=[pltpu.VMEM((tm, BRANCH_W), F32)],
        compiler_params=_cparams(("parallel", "parallel")),
        name="merge_ffn",
    )(*args)


def _inproj_columns(d):
    de64, de128 = _deinterleave(64), _deinterleave(128)
    off_da = 512
    off_gq = off_da + 3 * DA_HEADS * DA_DV
    off_ml = off_gq + (GQ_HEADS + 2 * GQ_KV) * GQ_DH
    off_gate = off_ml + ML_QRANK + ML_KVRANK + ML_ROPE
    cols = [np.arange(512)]
    for part in range(2):
        for hd in range(DA_HEADS):
            for comp in range(2):
                cols.append(off_da + part * 512 + hd * 128 + comp * 64 + de64)
    cols.append(off_da + 1024 + np.arange(512))
    for hd in range(GQ_HEADS):
        cols.append(off_gq + hd * 128 + de128)
    for hd in range(GQ_KV):
        cols.append(off_gq + 512 + hd * 128 + de128)
    cols.append(off_gq + 768 + np.arange(256))
    cols.append(off_ml + np.arange(ML_QRANK + ML_KVRANK))
    cols.append(off_ml + ML_QRANK + ML_KVRANK + de64)
    cols.append(off_ml + ML_QRANK + ML_KVRANK + de64)
    cols.append(off_gate + np.arange(N_BRANCH * d))
    return np.concatenate(cols)


def _rope_table(n_lat, n_ctx):
    t = np.arange(n_lat)
    r = (t // GRID_W).astype(np.float32)
    col = (t % GRID_W).astype(np.float32)

    def angles(rot_dim):
        half = rot_dim // 2
        inv = jnp.asarray(ROPE_THETA, F32) ** (-jnp.arange(0, half, 2, dtype=F32) / half)
        return jnp.concatenate([jnp.asarray(r)[:, None] * inv, jnp.asarray(col)[:, None] * inv], axis=-1)

    a64, a128 = angles(64), angles(128)
    c64, s64 = jnp.cos(a64), jnp.sin(a64)
    c128, s128 = jnp.cos(a128), jnp.sin(a128)
    lat = jnp.concatenate([c64, c64, c64, c64, -s64, s64, -s64, s64, c128, c128, -s128, s128], axis=-1)
    ctx = jnp.concatenate([jnp.ones((n_ctx, 128), F32), jnp.zeros((n_ctx, 128), F32),
                           jnp.ones((n_ctx, 128), F32), jnp.zeros((n_ctx, 128), F32)], axis=-1)
    return jnp.concatenate([lat, ctx], axis=0)


def kernel(x, c, ctx, c_ctx, w_mod, b_mod, norm_mix, w_in, s5_lam_re, s5_lam_im, s5_log_step, s5_b_re, s5_b_im, s5_c_re, s5_c_im, s5_d, s5_w_glu, s5_b_glu, da_q_norm, da_k_norm, da_lam, da_subln, gq_q_norm, gq_k_norm, ml_cq_norm, ml_ckv_norm, ml_w_uq, ml_w_ukv, ml_q_norm, ml_k_norm, w_branch, w_out, norm_ffn, w_ffn_in, w_ffn_out):
    b, n_lat, d = x.shape
    n_ctx = ctx.shape[1]
    depth = w_mod.shape[0]
    assert n_ctx == ROW_TILE and n_lat % ROW_TILE == 0 and b < 8 and d == 1024

    cc = jnp.zeros((8, d), F32).at[:b].set(c).at[b].set(c_ctx)
    mod_rows = _modulation(cc, w_mod, b_mod).reshape(depth * 8, 1, 6 * d)
    rope = _rope_table(n_lat, n_ctx)
    xc = (x, ctx)

    de64, de128 = _deinterleave(64), _deinterleave(128)
    cols = _inproj_columns(d)
    uq_cols = np.concatenate([hd * 192 + np.arange(128) for hd in range(ML_HEADS)]
                             + [hd * 192 + 128 + de64 for hd in range(ML_HEADS)])
    ukv_cols = np.concatenate([hd * 256 + np.arange(128) for hd in range(ML_HEADS)]
                              + [hd * 256 + 128 + np.arange(128) for hd in range(ML_HEADS)])
    log2e = math.log2(math.e)
    da_scale, gq_scale = DA_DK ** -0.5 * log2e, GQ_DH ** -0.5 * log2e
    ml_scale = (ML_NOPE + ML_ROPE) ** -0.5 * log2e

    for i in range(depth):
        last = i == depth - 1
        lam_init = 0.8 - 0.6 * math.exp(-0.3 * i)
        w = w_in[i][:, cols].astype(BF16)
        wuq = ml_w_uq[i][:, uq_cols].astype(BF16)
        wukv = ml_w_ukv[i][:, ukv_cols].astype(BF16)
        gains = jnp.concatenate([
            jnp.tile(da_q_norm[i][de64], 8) * da_scale, jnp.tile(da_k_norm[i][de64], 8),
            jnp.tile(gq_q_norm[i][de128], 4) * gq_scale, jnp.tile(gq_k_norm[i][de128], 2),
            ml_cq_norm[i], ml_ckv_norm[i],
            jnp.tile(ml_q_norm[i][:ML_NOPE], 4) * ml_scale,
            jnp.tile(ml_q_norm[i][ML_NOPE:][de64], 4) * ml_scale,
            jnp.tile(ml_k_norm[i][:ML_NOPE], 4), jnp.tile(ml_k_norm[i][ML_NOPE:][de64], 2),
        ]).astype(F32)[None, :]

        (u, daq, dak, dav, gqq, gqk, gqv, mlq, mlk, mlv, gates) = _inproj(
            xc, mod_rows, i, b, n_lat, n_lat + n_ctx, norm_mix[i][None, :], gains, rope, w, wuq, wukv)

        s5_ops = _s5_prep(s5_lam_re[i], s5_lam_im[i], s5_log_step[i], s5_b_re[i], s5_b_im[i],
                          s5_c_re[i], s5_c_im[i], s5_d[i])
        y = _s5(u, n_lat, s5_ops)

        n_rows = n_lat if last else n_lat + n_ctx
        da = _attention("da", daq, dak, dav, n_lat, not last,
                        extra=(da_lam[i].astype(F32), da_subln[i].astype(F32)[None, :]),
                        lam_init=lam_init)
        gq = _attention("gq", gqq, gqk, gqv, n_lat, not last)
        ml = _attention("ml", mlq, mlk, mlv, n_lat, not last)

        xc = _merge_ffn(xc, mod_rows, i, n_lat, n_rows, y, (da, gq, ml), gates,
                        s5_w_glu[i].astype(BF16), s5_b_glu[i].astype(F32)[None, :],
                        w_branch[i].astype(BF16), w_out[i].astype(BF16),
                        norm_ffn[i][None, :], w_ffn_in[i].astype(BF16), w_ffn_out[i].astype(BF16))
    return xc
```

```python
import functools
import math

import numpy as np
import jax
import jax.numpy as jnp
from jax import lax
from jax.experimental import pallas as pl
from jax.experimental.pallas import tpu as pltpu

GRID_W = 64
ROPE_THETA = 10000.0
NORM_EPS = 1e-6

S5_GROUP = 16
S5_STATE = 64
S5_CHUNK = 8

DA_HEADS = 4
DA_DK = 64
DA_DV = 128
GQ_HEADS = 4
GQ_KV = 2
GQ_DH = 128
ML_HEADS = 4
ML_QRANK = 256
ML_KVRANK = 128
ML_NOPE = 128
ML_ROPE = 64
ML_DV = 128
N_BRANCH = 4
BRANCH_W = 512

LANES = 128
S5_TILE_GROUPS = LANES // S5_GROUP
ROW_TILE = 256
VMEM_LIMIT = 56 * 1024 * 1024

F32 = jnp.float32
BF16 = jnp.bfloat16


def _deinterleave(n):
    return np.concatenate([np.arange(0, n, 2), np.arange(1, n, 2)])


def _cparams(sem):
    return pltpu.CompilerParams(dimension_semantics=sem, vmem_limit_bytes=VMEM_LIMIT)


def _const_spec(shape):
    nd = len(shape)
    return pl.BlockSpec(shape, lambda *_: (0,) * nd)


def _mod_kernel(cc_ref, w_ref, b_ref, o_ref):
    a = cc_ref[...]
    a = a * jax.nn.sigmoid(a)
    o_ref[0] = jnp.dot(a, w_ref[0], preferred_element_type=F32,
                       precision=lax.Precision.HIGHEST) + b_ref[0]


def _modulation(cc, w_mod, b_mod):
    depth, d, n = w_mod.shape
    tn = n // 4
    return pl.pallas_call(
        _mod_kernel,
        grid=(depth, n // tn),
        in_specs=[pl.BlockSpec((8, d), lambda i, j: (0, 0)),
                  pl.BlockSpec((1, d, tn), lambda i, j: (i, 0, j)),
                  pl.BlockSpec((1, 1, tn), lambda i, j: (i, 0, j))],
        out_specs=pl.BlockSpec((1, 8, tn), lambda i, j: (i, 0, j)),
        out_shape=jax.ShapeDtypeStruct((depth, 8, n), F32),
        compiler_params=_cparams(("arbitrary", "arbitrary")),
        name="modulation",
    )(cc, w_mod, b_mod.reshape(depth, 1, n))


def _seg_rms(z, seg):
    width = z.shape[-1]
    if seg == 2 * LANES:
        outs = []
        for g in range(width // seg):
            zg = z[:, g * seg:(g + 1) * seg]
            ms = jnp.sum(zg * zg, axis=-1, keepdims=True) * (1.0 / seg)
            outs.append(zg * lax.rsqrt(ms + NORM_EPS))
        return outs[0] if len(outs) == 1 else jnp.concatenate(outs, axis=-1)
    outs = []
    for g in range(width // LANES):
        zg = z[:, g * LANES:(g + 1) * LANES]
        zz = zg * zg
        if seg == LANES:
            ms = jnp.sum(zz, axis=-1, keepdims=True) * (1.0 / seg)
        else:
            lo = lax.broadcasted_iota(jnp.int32, zz.shape, 1) < seg
            s_lo = jnp.sum(jnp.where(lo, zz, 0.0), axis=-1, keepdims=True)
            s_hi = jnp.sum(jnp.where(lo, 0.0, zz), axis=-1, keepdims=True)
            ms = jnp.where(lo, s_lo, s_hi) * (1.0 / seg)
        outs.append(zg * lax.rsqrt(ms + NORM_EPS))
    return outs[0] if len(outs) == 1 else jnp.concatenate(outs, axis=-1)


def _rope(x, cos, sin, unit):
    outs = []
    for g in range(x.shape[-1] // LANES):
        xg = x[:, g * LANES:(g + 1) * LANES]
        if unit == LANES:
            rot = pltpu.roll(xg, LANES // 2, 1)
        else:
            lane = lax.broadcasted_iota(jnp.int32, xg.shape, 1)
            rot = jnp.where((lane & (unit // 2)) == 0,
                            pltpu.roll(xg, LANES - unit // 2, 1), pltpu.roll(xg, unit // 2, 1))
        outs.append(xg * cos + rot * sin)
    return outs[0] if len(outs) == 1 else jnp.concatenate(outs, axis=-1)


_C_S5 = 0
_C_DAQ = 512
_C_DAK = 1024
_C_DAV = 1536
_C_GQQ = 2048
_C_GQK = 2560
_C_GQV = 2816
_C_ML = 3072
_C_GATE = 3584
_C_END = 3584 + 4096

_G_DAQ, _G_DAK, _G_GQQ, _G_GQK = 0, 512, 1024, 1536
_G_CQ, _G_CKV, _G_QN, _G_QR, _G_KN, _G_KR, _G_END = 1792, 2048, 2176, 2688, 2944, 3456, 3584


def _stream_specs(stream, tm, lat_tiles):
    if isinstance(stream, tuple):
        lat, ctx = stream
        d = lat.shape[-1]
        return ([pl.BlockSpec((1, tm, d), lambda i, t: (i, jnp.minimum(t, lat_tiles - 1), 0)),
                 pl.BlockSpec((1, tm, d), lambda i, t: (i, 0, 0))], [lat, ctx])
    return [pl.BlockSpec((1, tm, stream.shape[-1]), lambda i, t: (i, t, 0))], [stream]


def _stream_tile(refs, lat_tiles):
    if len(refs) == 2:
        return jnp.where(pl.program_id(1) >= lat_tiles, refs[1][0], refs[0][0])
    return refs[0][0]


def _inproj_kernel(*refs, lat_tiles, n_stream):
    (mod_ref, nrm_ref, gains_ref, rope_ref, w_ref, wuq_ref, wukv_ref,
     u_ref, daq_ref, dak_ref, dav_ref, gqq_ref, gqk_ref, gqv_ref,
     mlq_ref, mlk_ref, mlv_ref, gate_ref, u_sc) = refs[n_stream:]
    x = _stream_tile(refs[:n_stream], lat_tiles)
    d = x.shape[-1]
    ms = jnp.mean(x * x, axis=-1, keepdims=True)
    mod = mod_ref[0]
    shift, scale = mod[:, 0:d], mod[:, d:2 * d]
    h = (x * lax.rsqrt(ms + NORM_EPS) * nrm_ref[...]) * (1.0 + scale) + shift
    hb = h.astype(BF16)

    def proj(c0, c1):
        return jnp.dot(hb, w_ref[:, c0:c1], preferred_element_type=F32)

    def gain(g0, g1):
        return gains_ref[:, g0:g1]

    cos64, sin64 = rope_ref[:, 0:128], rope_ref[:, 128:256]
    cos128, sin128 = rope_ref[:, 256:384], rope_ref[:, 384:512]

    zu = proj(_C_S5, _C_DAQ)
    per_tile = x.shape[0] // S5_CHUNK
    for j in range(u_sc.shape[0]):
        u_sc[j] = zu[:, j * LANES:(j + 1) * LANES]
        for s in range(S5_CHUNK):
            u_ref[0, s * per_tile:(s + 1) * per_tile, j * LANES:(j + 1) * LANES] = (
                u_sc[j, pl.ds(s, per_tile, stride=S5_CHUNK), :].astype(BF16))

    q = _seg_rms(proj(_C_DAQ, _C_DAK), DA_DK) * gain(_G_DAQ, _G_DAK)
    daq_ref[0] = _rope(q, cos64, sin64, DA_DK).astype(BF16)
    k = _seg_rms(proj(_C_DAK, _C_DAV), DA_DK) * gain(_G_DAK, _G_GQQ)
    dak_ref[0] = _rope(k, cos64, sin64, DA_DK).astype(BF16)
    dav_ref[0] = proj(_C_DAV, _C_GQQ).astype(BF16)

    q = _seg_rms(proj(_C_GQQ, _C_GQK), GQ_DH) * gain(_G_GQQ, _G_GQK)
    gqq_ref[0] = _rope(q, cos128, sin128, GQ_DH).astype(BF16)
    k = _seg_rms(proj(_C_GQK, _C_GQV), GQ_DH) * gain(_G_GQK, _G_CQ)
    gqk_ref[0] = _rope(k, cos128, sin128, GQ_DH).astype(BF16)
    gqv_ref[0] = proj(_C_GQV, _C_ML).astype(BF16)

    zc = proj(_C_ML, _C_GATE)
    cq = (_seg_rms(zc[:, 0:256], 256) * gain(_G_CQ, _G_CKV)).astype(BF16)
    ckv = (_seg_rms(zc[:, 256:384], 128) * gain(_G_CKV, _G_QN)).astype(BF16)
    qq = jnp.dot(cq, wuq_ref[...], preferred_element_type=F32)
    kv = jnp.dot(ckv, wukv_ref[...], preferred_element_type=F32)
    qn = (_seg_rms(qq[:, 0:512], ML_NOPE) * gain(_G_QN, _G_QR)).astype(BF16)
    qr = _seg_rms(qq[:, 512:768], ML_ROPE) * gain(_G_QR, _G_KN)
    qr = _rope(qr, cos64, sin64, ML_ROPE)
    kn = (_seg_rms(kv[:, 0:512], ML_NOPE) * gain(_G_KN, _G_KR)).astype(BF16)
    kr = _seg_rms(zc[:, 384:512], ML_ROPE) * gain(_G_KR, _G_END)
    kr = _rope(kr, cos64, sin64, ML_ROPE).astype(BF16)
    lo = lax.broadcasted_iota(jnp.int32, (x.shape[0], LANES), 1) < ML_ROPE
    for hd in range(ML_HEADS):
        pair = qr[:, (hd // 2) * LANES:(hd // 2 + 1) * LANES]
        keep = lo if hd % 2 == 0 else jnp.logical_not(lo)
        mlq_ref[0, :, hd * 256:hd * 256 + 128] = qn[:, hd * 128:(hd + 1) * 128]
        mlq_ref[0, :, hd * 256 + 128:(hd + 1) * 256] = jnp.where(keep, pair, 0.0).astype(BF16)
        mlk_ref[0, :, hd * 256:hd * 256 + 128] = kn[:, hd * 128:(hd + 1) * 128]
        mlk_ref[0, :, hd * 256 + 128:(hd + 1) * 256] = kr
    mlv_ref[0] = kv[:, 512:1024].astype(BF16)

    for j in range(N_BRANCH):
        c0 = _C_GATE + j * d
        gate_ref[0, :, j * d:(j + 1) * d] = jax.nn.sigmoid(proj(c0, c0 + d)).astype(BF16)


def _inproj(stream, mod_rows, layer, n_batch, n_lat, lt, nrm, gains, rope, w, wuq, wukv):
    tm = ROW_TILE
    nt = lt // tm
    lat_tiles = n_lat // tm
    d = w.shape[0]
    x_specs, x_args = _stream_specs(stream, tm, lat_tiles)

    def mod_idx(i, t):
        return (layer * 8 + jnp.where(t >= lat_tiles, n_batch, i), 0, 0)

    widths = [512, 512, 512, 512, 512, 256, 256, 1024, 1024, 512, N_BRANCH * d]
    return pl.pallas_call(
        functools.partial(_inproj_kernel, lat_tiles=lat_tiles, n_stream=len(x_args)),
        grid=(n_batch, nt),
        in_specs=x_specs + [pl.BlockSpec((1, 1, mod_rows.shape[-1]), mod_idx),
                            _const_spec(nrm.shape), _const_spec(gains.shape),
                            pl.BlockSpec((tm, 512), lambda i, t: (t, 0)),
                            _const_spec(w.shape), _const_spec(wuq.shape), _const_spec(wukv.shape)],
        out_specs=[pl.BlockSpec((1, tm, wd), lambda i, t: (i, t, 0)) for wd in widths],
        out_shape=[jax.ShapeDtypeStruct((n_batch, lt, wd), BF16) for wd in widths],
        scratch_shapes=[pltpu.VMEM((widths[0] // LANES, tm, LANES), F32)],
        compiler_params=_cparams(("parallel", "parallel")),
        name="inproj",
    )(*x_args, mod_rows, nrm, gains, rope, w, wuq, wukv)


def _shift(n):
    assert n & (n - 1) == 0
    return n.bit_length() - 1


def _s5_expand(mg_ref, wsg_ref, wog_ref, m_sc, ws_sc, wo_sc):
    n_rows, cw = mg_ref.shape[1], mg_ref.shape[2]
    n_c, gt = S5_GROUP, n_rows // cw
    n_sp = wsg_ref.shape[2]
    n_p = n_sp // 4
    n_state = gt * n_sp

    def iota2(shape):
        return lax.broadcasted_iota(jnp.int32, shape, 0), lax.broadcasted_iota(jnp.int32, shape, 1)

    def div(i, n):
        return lax.shift_right_logical(i, _shift(n))

    def mod(i, n):
        return i & (n - 1)

    def onehot(cond):
        return jnp.where(cond, 1.0, 0.0).astype(BF16)

    def lane_group(i):
        return mod(div(i, n_c), gt)

    def state_group(i):
        return mod(div(i, n_p), gt)

    r, q = iota2((n_rows, n_rows))
    regroup = onehot(q == lane_group(r) * cw + div(r, gt * n_c) * n_c + mod(r, n_c))
    same_lane_group = lane_group(r) == lane_group(q)

    k, q = iota2((cw, n_rows))
    spread = onehot((div(q, gt * n_c) == div(k, n_c)) & (mod(q, n_c) == mod(k, n_c)))
    t_m = jnp.dot(regroup, mg_ref[0], preferred_element_type=F32).astype(BF16)
    m_sc[...] = jnp.where(same_lane_group, jnp.dot(t_m, spread, preferred_element_type=F32), 0.0).astype(BF16)

    k, q = iota2((n_sp, n_state))
    spread_s = onehot((div(q, gt * n_p) == div(k, n_p)) & (mod(q, n_p) == mod(k, n_p)))
    r, q = iota2((n_rows, n_state))
    t_s = jnp.dot(regroup, wsg_ref[0], preferred_element_type=F32).astype(BF16)
    ws_sc[...] = jnp.where(lane_group(r) == state_group(q),
                           jnp.dot(t_s, spread_s, preferred_element_type=F32), 0.0).astype(BF16)

    r, k = iota2((n_state, n_sp))
    gather_s = onehot((div(r, gt * n_p) == div(k, n_p)) & (mod(r, n_p) == mod(k, n_p)))
    t_o = lax.dot_general(wog_ref[0], regroup, (((1,), (1,)), ((), ())),
                          preferred_element_type=F32).astype(BF16)
    r, q = iota2((n_state, n_rows))
    wo_sc[...] = jnp.where(state_group(r) == lane_group(q),
                           jnp.dot(gather_s, t_o, preferred_element_type=F32), 0.0).astype(BF16)


def _s5_kernel(u_ref, mg_ref, wsg_ref, wog_ref, are_ref, aim_ref, y_ref, s_sc, m_sc, ws_sc, wo_sc,
               *, n_chunks, n_lat_chunks):
    @pl.when(pl.program_id(1) == 0)
    def _():
        _s5_expand(mg_ref, wsg_ref, wog_ref, m_sc, ws_sc, wo_sc)

    t_len = u_ref.shape[2]
    quarter = s_sc.shape[1] // 4
    a = jnp.concatenate([u_ref[0, :, s].reshape(n_chunks, LANES) for s in range(t_len)], axis=-1)
    s_sc[...] = jnp.dot(a, ws_sc[...], preferred_element_type=F32)
    af_re, ab_re = are_ref[0][:, 0:quarter], are_ref[0][:, quarter:2 * quarter]
    af_im, ab_im = aim_ref[0][:, 0:quarter], aim_ref[0][:, quarter:2 * quarter]
    n_ctx_chunks = n_chunks - n_lat_chunks

    def step(i, carry):
        hf_re, hf_im, hb_re, hb_im = carry
        rf = pl.ds(jnp.where(i < n_ctx_chunks, n_lat_chunks + i, i - n_ctx_chunks), 1)
        rb = pl.ds(n_chunks - 1 - i, 1)
        sf_re, sf_im = s_sc[rf, 0:quarter], s_sc[rf, quarter:2 * quarter]
        sb_re, sb_im = s_sc[rb, 2 * quarter:3 * quarter], s_sc[rb, 3 * quarter:4 * quarter]
        s_sc[rf, 0:quarter] = hf_re
        s_sc[rf, quarter:2 * quarter] = hf_im
        s_sc[rb, 2 * quarter:3 * quarter] = hb_re
        s_sc[rb, 3 * quarter:4 * quarter] = hb_im
        return (af_re * hf_re - af_im * hf_im + sf_re, af_re * hf_im + af_im * hf_re + sf_im,
                ab_re * hb_re - ab_im * hb_im + sb_re, ab_re * hb_im + ab_im * hb_re + sb_im)

    zero = jnp.zeros((1, quarter), F32)
    lax.fori_loop(0, n_chunks, step, (zero, zero, zero, zero))

    y = (jnp.dot(a, m_sc[...], preferred_element_type=F32)
         + jnp.dot(s_sc[...].astype(BF16), wo_sc[...], preferred_element_type=F32))
    for t in range(t_len):
        y_ref[0, 0, :, t] = y[:, t * LANES:(t + 1) * LANES].reshape(y_ref.shape[2], y_ref.shape[4], LANES)


def _s5_prep(lam_re, lam_im, log_step, b_re, b_im, c_re, c_im, d_skip):
    hp = lax.Precision.HIGHEST
    t_len, n_g, n_p, n_c = S5_CHUNK, lam_re.shape[1], S5_STATE, S5_GROUP
    lam_re, lam_im = lam_re.astype(F32), lam_im.astype(F32)
    dt = jnp.exp(log_step.astype(F32))[..., None]
    mag = jnp.exp(lam_re * dt)
    ab_re, ab_im = mag * jnp.cos(lam_im * dt), mag * jnp.sin(lam_im * dt)
    den = lam_re * lam_re + lam_im * lam_im
    nr, ni = ab_re - 1.0, ab_im
    coef_re = (nr * lam_re + ni * lam_im) / den
    coef_im = (ni * lam_re - nr * lam_im) / den
    b_re, b_im = b_re.astype(F32), b_im.astype(F32)
    bb_re = coef_re[..., None] * b_re - coef_im[..., None] * b_im
    bb_im = coef_re[..., None] * b_im + coef_im[..., None] * b_re
    kk = jnp.arange(t_len + 1, dtype=F32)[:, None, None, None]
    pmag = jnp.exp(lam_re * dt * kk)
    pw_re, pw_im = pmag * jnp.cos(lam_im * dt * kk), pmag * jnp.sin(lam_im * dt * kk)
    c_re, c_im = c_re.astype(F32), c_im.astype(F32)
    cp_re = c_re[None] * pw_re[:, :, :, None, :] - c_im[None] * pw_im[:, :, :, None, :]
    cp_im = c_re[None] * pw_im[:, :, :, None, :] + c_im[None] * pw_re[:, :, :, None, :]
    kern = (jnp.einsum('tdgop,dgpc->tdgoc', cp_re, bb_re, precision=hp)
            - jnp.einsum('tdgop,dgpc->tdgoc', cp_im, bb_im, precision=hp))
    s_idx = np.arange(t_len)[:, None]
    t_idx = np.arange(t_len)[None, :]
    lag_f = np.clip(t_idx - s_idx, 0, t_len)
    lag_b = np.clip(s_idx - t_idx, 0, t_len)
    mf = kern[lag_f, 0] * jnp.asarray(t_idx >= s_idx, F32)[:, :, None, None, None]
    mb = kern[lag_b, 1] * jnp.asarray(s_idx >= t_idx, F32)[:, :, None, None, None]
    m_full = jnp.transpose(mf + mb, (2, 0, 4, 1, 3))
    eye_t = jnp.eye(t_len, dtype=F32)[None, :, None, :, None]
    eye_c = jnp.eye(n_c, dtype=F32)[None, None, :, None, :]
    m_full = m_full + eye_t * eye_c * d_skip.astype(F32).reshape(n_g, 1, n_c, 1, 1)

    def bpow(pw_r, pw_i, direction):
        re = pw_r[..., None] * bb_re[direction][None] - pw_i[..., None] * bb_im[direction][None]
        im = pw_r[..., None] * bb_im[direction][None] + pw_i[..., None] * bb_re[direction][None]
        return jnp.transpose(re, (1, 0, 3, 2)), jnp.transpose(im, (1, 0, 3, 2))

    rev = np.arange(t_len - 1, -1, -1)
    fwd = np.arange(t_len)
    f_re, f_im = bpow(pw_re[rev, 0], pw_im[rev, 0], 0)
    g_re, g_im = bpow(pw_re[fwd, 1], pw_im[fwd, 1], 1)

    def cpow(idx, direction):
        return (jnp.transpose(cp_re[idx, direction], (1, 3, 0, 2)),
                jnp.transpose(cp_im[idx, direction], (1, 3, 0, 2)))

    of_re, of_im = cpow(np.arange(1, t_len + 1), 0)
    ob_re, ob_im = cpow(np.arange(t_len, 0, -1), 1)

    gt = S5_TILE_GROUPS
    n_j = n_g // gt
    cw = t_len * n_c
    mg = m_full.reshape(n_j, gt * cw, cw)
    wsg = jnp.stack([f_re, f_im, g_re, g_im], axis=3).reshape(n_j, gt * cw, 4 * n_p)
    wog = jnp.stack([of_re, -of_im, ob_re, -ob_im], axis=1).reshape(n_j, gt, 4 * n_p, cw)
    wog = jnp.transpose(wog, (0, 2, 1, 3)).reshape(n_j, 4 * n_p, gt * cw)
    a_re = jnp.concatenate([pw_re[t_len, 0].reshape(n_j, 1, gt * n_p),
                            pw_re[t_len, 1].reshape(n_j, 1, gt * n_p)], axis=-1)
    a_im = jnp.concatenate([pw_im[t_len, 0].reshape(n_j, 1, gt * n_p),
                            pw_im[t_len, 1].reshape(n_j, 1, gt * n_p)], axis=-1)
    return mg.astype(BF16), wsg.astype(BF16), wog.astype(BF16), a_re, a_im


def _s5(u, n_lat, ops):
    mg, wsg, wog, a_re, a_im = ops
    b, lt, width = u.shape
    n_tiles = lt // ROW_TILE
    per_tile = ROW_TILE // S5_CHUNK
    n_chunks = n_tiles * per_tile
    n_rows = mg.shape[1]
    n_state = S5_TILE_GROUPS * wsg.shape[2]
    blk = (1, n_tiles, S5_CHUNK, per_tile, LANES)
    tile = lambda j, i: (i, 0, 0, 0, j)
    wsel = lambda j, i: (j, 0, 0)
    y = pl.pallas_call(
        functools.partial(_s5_kernel, n_chunks=n_chunks, n_lat_chunks=n_lat // S5_CHUNK),
        grid=(width // LANES, b),
        in_specs=[pl.BlockSpec(blk, tile),
                  pl.BlockSpec((1,) + mg.shape[1:], wsel), pl.BlockSpec((1,) + wsg.shape[1:], wsel),
                  pl.BlockSpec((1,) + wog.shape[1:], wsel),
                  pl.BlockSpec((1, 1, n_state // 2), wsel), pl.BlockSpec((1, 1, n_state // 2), wsel)],
        out_specs=pl.BlockSpec((1,) + blk, lambda j, i: (j, i, 0, 0, 0, 0)),
        out_shape=jax.ShapeDtypeStruct((width // LANES, b, n_tiles, S5_CHUNK, per_tile, LANES), F32),
        scratch_shapes=[pltpu.VMEM((n_chunks, n_state), F32), pltpu.VMEM((n_rows, n_rows), BF16),
                        pltpu.VMEM((n_rows, n_state), BF16), pltpu.VMEM((n_state, n_rows), BF16)],
        compiler_params=_cparams(("arbitrary", "arbitrary")),
        name="s5_scan",
    )(u.reshape(b, n_tiles, S5_CHUNK, per_tile, width), mg, wsg, wog, a_re, a_im)
    return y.reshape(width // LANES, b, lt, LANES)


ATTN_ROWS = 1024
ATTN_ROW_BLOCKS = 4
ATTN_KEY_TILE = 256


def _attn_kernel(*refs, mode, tq, tk, n_lat, n_extra, lam_init):
    q_ref, k_ref, v_ref = refs[:3]
    extra = refs[3:3 + n_extra]
    o_ref, acc_sc, sa_sc, sb_sc = refs[-4:]
    n_keys = k_ref.shape[1]
    n_ctx = n_keys - n_lat

    q = q_ref[0]
    if mode == "da":
        lo = lax.broadcasted_iota(jnp.int32, q.shape, 1) < DA_DK
        zero = jnp.zeros_like(q)
        qs = jnp.concatenate([jnp.where(lo, q, zero), jnp.where(lo, zero, q)], axis=0)
    elif mode == "gq":
        qs = jnp.concatenate([q[:, 0:GQ_DH], q[:, GQ_DH:2 * GQ_DH]], axis=0)
    else:
        qs = q
    n_rb = ATTN_ROW_BLOCKS
    rb = qs.shape[0] // n_rb
    q_blocks = [qs[r * rb:(r + 1) * rb] for r in range(n_rb)]

    def qk(r, start, size):
        kc = k_ref[0, pl.ds(start, size), :]
        s = lax.dot_general(q_blocks[r], kc, (((1,), (1,)), ((), ())), preferred_element_type=F32)
        return s, jnp.max(s, axis=-1, keepdims=True)

    def softmax_pv(r, s_buf, mx, m_old, start, size):
        m_new = jnp.maximum(m_old, mx)
        alpha = jnp.exp2(m_old - m_new)
        rows = pl.ds(r * rb, rb)
        kt = min(ATTN_KEY_TILE, size)
        pv, lsum = None, None
        for c0 in range(0, size, kt):
            p = jnp.exp2((s_buf[:, c0:c0 + kt] - m_new).astype(BF16))
            p32 = p.astype(F32)
            for i in range(kt // LANES):
                t = p32[:, i * LANES:(i + 1) * LANES]
                lsum = t if lsum is None else lsum + t
            t = jnp.dot(p, v_ref[0, pl.ds(start + c0, kt), :], preferred_element_type=F32)
            pv = t if pv is None else pv + t
        acc_sc[rows, 0:LANES] = alpha * acc_sc[rows, 0:LANES] + pv
        acc_sc[rows, LANES:2 * LANES] = alpha * acc_sc[rows, LANES:2 * LANES] + lsum
        return m_new

    def stage(cur_buf, nxt_buf, nxt_start, nxt_size, cur_start, cur_size, ms, mxs):
        new_ms, new_mxs = [], []
        for r in range(n_rb):
            rows = pl.ds(r * rb, rb)
            s_n, mx_n = qk(r, nxt_start, nxt_size)
            nxt_buf[rows, 0:nxt_size] = s_n
            new_mxs.append(mx_n)
            new_ms.append(softmax_pv(r, cur_buf.at[rows, :], mxs[r], ms[r], cur_start, cur_size))
        return new_ms, new_mxs

    acc_sc[...] = jnp.zeros_like(acc_sc)
    chunks = [(j * tk, tk) for j in range(n_lat // tk)] + [(n_lat, n_ctx)]
    bufs = (sa_sc, sb_sc)
    ms = [jnp.full((rb, 1), -1e30, F32) for _ in range(n_rb)]
    mxs = []
    for r in range(n_rb):
        s, mx = qk(r, *chunks[0])
        sa_sc[pl.ds(r * rb, rb), 0:chunks[0][1]] = s
        mxs.append(mx)
    for j in range(len(chunks) - 1):
        ms, mxs = stage(bufs[j % 2], bufs[(j + 1) % 2], *chunks[j + 1], *chunks[j], ms, mxs)
    last = len(chunks) - 1
    for r in range(n_rb):
        softmax_pv(r, bufs[last % 2].at[pl.ds(r * rb, rb), :], mxs[r], ms[r], *chunks[last])
    acc = acc_sc[...]
    o = acc[:, 0:LANES] / jnp.sum(acc[:, LANES:2 * LANES], axis=-1, keepdims=True)

    if mode == "da":
        lam_ref, sub_ref = extra
        lam = lam_ref[...]
        lam_full = (jnp.exp(jnp.sum(lam[0:1] * lam[1:2], axis=-1, keepdims=True))
                    - jnp.exp(jnp.sum(lam[2:3] * lam[3:4], axis=-1, keepdims=True)) + lam_init)
        dlt = o[0:tq] - lam_full * o[tq:2 * tq]
        ms_d = jnp.mean(dlt * dlt, axis=-1, keepdims=True)
        o_ref[0] = (dlt * lax.rsqrt(ms_d + NORM_EPS) * sub_ref[...] * (1.0 - lam_init)).astype(o_ref.dtype)
    elif mode == "gq":
        o_ref[0] = jnp.concatenate([o[0:tq], o[tq:2 * tq]], axis=-1).astype(o_ref.dtype)
    else:
        o_ref[0] = o.astype(o_ref.dtype)


def _pick_tk(n_lat):
    for tk in (1024, 512, 256):
        if n_lat % tk == 0:
            return tk
    raise ValueError("latent length must be a multiple of 256")


def _attention(mode, q, k, v, n_lat, with_ctx, extra=(), lam_init=0.0):
    b, lt, _ = q.shape
    n_ctx = lt - n_lat
    if mode == "da":
        heads, qw, kw, ow, g = DA_HEADS, 128, 128, 128, 2
    elif mode == "gq":
        heads, qw, kw, ow, g = GQ_KV, 256, 128, 256, 2
    else:
        heads, qw, kw, ow, g = ML_HEADS, 256, 256, 128, 1
    tk = _pick_tk(n_lat)

    def call(ctx_only):
        if ctx_only:
            tq, q0, n_tiles, kv_rows, kv_blk = n_ctx, n_lat // n_ctx, 1, n_ctx, n_lat // n_ctx
        else:
            tq = ATTN_ROWS // g if n_lat % (ATTN_ROWS // g) == 0 else ROW_TILE
            q0, n_tiles, kv_rows, kv_blk = 0, n_lat // tq, lt, 0
        in_specs = [pl.BlockSpec((1, tq, qw), lambda i, h, t: (i, t + q0, h)),
                    pl.BlockSpec((1, kv_rows, kw), lambda i, h, t: (i, kv_blk, h)),
                    pl.BlockSpec((1, kv_rows, 128), lambda i, h, t: (i, kv_blk, h))]
        in_specs += [_const_spec(e.shape) for e in extra]
        return pl.pallas_call(
            functools.partial(_attn_kernel, mode=mode, tq=tq, tk=tk, n_lat=0 if ctx_only else n_lat,
                              n_extra=len(extra), lam_init=lam_init),
            grid=(b, heads, n_tiles),
            in_specs=in_specs,
            out_specs=pl.BlockSpec((1, tq, ow), lambda i, h, t: (i, t, h)),
            out_shape=jax.ShapeDtypeStruct((b, n_tiles * tq, BRANCH_W), BF16),
            scratch_shapes=[pltpu.VMEM((g * tq, 2 * LANES), F32),
                            pltpu.VMEM((g * tq, max(tk, n_ctx)), F32),
                            pltpu.VMEM((g * tq, max(tk, n_ctx)), F32)],
            compiler_params=_cparams(("parallel", "parallel", "arbitrary")),
            name="attn_" + mode + ("_ctx" if ctx_only else ""),
        )(q, k, v, *extra)

    return call(False), (call(True) if with_ctx else None)


def _gelu_tanh(x):
    return 0.5 * x * (1.0 + jnp.tanh(math.sqrt(2.0 / math.pi) * (x + 0.044715 * (x * x * x))))


def _merge_ffn_kernel(*refs, lat_tiles, with_ctx, n_stream, ff_chunk):
    x = _stream_tile(refs[:n_stream], lat_tiles)
    refs = refs[n_stream:]
    (mod_ref, y_ref, gate_ref, wglu_ref, bglu_ref, wbr_ref, wout_ref,
     nrm_ref, win_ref, wffn_ref) = refs[:10]
    lat_refs = refs[10:13]
    ctx_refs = refs[13:16] if with_ctx else None
    o_ref, y_sc = refs[-2:]
    d = x.shape[-1]
    mod = mod_ref[0]
    per_tile = y_sc.shape[0] // S5_CHUNK
    for j in range(y_ref.shape[0]):
        for n in range(per_tile):
            y_sc[n * S5_CHUNK:(n + 1) * S5_CHUNK, j * LANES:(j + 1) * LANES] = (
                y_ref[j, 0, pl.ds(n, S5_CHUNK, stride=per_tile), :])
    g = _gelu_tanh(y_sc[...])
    glu = jax.nn.sigmoid(jnp.dot(g.astype(BF16), wglu_ref[...], preferred_element_type=F32)
                         + bglu_ref[...])
    branches = [(g * glu).astype(BF16)]
    for n in range(3):
        if with_ctx:
            is_ctx = pl.program_id(1) >= lat_tiles
            branches.append(jnp.where(is_ctx, ctx_refs[n][0], lat_refs[n][0]))
        else:
            branches.append(lat_refs[n][0])
    acc = None
    for n, br in enumerate(branches):
        term = gate_ref[0, :, n * d:(n + 1) * d].astype(F32) * jnp.dot(
            br, wbr_ref[n], preferred_element_type=F32)
        acc = term if acc is None else acc + term
    out = jnp.dot(acc.astype(BF16), wout_ref[...], preferred_element_type=F32)
    x = x + mod[:, 2 * d:3 * d] * out

    d_ff = wffn_ref.shape[0]
    shift, scale, g2 = mod[:, 3 * d:4 * d], mod[:, 4 * d:5 * d], mod[:, 5 * d:6 * d]
    ms = jnp.mean(x * x, axis=-1, keepdims=True)
    hb = ((x * lax.rsqrt(ms + NORM_EPS) * nrm_ref[...]) * (1.0 + scale) + shift).astype(BF16)
    out = None
    for c0 in range(0, d_ff, ff_chunk):
        gate = jnp.dot(hb, win_ref[:, c0:c0 + ff_chunk], preferred_element_type=F32)
        up = jnp.dot(hb, win_ref[:, d_ff + c0:d_ff + c0 + ff_chunk], preferred_element_type=F32)
        act = (gate * jax.nn.sigmoid(gate) * up).astype(BF16)
        part = jnp.dot(act, wffn_ref[c0:c0 + ff_chunk, :], preferred_element_type=F32)
        out = part if out is None else out + part
    o_ref[0] = x + g2 * out


def _mod_index(layer, n_batch, lat_tiles):
    def idx(i, t):
        return (layer * 8 + jnp.where(t >= lat_tiles, n_batch, i), 0, 0)
    return idx


def _merge_ffn(stream, mod_rows, layer, n_lat, n_rows, y, attn, gates, wglu, bglu, wbr, wout,
               nrm_ffn, win, wffn):
    b, d = y.shape[1], wout.shape[0]
    tm = ROW_TILE
    lat_tiles = n_lat // tm
    with_ctx = n_rows > n_lat
    d_ff = wffn.shape[0]
    ff_chunk = d_ff // 2 if (d_ff // 2) % LANES == 0 else d_ff
    row = lambda i, t: (i, t, 0)
    lat_row = lambda i, t: (i, jnp.minimum(t, lat_tiles - 1), 0)
    x_specs, x_args = _stream_specs(stream, tm, lat_tiles)
    weights = [wglu, bglu, wbr, wout, nrm_ffn, win, wffn]
    in_specs = x_specs + [
        pl.BlockSpec((1, 1, mod_rows.shape[-1]), _mod_index(layer, b, lat_tiles)),
        pl.BlockSpec((BRANCH_W // LANES, 1, tm, LANES), lambda i, t: (0, i, t, 0)),
        pl.BlockSpec((1, tm, N_BRANCH * d), row)] + [_const_spec(w.shape) for w in weights]
    in_specs += [pl.BlockSpec((1, tm, BRANCH_W), lat_row)] * 3
    args = x_args + [mod_rows, y, gates] + weights + [a[0] for a in attn]
    if with_ctx:
        in_specs += [pl.BlockSpec((1, tm, BRANCH_W), lambda i, t: (i, 0, 0))] * 3
        args += [a[1] for a in attn]
    return pl.pallas_call(
        functools.partial(_merge_ffn_kernel, lat_tiles=lat_tiles, with_ctx=with_ctx,
                          n_stream=len(x_args), ff_chunk=ff_chunk),
        grid=(b, n_rows // tm),
        in_specs=in_specs,
        out_specs=pl.BlockSpec((1, tm, d), row),
        out_shape=jax.ShapeDtypeStruct((b, n_rows, d), F32),
        scratch_shapes=[pltpu.VMEM((tm, BRANCH_W), F32)],
        compiler_params=_cparams(("parallel", "parallel")),
        name="merge_ffn",
    )(*args)


def _inproj_columns(d):
    de64, de128 = _deinterleave(64), _deinterleave(128)
    off_da = 512
    off_gq = off_da + 3 * DA_HEADS * DA_DV
    off_ml = off_gq + (GQ_HEADS + 2 * GQ_KV) * GQ_DH
    off_gate = off_ml + ML_QRANK + ML_KVRANK + ML_ROPE
    cols = [np.arange(512)]
    for part in range(2):
        for hd in range(DA_HEADS):
            for comp in range(2):
                cols.append(off_da + part * 512 + hd * 128 + comp * 64 + de64)
    cols.append(off_da + 1024 + np.arange(512))
    for hd in range(GQ_HEADS):
        cols.append(off_gq + hd * 128 + de128)
    for hd in range(GQ_KV):
        cols.append(off_gq + 512 + hd * 128 + de128)
    cols.append(off_gq + 768 + np.arange(256))
    cols.append(off_ml + np.arange(ML_QRANK + ML_KVRANK))
    cols.append(off_ml + ML_QRANK + ML_KVRANK + de64)
    cols.append(off_ml + ML_QRANK + ML_KVRANK + de64)
    cols.append(off_gate + np.arange(N_BRANCH * d))
    return np.concatenate(cols)


def _rope_table(n_lat, n_ctx):
    t = np.arange(n_lat)
    r = (t // GRID_W).astype(np.float32)
    col = (t % GRID_W).astype(np.float32)

    def angles(rot_dim):
        half = rot_dim // 2
        inv = jnp.asarray(ROPE_THETA, F32) ** (-jnp.arange(0, half, 2, dtype=F32) / half)
        return jnp.concatenate([jnp.asarray(r)[:, None] * inv, jnp.asarray(col)[:, None] * inv], axis=-1)

    a64, a128 = angles(64), angles(128)
    c64, s64 = jnp.cos(a64), jnp.sin(a64)
    c128, s128 = jnp.cos(a128), jnp.sin(a128)
    lat = jnp.concatenate([c64, c64, c64, c64, -s64, s64, -s64, s64, c128, c128, -s128, s128], axis=-1)
    ctx = jnp.concatenate([jnp.ones((n_ctx, 128), F32), jnp.zeros((n_ctx, 128), F32),
                           jnp.ones((n_ctx, 128), F32), jnp.zeros((n_ctx, 128), F32)], axis=-1)
    return jnp.concatenate([lat, ctx], axis=0)


def kernel(x, c, ctx, c_ctx, w_mod, b_mod, norm_mix, w_in, s5_lam_re, s5_lam_im, s5_log_step, s5_b_re, s5_b_im, s5_c_re, s5_c_im, s5_d, s5_w_glu, s5_b_glu, da_q_norm, da_k_norm, da_lam, da_subln, gq_q_norm, gq_k_norm, ml_cq_norm, ml_ckv_norm, ml_w_uq, ml_w_ukv, ml_q_norm, ml_k_norm, w_branch, w_out, norm_ffn, w_ffn_in, w_ffn_out):
    b, n_lat, d = x.shape
    n_ctx = ctx.shape[1]
    depth = w_mod.shape[0]
    assert n_ctx == ROW_TILE and n_lat % ROW_TILE == 0 and b < 8 and d == 1024

    cc = jnp.zeros((8, d), F32).at[:b].set(c).at[b].set(c_ctx)
    mod_rows = _modulation(cc, w_mod, b_mod).reshape(depth * 8, 1, 6 * d)
    rope = _rope_table(n_lat, n_ctx)
    xc = (x, ctx)

    de64, de128 = _deinterleave(64), _deinterleave(128)
    cols = _inproj_columns(d)
    uq_cols = np.concatenate([hd * 192 + np.arange(128) for hd in range(ML_HEADS)]
                             + [hd * 192 + 128 + de64 for hd in range(ML_HEADS)])
    ukv_cols = np.concatenate([hd * 256 + np.arange(128) for hd in range(ML_HEADS)]
                              + [hd * 256 + 128 + np.arange(128) for hd in range(ML_HEADS)])
    log2e = math.log2(math.e)
    da_scale, gq_scale = DA_DK ** -0.5 * log2e, GQ_DH ** -0.5 * log2e
    ml_scale = (ML_NOPE + ML_ROPE) ** -0.5 * log2e

    def tile_l(v, n):
        return jnp.tile(v, (1, n))

    gains_all = jnp.concatenate([
        tile_l(da_q_norm[:, de64], 8) * da_scale, tile_l(da_k_norm[:, de64], 8),
        tile_l(gq_q_norm[:, de128], 4) * gq_scale, tile_l(gq_k_norm[:, de128], 2),
        ml_cq_norm, ml_ckv_norm,
        tile_l(ml_q_norm[:, :ML_NOPE], 4) * ml_scale,
        tile_l(ml_q_norm[:, ML_NOPE:][:, de64], 4) * ml_scale,
        tile_l(ml_k_norm[:, :ML_NOPE], 4), tile_l(ml_k_norm[:, ML_NOPE:][:, de64], 2),
    ], axis=1).astype(F32)
    s5_ops_all = jax.vmap(_s5_prep)(s5_lam_re, s5_lam_im, s5_log_step, s5_b_re, s5_b_im,
                                    s5_c_re, s5_c_im, s5_d)

    for i in range(depth):
        last = i == depth - 1
        lam_init = 0.8 - 0.6 * math.exp(-0.3 * i)
        w = w_in[i][:, cols].astype(BF16)
        wuq = ml_w_uq[i][:, uq_cols].astype(BF16)
        wukv = ml_w_ukv[i][:, ukv_cols].astype(BF16)

        (u, daq, dak, dav, gqq, gqk, gqv, mlq, mlk, mlv, gates) = _inproj(
            xc, mod_rows, i, b, n_lat, n_lat + n_ctx, norm_mix[i][None, :], gains_all[i][None, :],
            rope, w, wuq, wukv)

        y = _s5(u, n_lat, tuple(op[i] for op in s5_ops_all))

        n_rows = n_lat if last else n_lat + n_ctx
        da = _attention("da", daq, dak, dav, n_lat, not last,
                        extra=(da_lam[i].astype(F32), da_subln[i].astype(F32)[None, :]),
                        lam_init=lam_init)
        gq = _attention("gq", gqq, gqk, gqv, n_lat, not last)
        ml = _attention("ml", mlq, mlk, mlv, n_lat, not last)

        xc = _merge_ffn(xc, mod_rows, i, n_lat, n_rows, y, (da, gq, ml), gates,
                        s5_w_glu[i].astype(BF16), s5_b_glu[i].astype(F32)[None, :],
                        w_branch[i].astype(BF16), w_out[i].astype(BF16),
                        norm_ffn[i][None, :], w_ffn_in[i].astype(BF16), w_ffn_out[i].astype(BF16))
    return xc
```

```python
import functools
import math

import numpy as np
import jax
import jax.numpy as jnp
from jax import lax
from jax.experimental import pallas as pl
from jax.experimental.pallas import tpu as pltpu

GRID_W = 64
ROPE_THETA = 10000.0
NORM_EPS = 1e-6

S5_GROUP = 16
S5_STATE = 64
S5_CHUNK = 8

DA_HEADS = 4
DA_DK = 64
DA_DV = 128
GQ_HEADS = 4
GQ_KV = 2
GQ_DH = 128
ML_HEADS = 4
ML_QRANK = 256
ML_KVRANK = 128
ML_NOPE = 128
ML_ROPE = 64
ML_DV = 128
N_BRANCH = 4
BRANCH_W = 512

LANES = 128
S5_TILE_GROUPS = LANES // S5_GROUP
ROW_TILE = 256
VMEM_LIMIT = 56 * 1024 * 1024

F32 = jnp.float32
BF16 = jnp.bfloat16


def _deinterleave(n):
    return np.concatenate([np.arange(0, n, 2), np.arange(1, n, 2)])


def _cparams(sem):
    return pltpu.CompilerParams(dimension_semantics=sem, vmem_limit_bytes=VMEM_LIMIT)


def _const_spec(shape):
    nd = len(shape)
    return pl.BlockSpec(shape, lambda *_: (0,) * nd)


def _mod_kernel(cc_ref, w_ref, b_ref, o_ref):
    a = cc_ref[...]
    a = a * jax.nn.sigmoid(a)
    o_ref[0] = jnp.dot(a, w_ref[0], preferred_element_type=F32,
                       precision=lax.Precision.HIGHEST) + b_ref[0]


def _modulation(cc, w_mod, b_mod):
    depth, d, n = w_mod.shape
    tn = n // 4
    return pl.pallas_call(
        _mod_kernel,
        grid=(depth, n // tn),
        in_specs=[pl.BlockSpec((8, d), lambda i, j: (0, 0)),
                  pl.BlockSpec((1, d, tn), lambda i, j: (i, 0, j)),
                  pl.BlockSpec((1, 1, tn), lambda i, j: (i, 0, j))],
        out_specs=pl.BlockSpec((1, 8, tn), lambda i, j: (i, 0, j)),
        out_shape=jax.ShapeDtypeStruct((depth, 8, n), F32),
        compiler_params=_cparams(("arbitrary", "arbitrary")),
        name="modulation",
    )(cc, w_mod, b_mod.reshape(depth, 1, n))


def _seg_rms(z, seg):
    width = z.shape[-1]
    if seg == 2 * LANES:
        outs = []
        for g in range(width // seg):
            zg = z[:, g * seg:(g + 1) * seg]
            ms = jnp.sum(zg * zg, axis=-1, keepdims=True) * (1.0 / seg)
            outs.append(zg * lax.rsqrt(ms + NORM_EPS))
        return outs[0] if len(outs) == 1 else jnp.concatenate(outs, axis=-1)
    outs = []
    for g in range(width // LANES):
        zg = z[:, g * LANES:(g + 1) * LANES]
        zz = zg * zg
        if seg == LANES:
            ms = jnp.sum(zz, axis=-1, keepdims=True) * (1.0 / seg)
        else:
            lo = lax.broadcasted_iota(jnp.int32, zz.shape, 1) < seg
            s_lo = jnp.sum(jnp.where(lo, zz, 0.0), axis=-1, keepdims=True)
            s_hi = jnp.sum(jnp.where(lo, 0.0, zz), axis=-1, keepdims=True)
            ms = jnp.where(lo, s_lo, s_hi) * (1.0 / seg)
        outs.append(zg * lax.rsqrt(ms + NORM_EPS))
    return outs[0] if len(outs) == 1 else jnp.concatenate(outs, axis=-1)


def _rope(x, cos, sin, unit):
    outs = []
    for g in range(x.shape[-1] // LANES):
        xg = x[:, g * LANES:(g + 1) * LANES]
        if unit == LANES:
            rot = pltpu.roll(xg, LANES // 2, 1)
        else:
            lane = lax.broadcasted_iota(jnp.int32, xg.shape, 1)
            rot = jnp.where((lane & (unit // 2)) == 0,
                            pltpu.roll(xg, LANES - unit // 2, 1), pltpu.roll(xg, unit // 2, 1))
        outs.append(xg * cos + rot * sin)
    return outs[0] if len(outs) == 1 else jnp.concatenate(outs, axis=-1)


_C_S5 = 0
_C_DAQ = 512
_C_DAK = 1024
_C_DAV = 1536
_C_GQQ = 2048
_C_GQK = 2560
_C_GQV = 2816
_C_ML = 3072
_C_GATE = 3584
_C_END = 3584 + 4096

_G_DAQ, _G_DAK, _G_GQQ, _G_GQK = 0, 512, 1024, 1536
_G_CQ, _G_CKV, _G_QN, _G_QR, _G_KN, _G_KR, _G_END = 1792, 2048, 2176, 2688, 2944, 3456, 3584


def _stream_specs(stream, tm, lat_tiles):
    if isinstance(stream, tuple):
        lat, ctx = stream
        d = lat.shape[-1]
        return ([pl.BlockSpec((1, tm, d), lambda i, t: (i, jnp.minimum(t, lat_tiles - 1), 0)),
                 pl.BlockSpec((1, tm, d), lambda i, t: (i, 0, 0))], [lat, ctx])
    return [pl.BlockSpec((1, tm, stream.shape[-1]), lambda i, t: (i, t, 0))], [stream]


def _stream_tile(refs, lat_tiles):
    if len(refs) == 2:
        return jnp.where(pl.program_id(1) >= lat_tiles, refs[1][0], refs[0][0])
    return refs[0][0]


def _inproj_kernel(*refs, lat_tiles, n_stream):
    (mod_ref, nrm_ref, gains_ref, rope_ref, w_ref, wuq_ref, wukv_ref,
     u_ref, daq_ref, dak_ref, dav_ref, gqq_ref, gqk_ref, gqv_ref,
     mlq_ref, mlk_ref, mlv_ref, gate_ref, u_sc) = refs[n_stream:]
    x = _stream_tile(refs[:n_stream], lat_tiles)
    d = x.shape[-1]
    ms = jnp.mean(x * x, axis=-1, keepdims=True)
    mod = mod_ref[0]
    shift, scale = mod[:, 0:d], mod[:, d:2 * d]
    h = (x * lax.rsqrt(ms + NORM_EPS) * nrm_ref[...]) * (1.0 + scale) + shift
    hb = h.astype(BF16)

    def proj(c0, c1):
        return jnp.dot(hb, w_ref[:, c0:c1], preferred_element_type=F32)

    def gain(g0, g1):
        return gains_ref[:, g0:g1]

    cos64, sin64 = rope_ref[:, 0:128], rope_ref[:, 128:256]
    cos128, sin128 = rope_ref[:, 256:384], rope_ref[:, 384:512]

    zu = proj(_C_S5, _C_DAQ)
    per_tile = x.shape[0] // S5_CHUNK
    for j in range(u_sc.shape[0]):
        u_sc[j] = zu[:, j * LANES:(j + 1) * LANES]
        for s in range(S5_CHUNK):
            u_ref[0, s * per_tile:(s + 1) * per_tile, j * LANES:(j + 1) * LANES] = (
                u_sc[j, pl.ds(s, per_tile, stride=S5_CHUNK), :].astype(BF16))

    q = _seg_rms(proj(_C_DAQ, _C_DAK), DA_DK) * gain(_G_DAQ, _G_DAK)
    daq_ref[0] = _rope(q, cos64, sin64, DA_DK).astype(BF16)
    k = _seg_rms(proj(_C_DAK, _C_DAV), DA_DK) * gain(_G_DAK, _G_GQQ)
    dak_ref[0] = _rope(k, cos64, sin64, DA_DK).astype(BF16)
    dav_ref[0] = proj(_C_DAV, _C_GQQ).astype(BF16)

    q = _seg_rms(proj(_C_GQQ, _C_GQK), GQ_DH) * gain(_G_GQQ, _G_GQK)
    gqq_ref[0] = _rope(q, cos128, sin128, GQ_DH).astype(BF16)
    k = _seg_rms(proj(_C_GQK, _C_GQV), GQ_DH) * gain(_G_GQK, _G_CQ)
    gqk_ref[0] = _rope(k, cos128, sin128, GQ_DH).astype(BF16)
    gqv_ref[0] = proj(_C_GQV, _C_ML).astype(BF16)

    zc = proj(_C_ML, _C_GATE)
    cq = (_seg_rms(zc[:, 0:256], 256) * gain(_G_CQ, _G_CKV)).astype(BF16)
    ckv = (_seg_rms(zc[:, 256:384], 128) * gain(_G_CKV, _G_QN)).astype(BF16)
    qq = jnp.dot(cq, wuq_ref[...], preferred_element_type=F32)
    kv = jnp.dot(ckv, wukv_ref[...], preferred_element_type=F32)
    qn = (_seg_rms(qq[:, 0:512], ML_NOPE) * gain(_G_QN, _G_QR)).astype(BF16)
    qr = _seg_rms(qq[:, 512:768], ML_ROPE) * gain(_G_QR, _G_KN)
    qr = _rope(qr, cos64, sin64, ML_ROPE)
    kn = (_seg_rms(kv[:, 0:512], ML_NOPE) * gain(_G_KN, _G_KR)).astype(BF16)
    kr = _seg_rms(zc[:, 384:512], ML_ROPE) * gain(_G_KR, _G_END)
    kr = _rope(kr, cos64, sin64, ML_ROPE).astype(BF16)
    lo = lax.broadcasted_iota(jnp.int32, (x.shape[0], LANES), 1) < ML_ROPE
    for hd in range(ML_HEADS):
        pair = qr[:, (hd // 2) * LANES:(hd // 2 + 1) * LANES]
        keep = lo if hd % 2 == 0 else jnp.logical_not(lo)
        mlq_ref[0, :, hd * 256:hd * 256 + 128] = qn[:, hd * 128:(hd + 1) * 128]
        mlq_ref[0, :, hd * 256 + 128:(hd + 1) * 256] = jnp.where(keep, pair, 0.0).astype(BF16)
        mlk_ref[0, :, hd * 256:hd * 256 + 128] = kn[:, hd * 128:(hd + 1) * 128]
        mlk_ref[0, :, hd * 256 + 128:(hd + 1) * 256] = kr
    mlv_ref[0] = kv[:, 512:1024].astype(BF16)

    for j in range(N_BRANCH):
        c0 = _C_GATE + j * d
        gate_ref[0, :, j * d:(j + 1) * d] = jax.nn.sigmoid(proj(c0, c0 + d)).astype(BF16)


def _inproj(stream, mod_rows, layer, n_batch, n_lat, lt, nrm, gains, rope, w, wuq, wukv):
    tm = ROW_TILE
    nt = lt // tm
    lat_tiles = n_lat // tm
    d = w.shape[0]
    x_specs, x_args = _stream_specs(stream, tm, lat_tiles)

    def mod_idx(i, t):
        return (layer * 8 + jnp.where(t >= lat_tiles, n_batch, i), 0, 0)

    widths = [512, 512, 512, 512, 512, 256, 256, 1024, 1024, 512, N_BRANCH * d]
    return pl.pallas_call(
        functools.partial(_inproj_kernel, lat_tiles=lat_tiles, n_stream=len(x_args)),
        grid=(n_batch, nt),
        in_specs=x_specs + [pl.BlockSpec((1, 1, mod_rows.shape[-1]), mod_idx),
                            _const_spec(nrm.shape), _const_spec(gains.shape),
                            pl.BlockSpec((tm, 512), lambda i, t: (t, 0)),
                            _const_spec(w.shape), _const_spec(wuq.shape), _const_spec(wukv.shape)],
        out_specs=[pl.BlockSpec((1, tm, wd), lambda i, t: (i, t, 0)) for wd in widths],
        out_shape=[jax.ShapeDtypeStruct((n_batch, lt, wd), BF16) for wd in widths],
        scratch_shapes=[pltpu.VMEM((widths[0] // LANES, tm, LANES), F32)],
        compiler_params=_cparams(("parallel", "parallel")),
        name="inproj",
    )(*x_args, mod_rows, nrm, gains, rope, w, wuq, wukv)


def _shift(n):
    assert n & (n - 1) == 0
    return n.bit_length() - 1


def _s5_expand(mg_ref, wsg_ref, wog_ref, m_sc, ws_sc, wo_sc):
    n_rows, cw = mg_ref.shape[1], mg_ref.shape[2]
    n_c, gt = S5_GROUP, n_rows // cw
    n_sp = wsg_ref.shape[2]
    n_p = n_sp // 4
    n_state = gt * n_sp

    def iota2(shape):
        return lax.broadcasted_iota(jnp.int32, shape, 0), lax.broadcasted_iota(jnp.int32, shape, 1)

    def div(i, n):
        return lax.shift_right_logical(i, _shift(n))

    def mod(i, n):
        return i & (n - 1)

    def onehot(cond):
        return jnp.where(cond, 1.0, 0.0).astype(BF16)

    def lane_group(i):
        return mod(div(i, n_c), gt)

    def state_group(i):
        return mod(div(i, n_p), gt)

    r, q = iota2((n_rows, n_rows))
    regroup = onehot(q == lane_group(r) * cw + div(r, gt * n_c) * n_c + mod(r, n_c))
    same_lane_group = lane_group(r) == lane_group(q)

    k, q = iota2((cw, n_rows))
    spread = onehot((div(q, gt * n_c) == div(k, n_c)) & (mod(q, n_c) == mod(k, n_c)))
    t_m = jnp.dot(regroup, mg_ref[0], preferred_element_type=F32).astype(BF16)
    m_sc[...] = jnp.where(same_lane_group, jnp.dot(t_m, spread, preferred_element_type=F32), 0.0).astype(BF16)

    k, q = iota2((n_sp, n_state))
    spread_s = onehot((div(q, gt * n_p) == div(k, n_p)) & (mod(q, n_p) == mod(k, n_p)))
    r, q = iota2((n_rows, n_state))
    t_s = jnp.dot(regroup, wsg_ref[0], preferred_element_type=F32).astype(BF16)
    ws_sc[...] = jnp.where(lane_group(r) == state_group(q),
                           jnp.dot(t_s, spread_s, preferred_element_type=F32), 0.0).astype(BF16)

    r, k = iota2((n_state, n_sp))
    gather_s = onehot((div(r, gt * n_p) == div(k, n_p)) & (mod(r, n_p) == mod(k, n_p)))
    t_o = lax.dot_general(wog_ref[0], regroup, (((1,), (1,)), ((), ())),
                          preferred_element_type=F32).astype(BF16)
    r, q = iota2((n_state, n_rows))
    wo_sc[...] = jnp.where(state_group(r) == lane_group(q),
                           jnp.dot(gather_s, t_o, preferred_element_type=F32), 0.0).astype(BF16)


def _s5_kernel(u_ref, mg_ref, wsg_ref, wog_ref, are_ref, aim_ref, y_ref, s_sc, m_sc, ws_sc, wo_sc,
               *, n_chunks, n_lat_chunks):
    @pl.when(pl.program_id(1) == 0)
    def _():
        _s5_expand(mg_ref, wsg_ref, wog_ref, m_sc, ws_sc, wo_sc)

    t_len = u_ref.shape[2]
    quarter = s_sc.shape[1] // 4
    a = jnp.concatenate([u_ref[0, :, s].reshape(n_chunks, LANES) for s in range(t_len)], axis=-1)
    s_sc[...] = jnp.dot(a, ws_sc[...], preferred_element_type=F32)
    af_re, ab_re = are_ref[0][:, 0:quarter], are_ref[0][:, quarter:2 * quarter]
    af_im, ab_im = aim_ref[0][:, 0:quarter], aim_ref[0][:, quarter:2 * quarter]
    n_ctx_chunks = n_chunks - n_lat_chunks

    def step(i, carry):
        hf_re, hf_im, hb_re, hb_im = carry
        rf = pl.ds(jnp.where(i < n_ctx_chunks, n_lat_chunks + i, i - n_ctx_chunks), 1)
        rb = pl.ds(n_chunks - 1 - i, 1)
        sf_re, sf_im = s_sc[rf, 0:quarter], s_sc[rf, quarter:2 * quarter]
        sb_re, sb_im = s_sc[rb, 2 * quarter:3 * quarter], s_sc[rb, 3 * quarter:4 * quarter]
        s_sc[rf, 0:quarter] = hf_re
        s_sc[rf, quarter:2 * quarter] = hf_im
        s_sc[rb, 2 * quarter:3 * quarter] = hb_re
        s_sc[rb, 3 * quarter:4 * quarter] = hb_im
        return (af_re * hf_re - af_im * hf_im + sf_re, af_re * hf_im + af_im * hf_re + sf_im,
                ab_re * hb_re - ab_im * hb_im + sb_re, ab_re * hb_im + ab_im * hb_re + sb_im)

    zero = jnp.zeros((1, quarter), F32)
    lax.fori_loop(0, n_chunks, step, (zero, zero, zero, zero))

    y = (jnp.dot(a, m_sc[...], preferred_element_type=F32)
         + jnp.dot(s_sc[...].astype(BF16), wo_sc[...], preferred_element_type=F32))
    for t in range(t_len):
        y_ref[0, 0, :, t] = y[:, t * LANES:(t + 1) * LANES].reshape(y_ref.shape[2], y_ref.shape[4], LANES)


def _s5_prep(lam_re, lam_im, log_step, b_re, b_im, c_re, c_im, d_skip):
    hp = lax.Precision.HIGHEST
    t_len, n_g, n_p, n_c = S5_CHUNK, lam_re.shape[1], S5_STATE, S5_GROUP
    lam_re, lam_im = lam_re.astype(F32), lam_im.astype(F32)
    dt = jnp.exp(log_step.astype(F32))[..., None]
    mag = jnp.exp(lam_re * dt)
    ab_re, ab_im = mag * jnp.cos(lam_im * dt), mag * jnp.sin(lam_im * dt)
    den = lam_re * lam_re + lam_im * lam_im
    nr, ni = ab_re - 1.0, ab_im
    coef_re = (nr * lam_re + ni * lam_im) / den
    coef_im = (ni * lam_re - nr * lam_im) / den
    b_re, b_im = b_re.astype(F32), b_im.astype(F32)
    bb_re = coef_re[..., None] * b_re - coef_im[..., None] * b_im
    bb_im = coef_re[..., None] * b_im + coef_im[..., None] * b_re
    kk = jnp.arange(t_len + 1, dtype=F32)[:, None, None, None]
    pmag = jnp.exp(lam_re * dt * kk)
    pw_re, pw_im = pmag * jnp.cos(lam_im * dt * kk), pmag * jnp.sin(lam_im * dt * kk)
    c_re, c_im = c_re.astype(F32), c_im.astype(F32)
    cp_re = c_re[None] * pw_re[:, :, :, None, :] - c_im[None] * pw_im[:, :, :, None, :]
    cp_im = c_re[None] * pw_im[:, :, :, None, :] + c_im[None] * pw_re[:, :, :, None, :]
    kern = (jnp.einsum('tdgop,dgpc->tdgoc', cp_re, bb_re, precision=hp)
            - jnp.einsum('tdgop,dgpc->tdgoc', cp_im, bb_im, precision=hp))
    s_idx = np.arange(t_len)[:, None]
    t_idx = np.arange(t_len)[None, :]
    lag_f = np.clip(t_idx - s_idx, 0, t_len)
    lag_b = np.clip(s_idx - t_idx, 0, t_len)
    mf = kern[lag_f, 0] * jnp.asarray(t_idx >= s_idx, F32)[:, :, None, None, None]
    mb = kern[lag_b, 1] * jnp.asarray(s_idx >= t_idx, F32)[:, :, None, None, None]
    m_full = jnp.transpose(mf + mb, (2, 0, 4, 1, 3))
    eye_t = jnp.eye(t_len, dtype=F32)[None, :, None, :, None]
    eye_c = jnp.eye(n_c, dtype=F32)[None, None, :, None, :]
    m_full = m_full + eye_t * eye_c * d_skip.astype(F32).reshape(n_g, 1, n_c, 1, 1)

    def bpow(pw_r, pw_i, direction):
        re = pw_r[..., None] * bb_re[direction][None] - pw_i[..., None] * bb_im[direction][None]
        im = pw_r[..., None] * bb_im[direction][None] + pw_i[..., None] * bb_re[direction][None]
        return jnp.transpose(re, (1, 0, 3, 2)), jnp.transpose(im, (1, 0, 3, 2))

    rev = np.arange(t_len - 1, -1, -1)
    fwd = np.arange(t_len)
    f_re, f_im = bpow(pw_re[rev, 0], pw_im[rev, 0], 0)
    g_re, g_im = bpow(pw_re[fwd, 1], pw_im[fwd, 1], 1)

    def cpow(idx, direction):
        return (jnp.transpose(cp_re[idx, direction], (1, 3, 0, 2)),
                jnp.transpose(cp_im[idx, direction], (1, 3, 0, 2)))

    of_re, of_im = cpow(np.arange(1, t_len + 1), 0)
    ob_re, ob_im = cpow(np.arange(t_len, 0, -1), 1)

    gt = S5_TILE_GROUPS
    n_j = n_g // gt
    cw = t_len * n_c
    mg = m_full.reshape(n_j, gt * cw, cw)
    wsg = jnp.stack([f_re, f_im, g_re, g_im], axis=3).reshape(n_j, gt * cw, 4 * n_p)
    wog = jnp.stack([of_re, -of_im, ob_re, -ob_im], axis=1).reshape(n_j, gt, 4 * n_p, cw)
    wog = jnp.transpose(wog, (0, 2, 1, 3)).reshape(n_j, 4 * n_p, gt * cw)
    a_re = jnp.concatenate([pw_re[t_len, 0].reshape(n_j, 1, gt * n_p),
                            pw_re[t_len, 1].reshape(n_j, 1, gt * n_p)], axis=-1)
    a_im = jnp.concatenate([pw_im[t_len, 0].reshape(n_j, 1, gt * n_p),
                            pw_im[t_len, 1].reshape(n_j, 1, gt * n_p)], axis=-1)
    return mg.astype(BF16), wsg.astype(BF16), wog.astype(BF16), a_re, a_im


def _s5(u, n_lat, ops):
    mg, wsg, wog, a_re, a_im = ops
    b, lt, width = u.shape
    n_tiles = lt // ROW_TILE
    per_tile = ROW_TILE // S5_CHUNK
    n_chunks = n_tiles * per_tile
    n_rows = mg.shape[1]
    n_state = S5_TILE_GROUPS * wsg.shape[2]
    blk = (1, n_tiles, S5_CHUNK, per_tile, LANES)
    tile = lambda j, i: (i, 0, 0, 0, j)
    wsel = lambda j, i: (j, 0, 0)
    y = pl.pallas_call(
        functools.partial(_s5_kernel, n_chunks=n_chunks, n_lat_chunks=n_lat // S5_CHUNK),
        grid=(width // LANES, b),
        in_specs=[pl.BlockSpec(blk, tile),
                  pl.BlockSpec((1,) + mg.shape[1:], wsel), pl.BlockSpec((1,) + wsg.shape[1:], wsel),
                  pl.BlockSpec((1,) + wog.shape[1:], wsel),
                  pl.BlockSpec((1, 1, n_state // 2), wsel), pl.BlockSpec((1, 1, n_state // 2), wsel)],
        out_specs=pl.BlockSpec((1,) + blk, lambda j, i: (j, i, 0, 0, 0, 0)),
        out_shape=jax.ShapeDtypeStruct((width // LANES, b, n_tiles, S5_CHUNK, per_tile, LANES), F32),
        scratch_shapes=[pltpu.VMEM((n_chunks, n_state), F32), pltpu.VMEM((n_rows, n_rows), BF16),
                        pltpu.VMEM((n_rows, n_state), BF16), pltpu.VMEM((n_state, n_rows), BF16)],
        compiler_params=_cparams(("arbitrary", "arbitrary")),
        name="s5_scan",
    )(u.reshape(b, n_tiles, S5_CHUNK, per_tile, width), mg, wsg, wog, a_re, a_im)
    return y.reshape(width // LANES, b, lt, LANES)


ATTN_ROWS = 1024
ATTN_ROW_BLOCKS = 4
ATTN_KEY_TILE = 256


def _attn_kernel(*refs, mode, tq, tk, n_lat, n_extra, lam_init):
    q_ref, k_ref, v_ref = refs[:3]
    extra = refs[3:3 + n_extra]
    o_ref, acc_sc, sa_sc, sb_sc = refs[-4:]
    n_keys = k_ref.shape[1]
    n_ctx = n_keys - n_lat

    q = q_ref[0]
    if mode == "da":
        lo = lax.broadcasted_iota(jnp.int32, q.shape, 1) < DA_DK
        zero = jnp.zeros_like(q)
        qs = jnp.concatenate([jnp.where(lo, q, zero), jnp.where(lo, zero, q)], axis=0)
    elif mode == "gq":
        qs = jnp.concatenate([q[:, 0:GQ_DH], q[:, GQ_DH:2 * GQ_DH]], axis=0)
    else:
        qs = q
    n_rb = ATTN_ROW_BLOCKS
    rb = qs.shape[0] // n_rb
    q_blocks = [qs[r * rb:(r + 1) * rb] for r in range(n_rb)]

    def qk(r, start, size):
        kc = k_ref[0, pl.ds(start, size), :]
        s = lax.dot_general(q_blocks[r], kc, (((1,), (1,)), ((), ())), preferred_element_type=F32)
        return s, jnp.max(s, axis=-1, keepdims=True)

    def softmax_pv(r, s_buf, mx, m_old, start, size):
        m_new = jnp.maximum(m_old, mx)
        alpha = jnp.exp2(m_old - m_new)
        rows = pl.ds(r * rb, rb)
        kt = min(ATTN_KEY_TILE, size)
        pv, lsum = None, None
        for c0 in range(0, size, kt):
            p = jnp.exp2((s_buf[:, c0:c0 + kt] - m_new).astype(BF16))
            p32 = p.astype(F32)
            for i in range(kt // LANES):
                t = p32[:, i * LANES:(i + 1) * LANES]
                lsum = t if lsum is None else lsum + t
            t = jnp.dot(p, v_ref[0, pl.ds(start + c0, kt), :], preferred_element_type=F32)
            pv = t if pv is None else pv + t
        acc_sc[rows, 0:LANES] = alpha * acc_sc[rows, 0:LANES] + pv
        acc_sc[rows, LANES:2 * LANES] = alpha * acc_sc[rows, LANES:2 * LANES] + lsum
        return m_new

    def stage(cur_buf, nxt_buf, nxt_start, nxt_size, cur_start, cur_size, ms, mxs):
        new_ms, new_mxs = [], []
        for r in range(n_rb):
            rows = pl.ds(r * rb, rb)
            s_n, mx_n = qk(r, nxt_start, nxt_size)
            nxt_buf[rows, 0:nxt_size] = s_n
            new_mxs.append(mx_n)
            new_ms.append(softmax_pv(r, cur_buf.at[rows, :], mxs[r], ms[r], cur_start, cur_size))
        return new_ms, new_mxs

    acc_sc[...] = jnp.zeros_like(acc_sc)
    chunks = [(j * tk, tk) for j in range(n_lat // tk)] + [(n_lat, n_ctx)]
    bufs = (sa_sc, sb_sc)
    ms = [jnp.full((rb, 1), -1e30, F32) for _ in range(n_rb)]
    mxs = []
    for r in range(n_rb):
        s, mx = qk(r, *chunks[0])
        sa_sc[pl.ds(r * rb, rb), 0:chunks[0][1]] = s
        mxs.append(mx)
    for j in range(len(chunks) - 1):
        ms, mxs = stage(bufs[j % 2], bufs[(j + 1) % 2], *chunks[j + 1], *chunks[j], ms, mxs)
    last = len(chunks) - 1
    for r in range(n_rb):
        softmax_pv(r, bufs[last % 2].at[pl.ds(r * rb, rb), :], mxs[r], ms[r], *chunks[last])
    acc = acc_sc[...]
    o = acc[:, 0:LANES] / jnp.sum(acc[:, LANES:2 * LANES], axis=-1, keepdims=True)

    if mode == "da":
        lam_ref, sub_ref = extra
        lam = lam_ref[...]
        lam_full = (jnp.exp(jnp.sum(lam[0:1] * lam[1:2], axis=-1, keepdims=True))
                    - jnp.exp(jnp.sum(lam[2:3] * lam[3:4], axis=-1, keepdims=True)) + lam_init)
        dlt = o[0:tq] - lam_full * o[tq:2 * tq]
        ms_d = jnp.mean(dlt * dlt, axis=-1, keepdims=True)
        o_ref[0] = (dlt * lax.rsqrt(ms_d + NORM_EPS) * sub_ref[...] * (1.0 - lam_init)).astype(o_ref.dtype)
    elif mode == "gq":
        o_ref[0] = jnp.concatenate([o[0:tq], o[tq:2 * tq]], axis=-1).astype(o_ref.dtype)
    else:
        o_ref[0] = o.astype(o_ref.dtype)


def _pick_tk(n_lat):
    for tk in (1024, 512, 256):
        if n_lat % tk == 0:
            return tk
    raise ValueError("latent length must be a multiple of 256")


def _attention(mode, q, k, v, n_lat, with_ctx, extra=(), lam_init=0.0):
    b, lt, _ = q.shape
    n_ctx = lt - n_lat
    if mode == "da":
        heads, qw, kw, ow, g = DA_HEADS, 128, 128, 128, 2
    elif mode == "gq":
        heads, qw, kw, ow, g = GQ_KV, 256, 128, 256, 2
    else:
        heads, qw, kw, ow, g = ML_HEADS, 256, 256, 128, 1
    tk = _pick_tk(n_lat)

    def call(ctx_only):
        if ctx_only:
            tq, q0, n_tiles, kv_rows, kv_blk = n_ctx, n_lat // n_ctx, 1, n_ctx, n_lat // n_ctx
        else:
            tq = ATTN_ROWS // g if n_lat % (ATTN_ROWS // g) == 0 else ROW_TILE
            q0, n_tiles, kv_rows, kv_blk = 0, n_lat // tq, lt, 0
        in_specs = [pl.BlockSpec((1, tq, qw), lambda i, h, t: (i, t + q0, h)),
                    pl.BlockSpec((1, kv_rows, kw), lambda i, h, t: (i, kv_blk, h)),
                    pl.BlockSpec((1, kv_rows, 128), lambda i, h, t: (i, kv_blk, h))]
        in_specs += [_const_spec(e.shape) for e in extra]
        return pl.pallas_call(
            functools.partial(_attn_kernel, mode=mode, tq=tq, tk=tk, n_lat=0 if ctx_only else n_lat,
                              n_extra=len(extra), lam_init=lam_init),
            grid=(b, heads, n_tiles),
            in_specs=in_specs,
            out_specs=pl.BlockSpec((1, tq, ow), lambda i, h, t: (i, t, h)),
            out_shape=jax.ShapeDtypeStruct((b, n_tiles * tq, BRANCH_W), BF16),
            scratch_shapes=[pltpu.VMEM((g * tq, 2 * LANES), F32),
                            pltpu.VMEM((g * tq, max(tk, n_ctx)), F32),
                            pltpu.VMEM((g * tq, max(tk, n_ctx)), F32)],
            compiler_params=_cparams(("parallel", "parallel", "arbitrary")),
            name="attn_" + mode + ("_ctx" if ctx_only else ""),
        )(q, k, v, *extra)

    return call(False), (call(True) if with_ctx else None)


def _gelu_tanh(x):
    return 0.5 * x * (1.0 + jnp.tanh(math.sqrt(2.0 / math.pi) * (x + 0.044715 * (x * x * x))))


def _merge_ffn_kernel(*refs, lat_tiles, with_ctx, n_stream):
    x = _stream_tile(refs[:n_stream], lat_tiles)
    refs = refs[n_stream:]
    (mod_ref, y_ref, gate_ref, wglu_ref, bglu_ref, wbr_ref, wout_ref,
     nrm_ref, win_ref, wffn_ref) = refs[:10]
    lat_refs = refs[10:13]
    ctx_refs = refs[13:16] if with_ctx else None
    o_ref, y_sc = refs[-2:]
    d = x.shape[-1]
    mod = mod_ref[0]
    per_tile = y_sc.shape[0] // S5_CHUNK
    for j in range(y_ref.shape[0]):
        for n in range(per_tile):
            y_sc[n * S5_CHUNK:(n + 1) * S5_CHUNK, j * LANES:(j + 1) * LANES] = (
                y_ref[j, 0, pl.ds(n, S5_CHUNK, stride=per_tile), :])
    g = _gelu_tanh(y_sc[...])
    glu = jax.nn.sigmoid(jnp.dot(g.astype(BF16), wglu_ref[...], preferred_element_type=F32)
                         + bglu_ref[...])
    branches = [(g * glu).astype(BF16)]
    for n in range(3):
        if with_ctx:
            is_ctx = pl.program_id(1) >= lat_tiles
            branches.append(jnp.where(is_ctx, ctx_refs[n][0], lat_refs[n][0]))
        else:
            branches.append(lat_refs[n][0])
    acc = None
    for n, br in enumerate(branches):
        term = gate_ref[0, :, n * d:(n + 1) * d].astype(F32) * jnp.dot(
            br, wbr_ref[n], preferred_element_type=F32)
        acc = term if acc is None else acc + term
    out = jnp.dot(acc.astype(BF16), wout_ref[...], preferred_element_type=F32)
    x = x + mod[:, 2 * d:3 * d] * out

    d_ff = wffn_ref.shape[0]
    shift, scale, g2 = mod[:, 3 * d:4 * d], mod[:, 4 * d:5 * d], mod[:, 5 * d:6 * d]
    ms = jnp.mean(x * x, axis=-1, keepdims=True)
    hb = ((x * lax.rsqrt(ms + NORM_EPS) * nrm_ref[...]) * (1.0 + scale) + shift).astype(BF16)
    gate = jnp.dot(hb, win_ref[:, 0:d_ff], preferred_element_type=F32)
    up = jnp.dot(hb, win_ref[:, d_ff:2 * d_ff], preferred_element_type=F32)
    act = (gate * jax.nn.sigmoid(gate) * up).astype(BF16)
    o_ref[0] = x + g2 * jnp.dot(act, wffn_ref[...], preferred_element_type=F32)


def _mod_index(layer, n_batch, lat_tiles):
    def idx(i, t):
        return (layer * 8 + jnp.where(t >= lat_tiles, n_batch, i), 0, 0)
    return idx


def _merge_ffn(stream, mod_rows, layer, n_lat, n_rows, y, attn, gates, wglu, bglu, wbr, wout,
               nrm_ffn, win, wffn):
    b, d = y.shape[1], wout.shape[0]
    tm = ROW_TILE
    lat_tiles = n_lat // tm
    with_ctx = n_rows > n_lat
    row = lambda i, t: (i, t, 0)
    lat_row = lambda i, t: (i, jnp.minimum(t, lat_tiles - 1), 0)
    x_specs, x_args = _stream_specs(stream, tm, lat_tiles)
    weights = [wglu, bglu, wbr, wout, nrm_ffn, win, wffn]
    in_specs = x_specs + [
        pl.BlockSpec((1, 1, mod_rows.shape[-1]), _mod_index(layer, b, lat_tiles)),
        pl.BlockSpec((BRANCH_W // LANES, 1, tm, LANES), lambda i, t: (0, i, t, 0)),
        pl.BlockSpec((1, tm, N_BRANCH * d), row)] + [_const_spec(w.shape) for w in weights]
    in_specs += [pl.BlockSpec((1, tm, BRANCH_W), lat_row)] * 3
    args = x_args + [mod_rows, y, gates] + weights + [a[0] for a in attn]
    if with_ctx:
        in_specs += [pl.BlockSpec((1, tm, BRANCH_W), lambda i, t: (i, 0, 0))] * 3
        args += [a[1] for a in attn]
    return pl.pallas_call(
        functools.partial(_merge_ffn_kernel, lat_tiles=lat_tiles, with_ctx=with_ctx,
                          n_stream=len(x_args)),
        grid=(b, n_rows // tm),
        in_specs=in_specs,
        out_specs=pl.BlockSpec((1, tm, d), row),
        out_shape=jax.ShapeDtypeStruct((b, n_rows, d), F32),
        scratch_shapes=[pltpu.VMEM((tm, BRANCH_W), F32)],
        compiler_params=_cparams(("parallel", "parallel")),
        name="merge_ffn",
    )(*args)


def _inproj_columns(d):
    de64, de128 = _deinterleave(64), _deinterleave(128)
    off_da = 512
    off_gq = off_da + 3 * DA_HEADS * DA_DV
    off_ml = off_gq + (GQ_HEADS + 2 * GQ_KV) * GQ_DH
    off_gate = off_ml + ML_QRANK + ML_KVRANK + ML_ROPE
    cols = [np.arange(512)]
    for part in range(2):
        for hd in range(DA_HEADS):
            for comp in range(2):
                cols.append(off_da + part * 512 + hd * 128 + comp * 64 + de64)
    cols.append(off_da + 1024 + np.arange(512))
    for hd in range(GQ_HEADS):
        cols.append(off_gq + hd * 128 + de128)
    for hd in range(GQ_KV):
        cols.append(off_gq + 512 + hd * 128 + de128)
    cols.append(off_gq + 768 + np.arange(256))
    cols.append(off_ml + np.arange(ML_QRANK + ML_KVRANK))
    cols.append(off_ml + ML_QRANK + ML_KVRANK + de64)
    cols.append(off_ml + ML_QRANK + ML_KVRANK + de64)
    cols.append(off_gate + np.arange(N_BRANCH * d))
    return np.concatenate(cols)


def _rope_table(n_lat, n_ctx):
    t = np.arange(n_lat)
    r = (t // GRID_W).astype(np.float32)
    col = (t % GRID_W).astype(np.float32)

    def angles(rot_dim):
        half = rot_dim // 2
        inv = jnp.asarray(ROPE_THETA, F32) ** (-jnp.arange(0, half, 2, dtype=F32) / half)
        return jnp.concatenate([jnp.asarray(r)[:, None] * inv, jnp.asarray(col)[:, None] * inv], axis=-1)

    a64, a128 = angles(64), angles(128)
    c64, s64 = jnp.cos(a64), jnp.sin(a64)
    c128, s128 = jnp.cos(a128), jnp.sin(a128)
    lat = jnp.concatenate([c64, c64, c64, c64, -s64, s64, -s64, s64, c128, c128, -s128, s128], axis=-1)
    ctx = jnp.concatenate([jnp.ones((n_ctx, 128), F32), jnp.zeros((n_ctx, 128), F32),
                           jnp.ones((n_ctx, 128), F32), jnp.zeros((n_ctx, 128), F32)], axis=-1)
    return jnp.concatenate([lat, ctx], axis=0)


def kernel(x, c, ctx, c_ctx, w_mod, b_mod, norm_mix, w_in, s5_lam_re, s5_lam_im, s5_log_step, s5_b_re, s5_b_im, s5_c_re, s5_c_im, s5_d, s5_w_glu, s5_b_glu, da_q_norm, da_k_norm, da_lam, da_subln, gq_q_norm, gq_k_norm, ml_cq_norm, ml_ckv_norm, ml_w_uq, ml_w_ukv, ml_q_norm, ml_k_norm, w_branch, w_out, norm_ffn, w_ffn_in, w_ffn_out):
    b, n_lat, d = x.shape
    n_ctx = ctx.shape[1]
    depth = w_mod.shape[0]
    assert n_ctx == ROW_TILE and n_lat % ROW_TILE == 0 and b < 8 and d == 1024

    cc = jnp.zeros((8, d), F32).at[:b].set(c).at[b].set(c_ctx)
    mod_rows = _modulation(cc, w_mod, b_mod).reshape(depth * 8, 1, 6 * d)
    rope = _rope_table(n_lat, n_ctx)
    xc = (x, ctx)

    de64, de128 = _deinterleave(64), _deinterleave(128)
    cols = _inproj_columns(d)
    uq_cols = np.concatenate([hd * 192 + np.arange(128) for hd in range(ML_HEADS)]
                             + [hd * 192 + 128 + de64 for hd in range(ML_HEADS)])
    ukv_cols = np.concatenate([hd * 256 + np.arange(128) for hd in range(ML_HEADS)]
                              + [hd * 256 + 128 + np.arange(128) for hd in range(ML_HEADS)])
    log2e = math.log2(math.e)
    da_scale, gq_scale = DA_DK ** -0.5 * log2e, GQ_DH ** -0.5 * log2e
    ml_scale = (ML_NOPE + ML_ROPE) ** -0.5 * log2e

    for i in range(depth):
        last = i == depth - 1
        lam_init = 0.8 - 0.6 * math.exp(-0.3 * i)
        w = w_in[i][:, cols].astype(BF16)
        wuq = ml_w_uq[i][:, uq_cols].astype(BF16)
        wukv = ml_w_ukv[i][:, ukv_cols].astype(BF16)
        gains = jnp.concatenate([
            jnp.tile(da_q_norm[i][de64], 8) * da_scale, jnp.tile(da_k_norm[i][de64], 8),
            jnp.tile(gq_q_norm[i][de128], 4) * gq_scale, jnp.tile(gq_k_norm[i][de128], 2),
            ml_cq_norm[i], ml_ckv_norm[i],
            jnp.tile(ml_q_norm[i][:ML_NOPE], 4) * ml_scale,
            jnp.tile(ml_q_norm[i][ML_NOPE:][de64], 4) * ml_scale,
            jnp.tile(ml_k_norm[i][:ML_NOPE], 4), jnp.tile(ml_k_norm[i][ML_NOPE:][de64], 2),
        ]).astype(F32)[None, :]

        (u, daq, dak, dav, gqq, gqk, gqv, mlq, mlk, mlv, gates) = _inproj(
            xc, mod_rows, i, b, n_lat, n_lat + n_ctx, norm_mix[i][None, :], gains, rope, w, wuq, wukv)

        s5_ops = _s5_prep(s5_lam_re[i], s5_lam_im[i], s5_log_step[i], s5_b_re[i], s5_b_im[i],
                          s5_c_re[i], s5_c_im[i], s5_d[i])
        y = _s5(u, n_lat, s5_ops)

        n_rows = n_lat if last else n_lat + n_ctx
        da = _attention("da", daq, dak, dav, n_lat, not last,
                        extra=(da_lam[i].astype(F32), da_subln[i].astype(F32)[None, :]),
                        lam_init=lam_init)
        gq = _attention("gq", gqq, gqk, gqv, n_lat, not last)
        ml = _attention("ml", mlq, mlk, mlv, n_lat, not last)

        xc = _merge_ffn(xc, mod_rows, i, n_lat, n_rows, y, (da, gq, ml), gates,
                        s5_w_glu[i].astype(BF16), s5_b_glu[i].astype(F32)[None, :],
                        w_branch[i].astype(BF16), w_out[i].astype(BF16),
                        norm_ffn[i][None, :], w_ffn_in[i].astype(BF16), w_ffn_out[i].astype(BF16))
    return xc
```

```python
import functools
import math

import numpy as np
import jax
import jax.numpy as jnp
from jax import lax
from jax.experimental import pallas as pl
from jax.experimental.pallas import tpu as pltpu

GRID_W = 64
ROPE_THETA = 10000.0
NORM_EPS = 1e-6

S5_GROUP = 16
S5_STATE = 64
S5_CHUNK = 8

DA_HEADS = 4
DA_DK = 64
DA_DV = 128
GQ_HEADS = 4
GQ_KV = 2
GQ_DH = 128
ML_HEADS = 4
ML_QRANK = 256
ML_KVRANK = 128
ML_NOPE = 128
ML_ROPE = 64
ML_DV = 128
N_BRANCH = 4
BRANCH_W = 512

LANES = 128
S5_TILE_GROUPS = LANES // S5_GROUP
ROW_TILE = 256
VMEM_LIMIT = 56 * 1024 * 1024

F32 = jnp.float32
BF16 = jnp.bfloat16


def _deinterleave(n):
    return np.concatenate([np.arange(0, n, 2), np.arange(1, n, 2)])


def _cparams(sem):
    return pltpu.CompilerParams(dimension_semantics=sem, vmem_limit_bytes=VMEM_LIMIT)


def _const_spec(shape):
    nd = len(shape)
    return pl.BlockSpec(shape, lambda *_: (0,) * nd)


def _mod_kernel(cc_ref, w_ref, b_ref, o_ref):
    a = cc_ref[...]
    a = a * jax.nn.sigmoid(a)
    o_ref[0] = jnp.dot(a, w_ref[0], preferred_element_type=F32,
                       precision=lax.Precision.HIGHEST) + b_ref[0]


def _modulation(cc, w_mod, b_mod):
    depth, d, n = w_mod.shape
    tn = n // 4
    return pl.pallas_call(
        _mod_kernel,
        grid=(depth, n // tn),
        in_specs=[pl.BlockSpec((8, d), lambda i, j: (0, 0)),
                  pl.BlockSpec((1, d, tn), lambda i, j: (i, 0, j)),
                  pl.BlockSpec((1, 1, tn), lambda i, j: (i, 0, j))],
        out_specs=pl.BlockSpec((1, 8, tn), lambda i, j: (i, 0, j)),
        out_shape=jax.ShapeDtypeStruct((depth, 8, n), F32),
        compiler_params=_cparams(("arbitrary", "arbitrary")),
        name="modulation",
    )(cc, w_mod, b_mod.reshape(depth, 1, n))


def _seg_rms(z, seg):
    width = z.shape[-1]
    if seg == 2 * LANES:
        outs = []
        for g in range(width // seg):
            zg = z[:, g * seg:(g + 1) * seg]
            ms = jnp.sum(zg * zg, axis=-1, keepdims=True) * (1.0 / seg)
            outs.append(zg * lax.rsqrt(ms + NORM_EPS))
        return outs[0] if len(outs) == 1 else jnp.concatenate(outs, axis=-1)
    outs = []
    for g in range(width // LANES):
        zg = z[:, g * LANES:(g + 1) * LANES]
        zz = zg * zg
        if seg == LANES:
            ms = jnp.sum(zz, axis=-1, keepdims=True) * (1.0 / seg)
        else:
            lo = lax.broadcasted_iota(jnp.int32, zz.shape, 1) < seg
            s_lo = jnp.sum(jnp.where(lo, zz, 0.0), axis=-1, keepdims=True)
            s_hi = jnp.sum(jnp.where(lo, 0.0, zz), axis=-1, keepdims=True)
            ms = jnp.where(lo, s_lo, s_hi) * (1.0 / seg)
        outs.append(zg * lax.rsqrt(ms + NORM_EPS))
    return outs[0] if len(outs) == 1 else jnp.concatenate(outs, axis=-1)


def _rope(x, cos, sin, unit):
    outs = []
    for g in range(x.shape[-1] // LANES):
        xg = x[:, g * LANES:(g + 1) * LANES]
        if unit == LANES:
            rot = pltpu.roll(xg, LANES // 2, 1)
        else:
            lane = lax.broadcasted_iota(jnp.int32, xg.shape, 1)
            rot = jnp.where((lane & (unit // 2)) == 0,
                            pltpu.roll(xg, LANES - unit // 2, 1), pltpu.roll(xg, unit // 2, 1))
        outs.append(xg * cos + rot * sin)
    return outs[0] if len(outs) == 1 else jnp.concatenate(outs, axis=-1)


_C_S5 = 0
_C_DAQ = 512
_C_DAK = 1024
_C_DAV = 1536
_C_GQQ = 2048
_C_GQK = 2560
_C_GQV = 2816
_C_ML = 3072
_C_GATE = 3584
_C_END = 3584 + 4096

_G_DAQ, _G_DAK, _G_GQQ, _G_GQK = 0, 512, 1024, 1536
_G_CQ, _G_CKV, _G_QN, _G_QR, _G_KN, _G_KR, _G_END = 1792, 2048, 2176, 2688, 2944, 3456, 3584


def _stream_specs(stream, tm, lat_tiles):
    if isinstance(stream, tuple):
        lat, ctx = stream
        d = lat.shape[-1]
        return ([pl.BlockSpec((1, tm, d), lambda i, t: (i, jnp.minimum(t, lat_tiles - 1), 0)),
                 pl.BlockSpec((1, tm, d), lambda i, t: (i, 0, 0))], [lat, ctx])
    return [pl.BlockSpec((1, tm, stream.shape[-1]), lambda i, t: (i, t, 0))], [stream]


def _stream_tile(refs, lat_tiles):
    if len(refs) == 2:
        return jnp.where(pl.program_id(1) >= lat_tiles, refs[1][0], refs[0][0])
    return refs[0][0]


def _inproj_kernel(*refs, lat_tiles, n_stream):
    (mod_ref, nrm_ref, gains_ref, rope_ref, w_ref, wuq_ref, wukv_ref,
     u_ref, daq_ref, dak_ref, dav_ref, gqq_ref, gqk_ref, gqv_ref,
     mlq_ref, mlk_ref, mlv_ref, gate_ref, u_sc) = refs[n_stream:]
    x = _stream_tile(refs[:n_stream], lat_tiles)
    d = x.shape[-1]
    ms = jnp.mean(x * x, axis=-1, keepdims=True)
    mod = mod_ref[0]
    shift, scale = mod[:, 0:d], mod[:, d:2 * d]
    h = (x * lax.rsqrt(ms + NORM_EPS) * nrm_ref[...]) * (1.0 + scale) + shift
    hb = h.astype(BF16)

    def proj(c0, c1):
        return jnp.dot(hb, w_ref[:, c0:c1], preferred_element_type=F32)

    def gain(g0, g1):
        return gains_ref[:, g0:g1]

    cos64, sin64 = rope_ref[:, 0:128], rope_ref[:, 128:256]
    cos128, sin128 = rope_ref[:, 256:384], rope_ref[:, 384:512]

    zu = proj(_C_S5, _C_DAQ)
    per_tile = x.shape[0] // S5_CHUNK
    for j in range(u_sc.shape[0]):
        u_sc[j] = zu[:, j * LANES:(j + 1) * LANES]
        for s in range(S5_CHUNK):
            u_ref[0, s * per_tile:(s + 1) * per_tile, j * LANES:(j + 1) * LANES] = (
                u_sc[j, pl.ds(s, per_tile, stride=S5_CHUNK), :].astype(BF16))

    q = _seg_rms(proj(_C_DAQ, _C_DAK), DA_DK) * gain(_G_DAQ, _G_DAK)
    daq_ref[0] = _rope(q, cos64, sin64, DA_DK).astype(BF16)
    k = _seg_rms(proj(_C_DAK, _C_DAV), DA_DK) * gain(_G_DAK, _G_GQQ)
    dak_ref[0] = _rope(k, cos64, sin64, DA_DK).astype(BF16)
    dav_ref[0] = proj(_C_DAV, _C_GQQ).astype(BF16)

    q = _seg_rms(proj(_C_GQQ, _C_GQK), GQ_DH) * gain(_G_GQQ, _G_GQK)
    gqq_ref[0] = _rope(q, cos128, sin128, GQ_DH).astype(BF16)
    k = _seg_rms(proj(_C_GQK, _C_GQV), GQ_DH) * gain(_G_GQK, _G_CQ)
    gqk_ref[0] = _rope(k, cos128, sin128, GQ_DH).astype(BF16)
    gqv_ref[0] = proj(_C_GQV, _C_ML).astype(BF16)

    zc = proj(_C_ML, _C_GATE)
    cq = (_seg_rms(zc[:, 0:256], 256) * gain(_G_CQ, _G_CKV)).astype(BF16)
    ckv = (_seg_rms(zc[:, 256:384], 128) * gain(_G_CKV, _G_QN)).astype(BF16)
    qq = jnp.dot(cq, wuq_ref[...], preferred_element_type=F32)
    kv = jnp.dot(ckv, wukv_ref[...], preferred_element_type=F32)
    qn = (_seg_rms(qq[:, 0:512], ML_NOPE) * gain(_G_QN, _G_QR)).astype(BF16)
    qr = _seg_rms(qq[:, 512:768], ML_ROPE) * gain(_G_QR, _G_KN)
    qr = _rope(qr, cos64, sin64, ML_ROPE)
    kn = (_seg_rms(kv[:, 0:512], ML_NOPE) * gain(_G_KN, _G_KR)).astype(BF16)
    kr = _seg_rms(zc[:, 384:512], ML_ROPE) * gain(_G_KR, _G_END)
    kr = _rope(kr, cos64, sin64, ML_ROPE).astype(BF16)
    lo = lax.broadcasted_iota(jnp.int32, (x.shape[0], LANES), 1) < ML_ROPE
    for hd in range(ML_HEADS):
        pair = qr[:, (hd // 2) * LANES:(hd // 2 + 1) * LANES]
        keep = lo if hd % 2 == 0 else jnp.logical_not(lo)
        mlq_ref[0, :, hd * 256:hd * 256 + 128] = qn[:, hd * 128:(hd + 1) * 128]
        mlq_ref[0, :, hd * 256 + 128:(hd + 1) * 256] = jnp.where(keep, pair, 0.0).astype(BF16)
        mlk_ref[0, :, hd * 256:hd * 256 + 128] = kn[:, hd * 128:(hd + 1) * 128]
        mlk_ref[0, :, hd * 256 + 128:(hd + 1) * 256] = kr
    mlv_ref[0] = kv[:, 512:1024].astype(BF16)

    for j in range(N_BRANCH):
        c0 = _C_GATE + j * d
        gate_ref[0, :, j * d:(j + 1) * d] = jax.nn.sigmoid(proj(c0, c0 + d)).astype(BF16)


def _inproj(stream, mod_rows, layer, n_batch, n_lat, lt, nrm, gains, rope, w, wuq, wukv):
    tm = ROW_TILE
    nt = lt // tm
    lat_tiles = n_lat // tm
    d = w.shape[0]
    x_specs, x_args = _stream_specs(stream, tm, lat_tiles)

    def mod_idx(i, t):
        return (layer * 8 + jnp.where(t >= lat_tiles, n_batch, i), 0, 0)

    widths = [512, 512, 512, 512, 512, 256, 256, 1024, 1024, 512, N_BRANCH * d]
    return pl.pallas_call(
        functools.partial(_inproj_kernel, lat_tiles=lat_tiles, n_stream=len(x_args)),
        grid=(n_batch, nt),
        in_specs=x_specs + [pl.BlockSpec((1, 1, mod_rows.shape[-1]), mod_idx),
                            _const_spec(nrm.shape), _const_spec(gains.shape),
                            pl.BlockSpec((tm, 512), lambda i, t: (t, 0)),
                            _const_spec(w.shape), _const_spec(wuq.shape), _const_spec(wukv.shape)],
        out_specs=[pl.BlockSpec((1, tm, wd), lambda i, t: (i, t, 0)) for wd in widths],
        out_shape=[jax.ShapeDtypeStruct((n_batch, lt, wd), BF16) for wd in widths],
        scratch_shapes=[pltpu.VMEM((widths[0] // LANES, tm, LANES), F32)],
        compiler_params=_cparams(("parallel", "parallel")),
        name="inproj",
    )(*x_args, mod_rows, nrm, gains, rope, w, wuq, wukv)


def _shift(n):
    assert n & (n - 1) == 0
    return n.bit_length() - 1


def _s5_expand(mg_ref, wsg_ref, wog_ref, m_sc, ws_sc, wo_sc):
    n_rows, cw = mg_ref.shape[1], mg_ref.shape[2]
    n_c, gt = S5_GROUP, n_rows // cw
    n_sp = wsg_ref.shape[2]
    n_p = n_sp // 4
    n_state = gt * n_sp

    def iota2(shape):
        return lax.broadcasted_iota(jnp.int32, shape, 0), lax.broadcasted_iota(jnp.int32, shape, 1)

    def div(i, n):
        return lax.shift_right_logical(i, _shift(n))

    def mod(i, n):
        return i & (n - 1)

    def onehot(cond):
        return jnp.where(cond, 1.0, 0.0).astype(BF16)

    def lane_group(i):
        return mod(div(i, n_c), gt)

    def state_group(i):
        return mod(div(i, n_p), gt)

    r, q = iota2((n_rows, n_rows))
    regroup = onehot(q == lane_group(r) * cw + div(r, gt * n_c) * n_c + mod(r, n_c))
    same_lane_group = lane_group(r) == lane_group(q)

    k, q = iota2((cw, n_rows))
    spread = onehot((div(q, gt * n_c) == div(k, n_c)) & (mod(q, n_c) == mod(k, n_c)))
    t_m = jnp.dot(regroup, mg_ref[0], preferred_element_type=F32).astype(BF16)
    m_sc[...] = jnp.where(same_lane_group, jnp.dot(t_m, spread, preferred_element_type=F32), 0.0).astype(BF16)

    k, q = iota2((n_sp, n_state))
    spread_s = onehot((div(q, gt * n_p) == div(k, n_p)) & (mod(q, n_p) == mod(k, n_p)))
    r, q = iota2((n_rows, n_state))
    t_s = jnp.dot(regroup, wsg_ref[0], preferred_element_type=F32).astype(BF16)
    ws_sc[...] = jnp.where(lane_group(r) == state_group(q),
                           jnp.dot(t_s, spread_s, preferred_element_type=F32), 0.0).astype(BF16)

    r, k = iota2((n_state, n_sp))
    gather_s = onehot((div(r, gt * n_p) == div(k, n_p)) & (mod(r, n_p) == mod(k, n_p)))
    t_o = lax.dot_general(wog_ref[0], regroup, (((1,), (1,)), ((), ())),
                          preferred_element_type=F32).astype(BF16)
    r, q = iota2((n_state, n_rows))
    wo_sc[...] = jnp.where(state_group(r) == lane_group(q),
                           jnp.dot(gather_s, t_o, preferred_element_type=F32), 0.0).astype(BF16)


def _s5_kernel(u_ref, mg_ref, wsg_ref, wog_ref, are_ref, aim_ref, y_ref, s_sc, m_sc, ws_sc, wo_sc,
               *, n_chunks, n_lat_chunks):
    @pl.when(pl.program_id(1) == 0)
    def _():
        _s5_expand(mg_ref, wsg_ref, wog_ref, m_sc, ws_sc, wo_sc)

    t_len = u_ref.shape[2]
    quarter = s_sc.shape[1] // 4
    a = jnp.concatenate([u_ref[0, :, s].reshape(n_chunks, LANES) for s in range(t_len)], axis=-1)
    s_sc[...] = jnp.dot(a, ws_sc[...], preferred_element_type=F32)
    af_re, ab_re = are_ref[0][:, 0:quarter], are_ref[0][:, quarter:2 * quarter]
    af_im, ab_im = aim_ref[0][:, 0:quarter], aim_ref[0][:, quarter:2 * quarter]
    n_ctx_chunks = n_chunks - n_lat_chunks

    def step(i, carry):
        hf_re, hf_im, hb_re, hb_im = carry
        rf = pl.ds(jnp.where(i < n_ctx_chunks, n_lat_chunks + i, i - n_ctx_chunks), 1)
        rb = pl.ds(n_chunks - 1 - i, 1)
        sf_re, sf_im = s_sc[rf, 0:quarter], s_sc[rf, quarter:2 * quarter]
        sb_re, sb_im = s_sc[rb, 2 * quarter:3 * quarter], s_sc[rb, 3 * quarter:4 * quarter]
        s_sc[rf, 0:quarter] = hf_re
        s_sc[rf, quarter:2 * quarter] = hf_im
        s_sc[rb, 2 * quarter:3 * quarter] = hb_re
        s_sc[rb, 3 * quarter:4 * quarter] = hb_im
        return (af_re * hf_re - af_im * hf_im + sf_re, af_re * hf_im + af_im * hf_re + sf_im,
                ab_re * hb_re - ab_im * hb_im + sb_re, ab_re * hb_im + ab_im * hb_re + sb_im)

    zero = jnp.zeros((1, quarter), F32)
    lax.fori_loop(0, n_chunks, step, (zero, zero, zero, zero), unroll=4)

    y = (jnp.dot(a, m_sc[...], preferred_element_type=F32)
         + jnp.dot(s_sc[...].astype(BF16), wo_sc[...], preferred_element_type=F32))
    for t in range(t_len):
        y_ref[0, 0, :, t] = y[:, t * LANES:(t + 1) * LANES].reshape(y_ref.shape[2], y_ref.shape[4], LANES)


def _s5_prep(lam_re, lam_im, log_step, b_re, b_im, c_re, c_im, d_skip):
    hp = lax.Precision.HIGHEST
    t_len, n_g, n_p, n_c = S5_CHUNK, lam_re.shape[1], S5_STATE, S5_GROUP
    lam_re, lam_im = lam_re.astype(F32), lam_im.astype(F32)
    dt = jnp.exp(log_step.astype(F32))[..., None]
    mag = jnp.exp(lam_re * dt)
    ab_re, ab_im = mag * jnp.cos(lam_im * dt), mag * jnp.sin(lam_im * dt)
    den = lam_re * lam_re + lam_im * lam_im
    nr, ni = ab_re - 1.0, ab_im
    coef_re = (nr * lam_re + ni * lam_im) / den
    coef_im = (ni * lam_re - nr * lam_im) / den
    b_re, b_im = b_re.astype(F32), b_im.astype(F32)
    bb_re = coef_re[..., None] * b_re - coef_im[..., None] * b_im
    bb_im = coef_re[..., None] * b_im + coef_im[..., None] * b_re
    kk = jnp.arange(t_len + 1, dtype=F32)[:, None, None, None]
    pmag = jnp.exp(lam_re * dt * kk)
    pw_re, pw_im = pmag * jnp.cos(lam_im * dt * kk), pmag * jnp.sin(lam_im * dt * kk)
    c_re, c_im = c_re.astype(F32), c_im.astype(F32)
    cp_re = c_re[None] * pw_re[:, :, :, None, :] - c_im[None] * pw_im[:, :, :, None, :]
    cp_im = c_re[None] * pw_im[:, :, :, None, :] + c_im[None] * pw_re[:, :, :, None, :]
    kern = (jnp.einsum('tdgop,dgpc->tdgoc', cp_re, bb_re, precision=hp)
            - jnp.einsum('tdgop,dgpc->tdgoc', cp_im, bb_im, precision=hp))
    s_idx = np.arange(t_len)[:, None]
    t_idx = np.arange(t_len)[None, :]
    lag_f = np.clip(t_idx - s_idx, 0, t_len)
    lag_b = np.clip(s_idx - t_idx, 0, t_len)
    mf = kern[lag_f, 0] * jnp.asarray(t_idx >= s_idx, F32)[:, :, None, None, None]
    mb = kern[lag_b, 1] * jnp.asarray(s_idx >= t_idx, F32)[:, :, None, None, None]
    m_full = jnp.transpose(mf + mb, (2, 0, 4, 1, 3))
    eye_t = jnp.eye(t_len, dtype=F32)[None, :, None, :, None]
    eye_c = jnp.eye(n_c, dtype=F32)[None, None, :, None, :]
    m_full = m_full + eye_t * eye_c * d_skip.astype(F32).reshape(n_g, 1, n_c, 1, 1)

    def bpow(pw_r, pw_i, direction):
        re = pw_r[..., None] * bb_re[direction][None] - pw_i[..., None] * bb_im[direction][None]
        im = pw_r[..., None] * bb_im[direction][None] + pw_i[..., None] * bb_re[direction][None]
        return jnp.transpose(re, (1, 0, 3, 2)), jnp.transpose(im, (1, 0, 3, 2))

    rev = np.arange(t_len - 1, -1, -1)
    fwd = np.arange(t_len)
    f_re, f_im = bpow(pw_re[rev, 0], pw_im[rev, 0], 0)
    g_re, g_im = bpow(pw_re[fwd, 1], pw_im[fwd, 1], 1)

    def cpow(idx, direction):
        return (jnp.transpose(cp_re[idx, direction], (1, 3, 0, 2)),
                jnp.transpose(cp_im[idx, direction], (1, 3, 0, 2)))

    of_re, of_im = cpow(np.arange(1, t_len + 1), 0)
    ob_re, ob_im = cpow(np.arange(t_len, 0, -1), 1)

    gt = S5_TILE_GROUPS
    n_j = n_g // gt
    cw = t_len * n_c
    mg = m_full.reshape(n_j, gt * cw, cw)
    wsg = jnp.stack([f_re, f_im, g_re, g_im], axis=3).reshape(n_j, gt * cw, 4 * n_p)
    wog = jnp.stack([of_re, -of_im, ob_re, -ob_im], axis=1).reshape(n_j, gt, 4 * n_p, cw)
    wog = jnp.transpose(wog, (0, 2, 1, 3)).reshape(n_j, 4 * n_p, gt * cw)
    a_re = jnp.concatenate([pw_re[t_len, 0].reshape(n_j, 1, gt * n_p),
                            pw_re[t_len, 1].reshape(n_j, 1, gt * n_p)], axis=-1)
    a_im = jnp.concatenate([pw_im[t_len, 0].reshape(n_j, 1, gt * n_p),
                            pw_im[t_len, 1].reshape(n_j, 1, gt * n_p)], axis=-1)
    return mg.astype(BF16), wsg.astype(BF16), wog.astype(BF16), a_re, a_im


def _s5(u, n_lat, ops):
    mg, wsg, wog, a_re, a_im = ops
    b, lt, width = u.shape
    n_tiles = lt // ROW_TILE
    per_tile = ROW_TILE // S5_CHUNK
    n_chunks = n_tiles * per_tile
    n_rows = mg.shape[1]
    n_state = S5_TILE_GROUPS * wsg.shape[2]
    blk = (1, n_tiles, S5_CHUNK, per_tile, LANES)
    tile = lambda j, i: (i, 0, 0, 0, j)
    wsel = lambda j, i: (j, 0, 0)
    y = pl.pallas_call(
        functools.partial(_s5_kernel, n_chunks=n_chunks, n_lat_chunks=n_lat // S5_CHUNK),
        grid=(width // LANES, b),
        in_specs=[pl.BlockSpec(blk, tile),
                  pl.BlockSpec((1,) + mg.shape[1:], wsel), pl.BlockSpec((1,) + wsg.shape[1:], wsel),
                  pl.BlockSpec((1,) + wog.shape[1:], wsel),
                  pl.BlockSpec((1, 1, n_state // 2), wsel), pl.BlockSpec((1, 1, n_state // 2), wsel)],
        out_specs=pl.BlockSpec((1,) + blk, lambda j, i: (j, i, 0, 0, 0, 0)),
        out_shape=jax.ShapeDtypeStruct((width // LANES, b, n_tiles, S5_CHUNK, per_tile, LANES), F32),
        scratch_shapes=[pltpu.VMEM((n_chunks, n_state), F32), pltpu.VMEM((n_rows, n_rows), BF16),
                        pltpu.VMEM((n_rows, n_state), BF16), pltpu.VMEM((n_state, n_rows), BF16)],
        compiler_params=_cparams(("arbitrary", "arbitrary")),
        name="s5_scan",
    )(u.reshape(b, n_tiles, S5_CHUNK, per_tile, width), mg, wsg, wog, a_re, a_im)
    return y.reshape(width // LANES, b, lt, LANES)


ATTN_ROWS = 1024
ATTN_ROW_BLOCKS = 4
ATTN_KEY_TILE = 512


def _attn_kernel(*refs, mode, tq, tk, n_lat, n_extra, lam_init):
    q_ref, k_ref, v_ref = refs[:3]
    extra = refs[3:3 + n_extra]
    o_ref, acc_sc, sa_sc, sb_sc = refs[-4:]
    n_keys = k_ref.shape[1]
    n_ctx = n_keys - n_lat

    q = q_ref[0]
    if mode == "da":
        lo = lax.broadcasted_iota(jnp.int32, q.shape, 1) < DA_DK
        zero = jnp.zeros_like(q)
        qs = jnp.concatenate([jnp.where(lo, q, zero), jnp.where(lo, zero, q)], axis=0)
    elif mode == "gq":
        qs = jnp.concatenate([q[:, 0:GQ_DH], q[:, GQ_DH:2 * GQ_DH]], axis=0)
    else:
        qs = q
    n_rb = ATTN_ROW_BLOCKS
    rb = qs.shape[0] // n_rb
    q_blocks = [qs[r * rb:(r + 1) * rb] for r in range(n_rb)]

    def qk(r, start, size):
        kc = k_ref[0, pl.ds(start, size), :]
        s = lax.dot_general(q_blocks[r], kc, (((1,), (1,)), ((), ())), preferred_element_type=F32)
        return s, jnp.max(s, axis=-1, keepdims=True)

    def softmax_pv(r, s_buf, mx, m_old, start, size):
        m_new = jnp.maximum(m_old, mx)
        alpha = jnp.exp2(m_old - m_new)
        rows = pl.ds(r * rb, rb)
        kt = min(ATTN_KEY_TILE, size)
        pv, lsum = None, None
        for c0 in range(0, size, kt):
            p = jnp.exp2((s_buf[:, c0:c0 + kt] - m_new).astype(BF16))
            p32 = p.astype(F32)
            for i in range(kt // LANES):
                t = p32[:, i * LANES:(i + 1) * LANES]
                lsum = t if lsum is None else lsum + t
            t = jnp.dot(p, v_ref[0, pl.ds(start + c0, kt), :], preferred_element_type=F32)
            pv = t if pv is None else pv + t
        acc_sc[rows, 0:LANES] = alpha * acc_sc[rows, 0:LANES] + pv
        acc_sc[rows, LANES:2 * LANES] = alpha * acc_sc[rows, LANES:2 * LANES] + lsum
        return m_new

    def stage(cur_buf, nxt_buf, nxt_start, nxt_size, cur_start, cur_size, ms, mxs):
        new_ms, new_mxs = [], []
        for r in range(n_rb):
            rows = pl.ds(r * rb, rb)
            new_ms.append(softmax_pv(r, cur_buf.at[rows, :], mxs[r], ms[r], cur_start, cur_size))
            s_n, mx_n = qk(r, nxt_start, nxt_size)
            nxt_buf[rows, 0:nxt_size] = s_n
            new_mxs.append(mx_n)
        return new_ms, new_mxs

    acc_sc[...] = jnp.zeros_like(acc_sc)
    chunks = [(j * tk, tk) for j in range(n_lat // tk)] + [(n_lat, n_ctx)]
    bufs = (sa_sc, sb_sc)
    ms = [jnp.full((rb, 1), -1e30, F32) for _ in range(n_rb)]
    mxs = []
    for r in range(n_rb):
        s, mx = qk(r, *chunks[0])
        sa_sc[pl.ds(r * rb, rb), 0:chunks[0][1]] = s
        mxs.append(mx)
    for j in range(len(chunks) - 1):
        ms, mxs = stage(bufs[j % 2], bufs[(j + 1) % 2], *chunks[j + 1], *chunks[j], ms, mxs)
    last = len(chunks) - 1
    for r in range(n_rb):
        softmax_pv(r, bufs[last % 2].at[pl.ds(r * rb, rb), :], mxs[r], ms[r], *chunks[last])
    acc = acc_sc[...]
    o = acc[:, 0:LANES] / jnp.sum(acc[:, LANES:2 * LANES], axis=-1, keepdims=True)

    if mode == "da":
        lam_ref, sub_ref = extra
        lam = lam_ref[...]
        lam_full = (jnp.exp(jnp.sum(lam[0:1] * lam[1:2], axis=-1, keepdims=True))
                    - jnp.exp(jnp.sum(lam[2:3] * lam[3:4], axis=-1, keepdims=True)) + lam_init)
        dlt = o[0:tq] - lam_full * o[tq:2 * tq]
        ms_d = jnp.mean(dlt * dlt, axis=-1, keepdims=True)
        o_ref[0] = (dlt * lax.rsqrt(ms_d + NORM_EPS) * sub_ref[...] * (1.0 - lam_init)).astype(o_ref.dtype)
    elif mode == "gq":
        o_ref[0] = jnp.concatenate([o[0:tq], o[tq:2 * tq]], axis=-1).astype(o_ref.dtype)
    else:
        o_ref[0] = o.astype(o_ref.dtype)


def _pick_tk(n_lat):
    for tk in (1024, 512, 256):
        if n_lat % tk == 0:
            return tk
    raise ValueError("latent length must be a multiple of 256")


def _attention(mode, q, k, v, n_lat, with_ctx, extra=(), lam_init=0.0):
    b, lt, _ = q.shape
    n_ctx = lt - n_lat
    if mode == "da":
        heads, qw, kw, ow, g = DA_HEADS, 128, 128, 128, 2
    elif mode == "gq":
        heads, qw, kw, ow, g = GQ_KV, 256, 128, 256, 2
    else:
        heads, qw, kw, ow, g = ML_HEADS, 256, 256, 128, 1
    tk = _pick_tk(n_lat)

    def call(ctx_only):
        if ctx_only:
            tq, q0, n_tiles, kv_rows, kv_blk = n_ctx, n_lat // n_ctx, 1, n_ctx, n_lat // n_ctx
        else:
            tq = ATTN_ROWS // g if n_lat % (ATTN_ROWS // g) == 0 else ROW_TILE
            q0, n_tiles, kv_rows, kv_blk = 0, n_lat // tq, lt, 0
        in_specs = [pl.BlockSpec((1, tq, qw), lambda i, h, t: (i, t + q0, h)),
                    pl.BlockSpec((1, kv_rows, kw), lambda i, h, t: (i, kv_blk, h)),
                    pl.BlockSpec((1, kv_rows, 128), lambda i, h, t: (i, kv_blk, h))]
        in_specs += [_const_spec(e.shape) for e in extra]
        return pl.pallas_call(
            functools.partial(_attn_kernel, mode=mode, tq=tq, tk=tk, n_lat=0 if ctx_only else n_lat,
                              n_extra=len(extra), lam_init=lam_init),
            grid=(b, heads, n_tiles),
            in_specs=in_specs,
            out_specs=pl.BlockSpec((1, tq, ow), lambda i, h, t: (i, t, h)),
            out_shape=jax.ShapeDtypeStruct((b, n_tiles * tq, BRANCH_W), BF16),
            scratch_shapes=[pltpu.VMEM((g * tq, 2 * LANES), F32),
                            pltpu.VMEM((g * tq, max(tk, n_ctx)), F32),
                            pltpu.VMEM((g * tq, max(tk, n_ctx)), F32)],
            compiler_params=_cparams(("parallel", "parallel", "arbitrary")),
            name="attn_" + mode + ("_ctx" if ctx_only else ""),
        )(q, k, v, *extra)

    return call(False), (call(True) if with_ctx else None)


def _gelu_tanh(x):
    return 0.5 * x * (1.0 + jnp.tanh(math.sqrt(2.0 / math.pi) * (x + 0.044715 * (x * x * x))))


def _merge_ffn_kernel(*refs, lat_tiles, with_ctx, n_stream):
    x = _stream_tile(refs[:n_stream], lat_tiles)
    refs = refs[n_stream:]
    (mod_ref, y_ref, gate_ref, wglu_ref, bglu_ref, wbr_ref, wout_ref,
     nrm_ref, win_ref, wffn_ref) = refs[:10]
    lat_refs = refs[10:13]
    ctx_refs = refs[13:16] if with_ctx else None
    o_ref, y_sc = refs[-2:]
    d = x.shape[-1]
    mod = mod_ref[0]
    per_tile = y_sc.shape[0] // S5_CHUNK
    for j in range(y_ref.shape[0]):
        for n in range(per_tile):
            y_sc[n * S5_CHUNK:(n + 1) * S5_CHUNK, j * LANES:(j + 1) * LANES] = (
                y_ref[j, 0, pl.ds(n, S5_CHUNK, stride=per_tile), :])
    g = _gelu_tanh(y_sc[...])
    glu = jax.nn.sigmoid(jnp.dot(g.astype(BF16), wglu_ref[...], preferred_element_type=F32)
                         + bglu_ref[...])
    branches = [(g * glu).astype(BF16)]
    for n in range(3):
        if with_ctx:
            is_ctx = pl.program_id(1) >= lat_tiles
            branches.append(jnp.where(is_ctx, ctx_refs[n][0], lat_refs[n][0]))
        else:
            branches.append(lat_refs[n][0])
    acc = None
    for n, br in enumerate(branches):
        term = gate_ref[0, :, n * d:(n + 1) * d].astype(F32) * jnp.dot(
            br, wbr_ref[n], preferred_element_type=F32)
        acc = term if acc is None else acc + term
    out = jnp.dot(acc.astype(BF16), wout_ref[...], preferred_element_type=F32)
    x = x + mod[:, 2 * d:3 * d] * out

    d_ff = wffn_ref.shape[0]
    shift, scale, g2 = mod[:, 3 * d:4 * d], mod[:, 4 * d:5 * d], mod[:, 5 * d:6 * d]
    ms = jnp.mean(x * x, axis=-1, keepdims=True)
    hb = ((x * lax.rsqrt(ms + NORM_EPS) * nrm_ref[...]) * (1.0 + scale) + shift).astype(BF16)
    gate = jnp.dot(hb, win_ref[:, 0:d_ff], preferred_element_type=F32)
    up = jnp.dot(hb, win_ref[:, d_ff:2 * d_ff], preferred_element_type=F32)
    act = (gate * jax.nn.sigmoid(gate) * up).astype(BF16)
    o_ref[0] = x + g2 * jnp.dot(act, wffn_ref[...], preferred_element_type=F32)


def _mod_index(layer, n_batch, lat_tiles):
    def idx(i, t):
        return (layer * 8 + jnp.where(t >= lat_tiles, n_batch, i), 0, 0)
    return idx


def _merge_ffn(stream, mod_rows, layer, n_lat, n_rows, y, attn, gates, wglu, bglu, wbr, wout,
               nrm_ffn, win, wffn):
    b, d = y.shape[1], wout.shape[0]
    tm = ROW_TILE
    lat_tiles = n_lat // tm
    with_ctx = n_rows > n_lat
    row = lambda i, t: (i, t, 0)
    lat_row = lambda i, t: (i, jnp.minimum(t, lat_tiles - 1), 0)
    x_specs, x_args = _stream_specs(stream, tm, lat_tiles)
    weights = [wglu, bglu, wbr, wout, nrm_ffn, win, wffn]
    in_specs = x_specs + [
        pl.BlockSpec((1, 1, mod_rows.shape[-1]), _mod_index(layer, b, lat_tiles)),
        pl.BlockSpec((BRANCH_W // LANES, 1, tm, LANES), lambda i, t: (0, i, t, 0)),
        pl.BlockSpec((1, tm, N_BRANCH * d), row)] + [_const_spec(w.shape) for w in weights]
    in_specs += [pl.BlockSpec((1, tm, BRANCH_W), lat_row)] * 3
    args = x_args + [mod_rows, y, gates] + weights + [a[0] for a in attn]
    if with_ctx:
        in_specs += [pl.BlockSpec((1, tm, BRANCH_W), lambda i, t: (i, 0, 0))] * 3
        args += [a[1] for a in attn]
    return pl.pallas_call(
        functools.partial(_merge_ffn_kernel, lat_tiles=lat_tiles, with_ctx=with_ctx,
                          n_stream=len(x_args)),
        grid=(b, n_rows // tm),
        in_specs=in_specs,
        out_specs=pl.BlockSpec((1, tm, d), row),
        out_shape=jax.ShapeDtypeStruct((b, n_rows, d), F32),
        scratch_shapes=[pltpu.VMEM((tm, BRANCH_W), F32)],
        compiler_params=_cparams(("parallel", "parallel")),
        name="merge_ffn",
    )(*args)


def _inproj_columns(d):
    de64, de128 = _deinterleave(64), _deinterleave(128)
    off_da = 512
    off_gq = off_da + 3 * DA_HEADS * DA_DV
    off_ml = off_gq + (GQ_HEADS + 2 * GQ_KV) * GQ_DH
    off_gate = off_ml + ML_QRANK + ML_KVRANK + ML_ROPE
    cols = [np.arange(512)]
    for part in range(2):
        for hd in range(DA_HEADS):
            for comp in range(2):
                cols.append(off_da + part * 512 + hd * 128 + comp * 64 + de64)
    cols.append(off_da + 1024 + np.arange(512))
    for hd in range(GQ_HEADS):
        cols.append(off_gq + hd * 128 + de128)
    for hd in range(GQ_KV):
        cols.append(off_gq + 512 + hd * 128 + de128)
    cols.append(off_gq + 768 + np.arange(256))
    cols.append(off_ml + np.arange(ML_QRANK + ML_KVRANK))
    cols.append(off_ml + ML_QRANK + ML_KVRANK + de64)
    cols.append(off_ml + ML_QRANK + ML_KVRANK + de64)
    cols.append(off_gate + np.arange(N_BRANCH * d))
    return np.concatenate(cols)


def _rope_table(n_lat, n_ctx):
    t = np.arange(n_lat)
    r = (t // GRID_W).astype(np.float32)
    col = (t % GRID_W).astype(np.float32)

    def angles(rot_dim):
        half = rot_dim // 2
        inv = jnp.asarray(ROPE_THETA, F32) ** (-jnp.arange(0, half, 2, dtype=F32) / half)
        return jnp.concatenate([jnp.asarray(r)[:, None] * inv, jnp.asarray(col)[:, None] * inv], axis=-1)

    a64, a128 = angles(64), angles(128)
    c64, s64 = jnp.cos(a64), jnp.sin(a64)
    c128, s128 = jnp.cos(a128), jnp.sin(a128)
    lat = jnp.concatenate([c64, c64, c64, c64, -s64, s64, -s64, s64, c128, c128, -s128, s128], axis=-1)
    ctx = jnp.concatenate([jnp.ones((n_ctx, 128), F32), jnp.zeros((n_ctx, 128), F32),
                           jnp.ones((n_ctx, 128), F32), jnp.zeros((n_ctx, 128), F32)], axis=-1)
    return jnp.concatenate([lat, ctx], axis=0)


def kernel(x, c, ctx, c_ctx, w_mod, b_mod, norm_mix, w_in, s5_lam_re, s5_lam_im, s5_log_step, s5_b_re, s5_b_im, s5_c_re, s5_c_im, s5_d, s5_w_glu, s5_b_glu, da_q_norm, da_k_norm, da_lam, da_subln, gq_q_norm, gq_k_norm, ml_cq_norm, ml_ckv_norm, ml_w_uq, ml_w_ukv, ml_q_norm, ml_k_norm, w_branch, w_out, norm_ffn, w_ffn_in, w_ffn_out):
    b, n_lat, d = x.shape
    n_ctx = ctx.shape[1]
    depth = w_mod.shape[0]
    assert n_ctx == ROW_TILE and n_lat % ROW_TILE == 0 and b < 8 and d == 1024

    cc = jnp.zeros((8, d), F32).at[:b].set(c).at[b].set(c_ctx)
    mod_rows = _modulation(cc, w_mod, b_mod).reshape(depth * 8, 1, 6 * d)
    rope = _rope_table(n_lat, n_ctx)
    xc = (x, ctx)

    de64, de128 = _deinterleave(64), _deinterleave(128)
    cols = _inproj_columns(d)
    uq_cols = np.concatenate([hd * 192 + np.arange(128) for hd in range(ML_HEADS)]
                             + [hd * 192 + 128 + de64 for hd in range(ML_HEADS)])
    ukv_cols = np.concatenate([hd * 256 + np.arange(128) for hd in range(ML_HEADS)]
                              + [hd * 256 + 128 + np.arange(128) for hd in range(ML_HEADS)])
    log2e = math.log2(math.e)
    da_scale, gq_scale = DA_DK ** -0.5 * log2e, GQ_DH ** -0.5 * log2e
    ml_scale = (ML_NOPE + ML_ROPE) ** -0.5 * log2e

    for i in range(depth):
        last = i == depth - 1
        lam_init = 0.8 - 0.6 * math.exp(-0.3 * i)
        w = w_in[i][:, cols].astype(BF16)
        wuq = ml_w_uq[i][:, uq_cols].astype(BF16)
        wukv = ml_w_ukv[i][:, ukv_cols].astype(BF16)
        gains = jnp.concatenate([
            jnp.tile(da_q_norm[i][de64], 8) * da_scale, jnp.tile(da_k_norm[i][de64], 8),
            jnp.tile(gq_q_norm[i][de128], 4) * gq_scale, jnp.tile(gq_k_norm[i][de128], 2),
            ml_cq_norm[i], ml_ckv_norm[i],
            jnp.tile(ml_q_norm[i][:ML_NOPE], 4) * ml_scale,
            jnp.tile(ml_q_norm[i][ML_NOPE:][de64], 4) * ml_scale,
            jnp.tile(ml_k_norm[i][:ML_NOPE], 4), jnp.tile(ml_k_norm[i][ML_NOPE:][de64], 2),
        ]).astype(F32)[None, :]

        (u, daq, dak, dav, gqq, gqk, gqv, mlq, mlk, mlv, gates) = _inproj(
            xc, mod_rows, i, b, n_lat, n_lat + n_ctx, norm_mix[i][None, :], gains, rope, w, wuq, wukv)

        s5_ops = _s5_prep(s5_lam_re[i], s5_lam_im[i], s5_log_step[i], s5_b_re[i], s5_b_im[i],
                          s5_c_re[i], s5_c_im[i], s5_d[i])
        y = _s5(u, n_lat, s5_ops)

        n_rows = n_lat if last else n_lat + n_ctx
        da = _attention("da", daq, dak, dav, n_lat, not last,
                        extra=(da_lam[i].astype(F32), da_subln[i].astype(F32)[None, :]),
                        lam_init=lam_init)
        gq = _attention("gq", gqq, gqk, gqv, n_lat, not last)
        ml = _attention("ml", mlq, mlk, mlv, n_lat, not last)

        xc = _merge_ffn(xc, mod_rows, i, n_lat, n_rows, y, (da, gq, ml), gates,
                        s5_w_glu[i].astype(BF16), s5_b_glu[i].astype(F32)[None, :],
                        w_branch[i].astype(BF16), w_out[i].astype(BF16),
                        norm_ffn[i][None, :], w_ffn_in[i].astype(BF16), w_ffn_out[i].astype(BF16))
    return xc
```
